```python
import jax, jax.numpy as jnp
from jax import lax
import numpy as np

D_MODEL = 1024
BATCH = 8
SEQ = 4096
DEPTH = 1

N_META = 16
EPS = 1e-6
NEG_INF = -1e30
MLA_HEADS = 16
Q_RANK = 256
KV_RANK = 128
NOPE_DIM = 64
ROPE_DIM = 32
V_DIM = 64
ROPE_THETA = 10000.0
Q_BLOCK = 128
SSM_HEADS = 16
SSM_HEAD_DIM = 64
D_INNER = SSM_HEADS * SSM_HEAD_DIM
SSM_GROUPS = 2
HEADS_PER_GROUP = SSM_HEADS // SSM_GROUPS
D_STATE = 64
CONV_WIDTH = 4
CONV_DIM = D_INNER + 2 * SSM_GROUPS * D_STATE
CHUNK = 128
N_EXPERT_GROUPS = 4
EXPERTS_PER_GROUP = 8
N_EXPERTS = N_EXPERT_GROUPS * EXPERTS_PER_GROUP
TOP_K = 2
D_EXPERT = 256
PROJ_SIZES = (Q_RANK, KV_RANK, ROPE_DIM, D_INNER, CONV_DIM, SSM_HEADS, D_MODEL, D_MODEL)
D_IN_PROJ = Q_RANK + KV_RANK + ROPE_DIM + D_INNER + CONV_DIM + SSM_HEADS + 2 * D_MODEL

kernel_name = "hybrid_mla_ssd_gated_hmoe"


def rmsnorm(x, g):
    xf = x.astype(jnp.float32)
    y = xf * lax.rsqrt(jnp.mean(xf * xf, axis=-1, keepdims=True) + EPS)
    return (y * g.astype(jnp.float32)).astype(x.dtype)


def pad_left(a, n):
    return jnp.pad(a, [(0, 0), (n, 0)] + [(0, 0)] * (a.ndim - 2))


def apply_rope(x, cos, sin):
    x1, x2 = jnp.split(x, 2, axis=-1)
    return jnp.concatenate([x1 * cos - x2 * sin, x2 * cos + x1 * sin], axis=-1).astype(x.dtype)


def mla_branch(cq, ckv, kr, q_norm, w_uq, kv_norm, w_ukv):
    b, L, _ = cq.shape
    pad = CHUNK - N_META
    Lp = L + pad
    nb = Lp // Q_BLOCK
    q = (rmsnorm(cq, q_norm) @ w_uq).reshape(b, L, MLA_HEADS, NOPE_DIM + ROPE_DIM)
    kv = (rmsnorm(ckv, kv_norm) @ w_ukv).reshape(b, L, MLA_HEADS, NOPE_DIM + V_DIM)
    q, kv, kr = pad_left(q, pad), pad_left(kv, pad), pad_left(kr, pad)
    key_pos = jnp.arange(Lp)
    pos = jnp.maximum(key_pos - pad, 0).astype(jnp.float32)
    inv = ROPE_THETA ** (-jnp.arange(0, ROPE_DIM, 2, dtype=jnp.float32) / ROPE_DIM)
    ang = pos[:, None] * inv[None, :]
    cos, sin = jnp.cos(ang), jnp.sin(ang)
    q_nope, q_rope = q[..., :NOPE_DIM], q[..., NOPE_DIM:]
    q_rope = apply_rope(q_rope, cos[None, :, None, :], sin[None, :, None, :])
    k_nope, v = kv[..., :NOPE_DIM], kv[..., NOPE_DIM:]
    k_rope = apply_rope(kr, cos[None], sin[None])
    scale = (NOPE_DIM + ROPE_DIM) ** -0.5
    qn_b = jnp.moveaxis(q_nope.reshape(b, nb, Q_BLOCK, MLA_HEADS, NOPE_DIM), 1, 0)
    qr_b = jnp.moveaxis(q_rope.reshape(b, nb, Q_BLOCK, MLA_HEADS, ROPE_DIM), 1, 0)

    def attend(args):
        qn, qr, blk = args
        qpos = blk * Q_BLOCK + jnp.arange(Q_BLOCK)
        s = jnp.einsum('bqhd,bkhd->bhqk', qn, k_nope) + jnp.einsum('bqhr,bkr->bhqk', qr, k_rope)
        s = s.astype(jnp.float32) * scale
        mask = (key_pos[None, :] <= qpos[:, None]) & (key_pos[None, :] >= pad)
        s = jnp.where(mask, s, NEG_INF)
        p = jax.nn.softmax(s, axis=-1).astype(v.dtype)
        return jnp.einsum('bhqk,bkhd->bqhd', p, v)

    o = lax.map(attend, (qn_b, qr_b, jnp.arange(nb)))
    o = jnp.moveaxis(o, 0, 1).reshape(b, Lp, MLA_HEADS * V_DIM)
    return o[:, pad:]


def ssd_branch(z, xbc, dt_raw, conv_w, conv_b, dt_bias, a_log, d_skip, norm_g):
    b, L, _ = xbc.shape
    pad = CHUNK - N_META
    Lp = L + pad
    nc = Lp // CHUNK
    xbc = lax.conv_general_dilated(xbc, conv_w[:, None, :], window_strides=(1,),
                                   padding=[(CONV_WIDTH - 1, 0)],
                                   dimension_numbers=('NWC', 'WIO', 'NWC'),
                                   feature_group_count=CONV_DIM) + conv_b
    xbc = jax.nn.silu(xbc)
    xs = xbc[..., :D_INNER]
    bm = xbc[..., D_INNER:D_INNER + SSM_GROUPS * D_STATE]
    cm = xbc[..., D_INNER + SSM_GROUPS * D_STATE:]
    dt = jax.nn.softplus((dt_raw + dt_bias).astype(jnp.float32))
    A = -jnp.exp(a_log.astype(jnp.float32)).reshape(SSM_GROUPS, HEADS_PER_GROUP)
    xs, bm, cm, dt = pad_left(xs, pad), pad_left(bm, pad), pad_left(cm, pad), pad_left(dt, pad)
    x = xs.reshape(b, nc, CHUNK, SSM_GROUPS, HEADS_PER_GROUP, SSM_HEAD_DIM)
    bm = bm.reshape(b, nc, CHUNK, SSM_GROUPS, D_STATE)
    cm = cm.reshape(b, nc, CHUNK, SSM_GROUPS, D_STATE)
    dt = dt.reshape(b, nc, CHUNK, SSM_GROUPS, HEADS_PER_GROUP)
    a = dt * A
    xdt = x * dt[..., None]
    a_cs = jnp.cumsum(a, axis=2)
    seg = a_cs[:, :, :, None] - a_cs[:, :, None, :]
    tril = jnp.tril(jnp.ones((CHUNK, CHUNK), dtype=bool))[:, :, None, None]
    lmat = jnp.exp(jnp.where(tril, seg, -jnp.inf))
    cb = jnp.einsum('bclgn,bcsgn->bclsg', cm, bm)
    y_diag = jnp.einsum('bclsg,bclsge,bcsgep->bclgep', cb, lmat, xdt)
    decay = jnp.exp(a_cs[:, :, -1:] - a_cs)
    states = jnp.einsum('bclgn,bclge,bclgep->bcgepn', bm, decay, xdt)
    chunk_decay = jnp.exp(a_cs[:, :, -1])

    def step(carry, inp):
        st, dec = inp
        return carry * dec[..., None, None] + st, carry

    init = jnp.zeros((b, SSM_GROUPS, HEADS_PER_GROUP, SSM_HEAD_DIM, D_STATE), states.dtype)
    _, prev = lax.scan(step, init, (jnp.moveaxis(states, 1, 0), jnp.moveaxis(chunk_decay, 1, 0)))
    prev = jnp.moveaxis(prev, 0, 1)
    y_off = jnp.einsum('bclgn,bcgepn,bclge->bclgep', cm, prev, jnp.exp(a_cs))
    y = y_diag + y_off + x * d_skip.reshape(SSM_GROUPS, HEADS_PER_GROUP)[:, :, None]
    y = y.reshape(b, Lp, D_INNER)[:, pad:]
    yz = (y * jax.nn.silu(z)).astype(jnp.float32).reshape(b, L, SSM_GROUPS, D_INNER // SSM_GROUPS)
    yz = yz * lax.rsqrt(jnp.mean(yz * yz, axis=-1, keepdims=True) + EPS)
    return (yz.reshape(b, L, D_INNER) * norm_g.astype(jnp.float32)).astype(z.dtype)


def hier_moe(v, w_group, b_group, w_expert, b_expert, w_gate, w_up, w_down):
    shp = v.shape
    vt = v.reshape(-1, D_MODEL)
    g_logits = (vt @ w_group + b_group).astype(jnp.float32)
    p_group = jax.nn.softmax(g_logits, axis=-1)
    g_idx = jnp.argmax(g_logits, axis=-1)
    p_g = jnp.take_along_axis(p_group, g_idx[:, None], axis=-1)
    e_logits = (vt @ w_expert + b_expert).astype(jnp.float32).reshape(-1, N_EXPERT_GROUPS, EXPERTS_PER_GROUP)
    e_logits = jnp.take_along_axis(e_logits, g_idx[:, None, None], axis=1)[:, 0]
    top_vals, top_idx = lax.top_k(e_logits, TOP_K)
    w = jax.nn.softmax(top_vals, axis=-1) * p_g
    ids = g_idx[:, None] * EXPERTS_PER_GROUP + top_idx
    comb = jnp.sum(jax.nn.one_hot(ids, N_EXPERTS, dtype=jnp.float32) * w[..., None], axis=1).astype(vt.dtype)
    out = jnp.zeros_like(vt)
    for gi in range(N_EXPERT_GROUPS):
        sl = slice(gi * EXPERTS_PER_GROUP, (gi + 1) * EXPERTS_PER_GROUP)
        h = jax.nn.silu(jnp.einsum('td,edf->tef', vt, w_gate[sl])) * jnp.einsum('td,edf->tef', vt, w_up[sl])
        out = out + jnp.einsum('tef,te,efd->td', h, comb[:, sl], w_down[sl])
    return out.reshape(shp)


def setup_inputs(seed: int = 0) -> dict:
    key = jax.random.key(seed)
    ks = jax.random.split(key, 32)

    def nrm(k, shape, scale):
        return jax.random.normal(k, shape, jnp.float32) * scale

    def gain(k, shape):
        return 1.0 + 0.01 * jax.random.normal(k, shape, jnp.float32)

    dt0 = jnp.exp(jax.random.uniform(ks[9], (DEPTH, SSM_HEADS), jnp.float32, np.log(1e-3), np.log(1e-1)))
    return {
        "x": nrm(ks[0], (BATCH, SEQ, D_MODEL), 1.0),
        "meta_tokens": nrm(ks[1], (N_META, D_MODEL), 1.0),
        "norm_mix": gain(ks[2], (DEPTH, D_MODEL)),
        "w_in": nrm(ks[3], (DEPTH, D_MODEL, D_IN_PROJ), D_MODEL ** -0.5),
        "mla_q_norm": gain(ks[4], (DEPTH, Q_RANK)),
        "mla_w_uq": nrm(ks[5], (DEPTH, Q_RANK, MLA_HEADS * (NOPE_DIM + ROPE_DIM)), Q_RANK ** -0.5),
        "mla_kv_norm": gain(ks[6], (DEPTH, KV_RANK)),
        "mla_w_ukv": nrm(ks[7], (DEPTH, KV_RANK, MLA_HEADS * (NOPE_DIM + V_DIM)), KV_RANK ** -0.5),
        "ssm_conv_w": nrm(ks[8], (DEPTH, CONV_WIDTH, CONV_DIM), CONV_WIDTH ** -0.5),
        "ssm_conv_b": nrm(ks[10], (DEPTH, CONV_DIM), 0.01),
        "ssm_dt_bias": dt0 + jnp.log(-jnp.expm1(-dt0)),
        "ssm_a_log": jnp.log(jax.random.uniform(ks[11], (DEPTH, SSM_HEADS), jnp.float32, 1.0, 16.0)),
        "ssm_d_skip": gain(ks[12], (DEPTH, SSM_HEADS)),
        "ssm_norm": gain(ks[13], (DEPTH, D_INNER)),
        "w_branch_attn": nrm(ks[14], (DEPTH, MLA_HEADS * V_DIM, D_MODEL), (MLA_HEADS * V_DIM) ** -0.5),
        "w_branch_ssm": nrm(ks[15], (DEPTH, D_INNER, D_MODEL), D_INNER ** -0.5),
        "w_out": nrm(ks[16], (DEPTH, D_MODEL, D_MODEL), D_MODEL ** -0.5),
        "norm_ffn": gain(ks[17], (DEPTH, D_MODEL)),
        "moe_w_group": nrm(ks[18], (DEPTH, D_MODEL, N_EXPERT_GROUPS), D_MODEL ** -0.5),
        "moe_b_group": nrm(ks[19], (DEPTH, N_EXPERT_GROUPS), 0.01),
        "moe_w_expert": nrm(ks[20], (DEPTH, D_MODEL, N_EXPERTS), D_MODEL ** -0.5),
        "moe_b_expert": nrm(ks[21], (DEPTH, N_EXPERTS), 0.01),
        "moe_w_gate": nrm(ks[22], (DEPTH, N_EXPERTS, D_MODEL, D_EXPERT), D_MODEL ** -0.5),
        "moe_w_up": nrm(ks[23], (DEPTH, N_EXPERTS, D_MODEL, D_EXPERT), D_MODEL ** -0.5),
        "moe_w_down": nrm(ks[24], (DEPTH, N_EXPERTS, D_EXPERT, D_MODEL), D_EXPERT ** -0.5),
        "norm_final": gain(ks[25], (D_MODEL,)),
    }


def reference(x, meta_tokens, norm_mix, w_in, mla_q_norm, mla_w_uq, mla_kv_norm, mla_w_ukv,
              ssm_conv_w, ssm_conv_b, ssm_dt_bias, ssm_a_log, ssm_d_skip, ssm_norm,
              w_branch_attn, w_branch_ssm, w_out, norm_ffn, moe_w_group, moe_b_group,
              moe_w_expert, moe_b_expert, moe_w_gate, moe_w_up, moe_w_down, norm_final):
    b = x.shape[0]
    meta = jnp.broadcast_to(meta_tokens.astype(x.dtype)[None], (b, N_META, D_MODEL))
    h = jnp.concatenate([meta, x], axis=1)
    split_idx = list(np.cumsum(PROJ_SIZES)[:-1])
    for l in range(DEPTH):
        u = rmsnorm(h, norm_mix[l])
        proj = u @ w_in[l]
        cq, ckv, kr, z, xbc, dt_raw, g_attn, g_ssm = jnp.split(proj, split_idx, axis=-1)
        o_attn = mla_branch(cq, ckv, kr, mla_q_norm[l], mla_w_uq[l], mla_kv_norm[l], mla_w_ukv[l])
        o_ssm = ssd_branch(z, xbc, dt_raw, ssm_conv_w[l], ssm_conv_b[l], ssm_dt_bias[l],
                           ssm_a_log[l], ssm_d_skip[l], ssm_norm[l])
        merged = (jax.nn.sigmoid(g_attn) * (o_attn @ w_branch_attn[l])
                  + jax.nn.sigmoid(g_ssm) * (o_ssm @ w_branch_ssm[l]))
        h = h + (merged @ w_out[l]).astype(h.dtype)
        h = h + hier_moe(rmsnorm(h, norm_ffn[l]), moe_w_group[l], moe_b_group[l], moe_w_expert[l],
                         moe_b_expert[l], moe_w_gate[l], moe_w_up[l], moe_w_down[l]).astype(h.dtype)
    return rmsnorm(h, norm_final)[:, N_META:]
```

```python
import functools

import numpy as np
import jax
import jax.numpy as jnp
from jax import lax
from jax.experimental import pallas as pl
from jax.experimental.pallas import tpu as pltpu

F32 = jnp.float32
BF16 = jnp.bfloat16

D_MODEL = 1024
N_META = 16
EPS = 1e-6
NEG = -1e30
MLA_HEADS = 16
Q_RANK = 256
KV_RANK = 128
NOPE_DIM = 64
ROPE_DIM = 32
V_DIM = 64
ROPE_THETA = 10000.0
SSM_HEADS = 16
SSM_HEAD_DIM = 64
D_INNER = SSM_HEADS * SSM_HEAD_DIM
SSM_GROUPS = 2
D_STATE = 64
CONV_WIDTH = 4
CONV_DIM = D_INNER + 2 * SSM_GROUPS * D_STATE
N_EXPERT_GROUPS = 4
EXPERTS_PER_GROUP = 8
N_EXPERTS = N_EXPERT_GROUPS * EXPERTS_PER_GROUP
D_EXPERT = 256

LANES = 128
META_BLOCK = 128
META_PAD = META_BLOCK - N_META
HEAD_SLOT = 128
EXPERT_LANE0 = 32
DT_LANE0 = 96
VMEM_LIMIT = 48 * 1024 * 1024

A_CQ, A_CKV, A_SMALL, A_Z, A_XBC, A_GA, A_GS = 256, 128, 256, D_INNER, CONV_DIM, D_MODEL, D_MODEL
A_OFFS = np.cumsum([0, A_CQ, A_CKV, A_SMALL, A_Z, A_XBC, A_GA, A_GS])


def _cparams(sem):
    return pltpu.CompilerParams(dimension_semantics=sem, vmem_limit_bytes=VMEM_LIMIT)


def _rms(x, g):
    return x * lax.rsqrt(jnp.mean(x * x, axis=-1, keepdims=True) + EPS) * g


def _sigmoid(x):
    return 1.0 / (1.0 + jnp.exp(-x))


def _dot(a, b):
    return jnp.dot(a, b, preferred_element_type=F32)


def _dot_nt(a, b):
    return lax.dot_general(a, b, (((1,), (1,)), ((), ())), preferred_element_type=F32)


def _dot_tn(a, b):
    return lax.dot_general(a, b, (((0,), (0,)), ((), ())), preferred_element_type=F32)


def _split3(x):
    hi = x.astype(BF16)
    r1 = x - hi.astype(F32)
    mid = r1.astype(BF16)
    lo = (r1 - mid.astype(F32)).astype(BF16)
    return hi, mid, lo


def _dot_exact_rhs(x, m):
    hi, mid, lo = _split3(x)
    return _dot(hi, m) + _dot(mid, m) + _dot(lo, m)


def _dot_exact_lhs(m, x):
    hi, mid, lo = _split3(x)
    return _dot(m, hi) + _dot(m, mid) + _dot(m, lo)


def _inproj_body(x_ref, g_ref, w_ref, cq_ref, ckv_ref, small_ref, sz_ref, xbc_ref, ga_ref, gs_ref):
    u = _rms(x_ref[...], g_ref[...]).astype(BF16)

    def mm(i):
        return _dot(u, w_ref[:, A_OFFS[i]:A_OFFS[i + 1]])

    cq_ref[...] = mm(0).astype(BF16)
    ckv_ref[...] = mm(1).astype(BF16)
    small_ref[...] = mm(2)
    z = mm(3)
    sz_ref[...] = (z * _sigmoid(z)).astype(BF16)
    xbc_ref[...] = mm(4).astype(BF16)
    ga_ref[...] = _sigmoid(mm(5)).astype(BF16)
    gs_ref[...] = _sigmoid(mm(6)).astype(BF16)


def _inproj(x2d, g, w, tm):
    r = x2d.shape[0]
    widths = (A_CQ, A_CKV, A_SMALL, A_Z, A_XBC, A_GA, A_GS)
    dtypes = (BF16, BF16, F32, BF16, BF16, BF16, BF16)
    return pl.pallas_call(
        _inproj_body,
        grid=(r // tm,),
        in_specs=[
            pl.BlockSpec((tm, D_MODEL), lambda i: (i, 0)),
            pl.BlockSpec((1, D_MODEL), lambda i: (0, 0)),
            pl.BlockSpec(w.shape, lambda i: (0, 0), pipeline_mode=pl.Buffered(1)),
        ],
        out_specs=[pl.BlockSpec((tm, n), lambda i: (i, 0)) for n in widths],
        out_shape=[jax.ShapeDtypeStruct((r, n), dt) for n, dt in zip(widths, dtypes)],
        compiler_params=_cparams(("parallel",)),
        name="inproj",
    )(x2d, g, w)


def _mla_prep_body(cq_ref, ckv_ref, small_ref, qn_ref, kvn_ref, wq_ref, wk_ref, wv_ref,
                   cqt_ref, sqt_ref, ckt_ref, skt_ref, q_ref, k_ref, v_ref):
    cn = _rms(cq_ref[...].astype(F32), qn_ref[...]).astype(BF16)
    kn = _rms(ckv_ref[...].astype(F32), kvn_ref[...]).astype(BF16)
    krp = small_ref[:, 0:LANES] * ckt_ref[...] + small_ref[:, LANES:2 * LANES] * skt_ref[...]
    krp2 = jnp.concatenate([krp, krp], axis=1)
    cqt = cqt_ref[...]
    sqt = sqt_ref[...]
    half = MLA_HEADS * HEAD_SLOT
    for hp in range(MLA_HEADS // 2):
        lo, hi = hp * 2 * HEAD_SLOT, (hp + 1) * 2 * HEAD_SLOT
        qa = _dot(cn, wq_ref[:, lo:hi])
        qb = _dot(cn, wq_ref[:, half + lo:half + hi])
        q_ref[:, lo:hi] = (qa * cqt + qb * sqt).astype(BF16)
        k_ref[:, lo:hi] = (_dot(kn, wk_ref[:, lo:hi]) + krp2).astype(BF16)
    v_ref[...] = _dot(kn, wv_ref[...]).astype(BF16)


def _mla_prep(cq, ckv, small, qn, kvn, wq, wk, wv, tabs, tm, seq_blocks):
    r = cq.shape[0]
    cqt, sqt, ckt, skt = tabs
    row = lambda n: pl.BlockSpec((tm, n), lambda i: (i, 0))
    tab = lambda n: pl.BlockSpec((tm, n), lambda i: (i % seq_blocks, 0))
    full = lambda a: pl.BlockSpec(a.shape, lambda i: (0, 0))
    return pl.pallas_call(
        _mla_prep_body,
        grid=(r // tm,),
        in_specs=[row(Q_RANK), row(KV_RANK), row(A_SMALL), full(qn), full(kvn), full(wq), full(wk), full(wv),
                  tab(2 * HEAD_SLOT), tab(2 * HEAD_SLOT), tab(HEAD_SLOT), tab(HEAD_SLOT)],
        out_specs=[row(MLA_HEADS * HEAD_SLOT), row(MLA_HEADS * HEAD_SLOT), row(MLA_HEADS * V_DIM)],
        out_shape=[jax.ShapeDtypeStruct((r, MLA_HEADS * HEAD_SLOT), BF16),
                   jax.ShapeDtypeStruct((r, MLA_HEADS * HEAD_SLOT), BF16),
                   jax.ShapeDtypeStruct((r, MLA_HEADS * V_DIM), BF16)],
        compiler_params=_cparams(("parallel",)),
        name="mla_prep",
    )(cq, ckv, small, qn, kvn, wq, wk, wv, cqt, sqt, ckt, skt)


def _flash_body(q_ref, k_ref, v_ref, km_ref, vm_ref, o_ref, *, tq, tk, seq):
    lane = lax.broadcasted_iota(jnp.int32, (1, LANES), 1)
    row_i = lax.broadcasted_iota(jnp.int32, (tq, tk), 0)
    col_i = lax.broadcasted_iota(jnp.int32, (tq, tk), 1)

    def qblock(qi, _):
        q0 = pl.multiple_of(qi * tq, tq)
        outs = []
        for hh in range(2):
            hs = slice(hh * HEAD_SLOT, (hh + 1) * HEAD_SLOT)
            q = q_ref[pl.ds(q0, tq), hs]
            s = jnp.where(lane >= META_PAD, _dot_nt(q, km_ref[:, hs]), NEG)
            m = jnp.max(s, axis=1, keepdims=True)
            p = jnp.exp(s - m)
            l = jnp.sum(p, axis=1, keepdims=True)
            acc = _dot(p.astype(BF16), vm_ref[...])

            def kvstep(kj, carry, masked):
                m, l, acc = carry
                k0 = pl.multiple_of(kj * tk, tk)
                s = _dot_nt(q, k_ref[pl.ds(k0, tk), hs])
                if masked:
                    s = jnp.where(q0 + row_i >= k0 + col_i, s, NEG)
                m_new = jnp.maximum(m, jnp.max(s, axis=1, keepdims=True))
                alpha = jnp.exp(m - m_new)
                p = jnp.exp(s - m_new)
                l = alpha * l + jnp.sum(p, axis=1, keepdims=True)
                acc = alpha * acc + _dot(p.astype(BF16), v_ref[pl.ds(k0, tk), :])
                return m_new, l, acc

            nfull = qi * (tq // tk)
            carry = lax.fori_loop(0, nfull, functools.partial(kvstep, masked=False), (m, l, acc))
            for d in range(tq // tk):
                carry = kvstep(nfull + d, carry, True)
            m, l, acc = carry
            outs.append(acc / l)
        o_ref[pl.ds(q0, tq), :] = jnp.where(lane < V_DIM, outs[0], outs[1]).astype(BF16)
        return 0

    lax.fori_loop(0, seq // tq, qblock, 0)


def _flash(q, k, v, kmeta, vmeta, tq, tk):
    b, seq, _ = q.shape
    pairs = MLA_HEADS // 2
    return pl.pallas_call(
        functools.partial(_flash_body, tq=tq, tk=tk, seq=seq),
        grid=(b, pairs),
        in_specs=[
            pl.BlockSpec((None, seq, 2 * HEAD_SLOT), lambda i, p: (i, 0, p)),
            pl.BlockSpec((None, seq, 2 * HEAD_SLOT), lambda i, p: (i, 0, p)),
            pl.BlockSpec((None, seq, 2 * V_DIM), lambda i, p: (i, 0, p)),
            pl.BlockSpec((META_BLOCK, 2 * HEAD_SLOT), lambda i, p: (0, p)),
            pl.BlockSpec((META_BLOCK, 2 * V_DIM), lambda i, p: (0, p)),
        ],
        out_specs=pl.BlockSpec((None, seq, 2 * V_DIM), lambda i, p: (i, 0, p)),
        out_shape=jax.ShapeDtypeStruct((b, seq, MLA_HEADS * V_DIM), BF16),
        compiler_params=_cparams(("parallel", "parallel")),
        name="flash",
    )(q, k, v, kmeta, vmeta)


HALO = 8


def _ssd_body(xbc_ref, halo_ref, mh_ref, small_ref, sz_ref, cw_ref, cb_ref, dtb_ref, aneg_ref, dsk_ref,
              ng_ref, init_ref, o_ref, fin_ref, scr_ref, st_ref, y_ref, *, q, n_pad):
    c = pl.program_id(1)
    nc = pl.num_programs(1)

    @pl.when(c == 0)
    def _():
        st_ref[...] = init_ref[...]
        scr_ref[0:HALO, :] = mh_ref[...].astype(F32)

    @pl.when(c > 0)
    def _():
        scr_ref[0:HALO, :] = halo_ref[...].astype(F32)

    scr_ref[HALO:HALO + q, :] = xbc_ref[...].astype(F32)

    xc = cb_ref[...] + cw_ref[0:1, :] * scr_ref[pl.ds(HALO - 3, q), :]
    for w in range(1, CONV_WIDTH):
        xc = xc + cw_ref[w:w + 1, :] * scr_ref[pl.ds(HALO - 3 + w, q), :]
    xc = xc * _sigmoid(xc)

    lane = lax.broadcasted_iota(jnp.int32, (q, LANES), 1)
    dt = small_ref[:, LANES:2 * LANES] + dtb_ref[...]
    dt = jnp.maximum(dt, 0.0) + jnp.log1p(jnp.exp(-jnp.abs(dt)))
    dt = jnp.where((lane >= DT_LANE0) & (lane < DT_LANE0 + SSM_HEADS), dt, 0.0)
    if n_pad:
        rowv = lax.broadcasted_iota(jnp.int32, (q, 1), 0) >= n_pad
        xc = jnp.where(rowv, xc, 0.0)
        dt = jnp.where(rowv, dt, 0.0)

    xs = xc[:, :D_INNER]
    bm = xc[:, D_INNER:D_INNER + LANES]
    cm = xc[:, D_INNER + LANES:]

    a = dt * aneg_ref[...]
    ri = lax.broadcasted_iota(jnp.int32, (q, q), 0)
    ci = lax.broadcasted_iota(jnp.int32, (q, q), 1)
    tril = ri >= ci
    a_cs = _dot_exact_lhs(tril.astype(BF16), a)
    a_cs_t = a_cs.T

    er = lax.broadcasted_iota(jnp.int32, (LANES, D_INNER), 0)
    ec = lax.broadcasted_iota(jnp.int32, (LANES, D_INNER), 1)
    expand = (er - DT_LANE0 == ec // SSM_HEAD_DIM).astype(BF16)
    dt_x = _dot_exact_rhs(dt, expand)
    acs_x = _dot_exact_rhs(a_cs, expand)
    a_last = acs_x[q - 1:q, :]

    xdt = xs * dt_x
    xdt_b = xdt.astype(BF16)
    xd_b = (xdt * jnp.exp(a_last - acs_x)).astype(BF16)

    bm_b = bm.astype(BF16)
    cm_b = cm.astype(BF16)
    lane_q = lax.broadcasted_iota(jnp.int32, (q, LANES), 1)

    for g in range(SSM_GROUPS):
        in_g = (lane_q >= g * D_STATE) & (lane_q < (g + 1) * D_STATE)
        cb = _dot_nt(jnp.where(in_g, cm, 0.0).astype(BF16), bm_b)
        for pr in range(SSM_HEADS // SSM_GROUPS // 2):
            pair = g * (SSM_HEADS // SSM_GROUPS // 2) + pr
            rhs = xdt_b[:, pair * LANES:(pair + 1) * LANES]
            ys = []
            for hh in range(2):
                e = DT_LANE0 + 2 * pair + hh
                seg = a_cs[:, e:e + 1] - a_cs_t[e:e + 1, :]
                lmat = jnp.exp(jnp.where(tril, seg, NEG))
                ys.append(_dot((cb * lmat).astype(BF16), rhs))
            y_ref[:, pair * LANES:(pair + 1) * LANES] = jnp.where(lane_q < SSM_HEAD_DIM, ys[0], ys[1])

    sr = lax.broadcasted_iota(jnp.int32, (LANES, D_INNER), 0)
    sc = lax.broadcasted_iota(jnp.int32, (LANES, D_INNER), 1)
    same_group = (sr // D_STATE) == (sc // (D_INNER // SSM_GROUPS))
    prev = st_ref[...]
    y_off = _dot(cm_b, prev.astype(BF16)) * jnp.exp(acs_x)
    st_new = prev * jnp.exp(a_last) + jnp.where(same_group, _dot_tn(bm_b, xd_b), 0.0)
    st_ref[...] = st_new

    @pl.when(c == nc - 1)
    def _():
        fin_ref[...] = st_new

    y = y_ref[...] + y_off + xs * dsk_ref[...]
    yz = y * sz_ref[...].astype(F32)
    half = D_INNER // SSM_GROUPS
    outs = []
    for g in range(SSM_GROUPS):
        part = yz[:, g * half:(g + 1) * half]
        outs.append(part * lax.rsqrt(jnp.mean(part * part, axis=-1, keepdims=True) + EPS))
    o_ref[...] = (jnp.concatenate(outs, axis=1) * ng_ref[...]).astype(BF16)


def _ssd(xbc, small, sz, meta_halo, init_state, cw, cb, dtb, aneg, dsk, ng, q, n_pad):
    b, seq, _ = xbc.shape
    nc = seq // q
    hb = q // HALO
    full = lambda a: pl.BlockSpec(a.shape, lambda i, c: (0,) * a.ndim)
    return pl.pallas_call(
        functools.partial(_ssd_body, q=q, n_pad=n_pad),
        grid=(b, nc),
        in_specs=[
            pl.BlockSpec((None, q, CONV_DIM), lambda i, c: (i, c, 0)),
            pl.BlockSpec((None, HALO, CONV_DIM), lambda i, c: (i, jnp.maximum(c * hb - 1, 0), 0)),
            full(meta_halo),
            pl.BlockSpec((None, q, A_SMALL), lambda i, c: (i, c, 0)),
            pl.BlockSpec((None, q, D_INNER), lambda i, c: (i, c, 0)),
            full(cw), full(cb), full(dtb), full(aneg), full(dsk), full(ng), full(init_state),
        ],
        out_specs=[
            pl.BlockSpec((None, q, D_INNER), lambda i, c: (i, c, 0)),
            pl.BlockSpec((None, LANES, D_INNER), lambda i, c: (i, 0, 0)),
        ],
        out_shape=[jax.ShapeDtypeStruct((b, seq, D_INNER), BF16),
                   jax.ShapeDtypeStruct((b, LANES, D_INNER), F32)],
        scratch_shapes=[pltpu.VMEM((HALO + q, CONV_DIM), F32),
                        pltpu.VMEM((LANES, D_INNER), F32),
                        pltpu.VMEM((q, D_INNER), F32)],
        compiler_params=_cparams(("parallel", "arbitrary")),
        name="ssd",
    )(xbc, xbc, meta_halo, small, sz, cw, cb, dtb, aneg, dsk, ng, init_state)


def _route(logits):
    lane = lax.broadcasted_iota(jnp.int32, logits.shape, 1).astype(F32)
    gl = jnp.where(lane < N_EXPERT_GROUPS, logits, NEG)
    gmax = jnp.max(gl, axis=1, keepdims=True)
    gidx = jnp.min(jnp.where(gl == gmax, lane, float(LANES)), axis=1, keepdims=True)
    p_g = 1.0 / jnp.sum(jnp.exp(gl - gmax), axis=1, keepdims=True)
    lo = EXPERT_LANE0 + EXPERTS_PER_GROUP * gidx
    el = jnp.where((lane >= lo) & (lane < lo + EXPERTS_PER_GROUP), logits, NEG)
    t1 = jnp.max(el, axis=1, keepdims=True)
    i1 = jnp.min(jnp.where(el == t1, lane, float(LANES)), axis=1, keepdims=True)
    el2 = jnp.where(lane == i1, NEG, el)
    t2 = jnp.max(el2, axis=1, keepdims=True)
    i2 = jnp.min(jnp.where(el2 == t2, lane, float(LANES)), axis=1, keepdims=True)
    e21 = jnp.exp(t2 - t1)
    w1 = p_g / (1.0 + e21)
    w2 = w1 * e21
    return jnp.where(lane == i1, w1, jnp.where(lane == i2, w2, 0.0))


def _merge_body(x_ref, oa_ref, os_ref, ga_ref, gs_ref, wa_ref, ws_ref, wo_ref, nf_ref, wr_ref, br_ref,
                h2_ref, vb_ref, comb_ref):
    merged = (ga_ref[...].astype(F32) * _dot(oa_ref[...], wa_ref[...])
              + gs_ref[...].astype(F32) * _dot(os_ref[...], ws_ref[...]))
    h2 = x_ref[...] + _dot(merged.astype(BF16), wo_ref[...])
    h2_ref[...] = h2
    vb = _rms(h2, nf_ref[...]).astype(BF16)
    vb_ref[...] = vb
    comb_ref[...] = _route(_dot(vb, wr_ref[...]) + br_ref[...])


def _merge(x2d, oa, osm, ga, gs, wa, ws, wo, nf, wr, br, tm):
    r = x2d.shape[0]
    row = lambda n: pl.BlockSpec((tm, n), lambda i: (i, 0))
    full = lambda a: pl.BlockSpec(a.shape, lambda i: (0, 0))
    return pl.pallas_call(
        _merge_body,
        grid=(r // tm,),
        in_specs=[row(D_MODEL)] * 5 + [full(wa), full(ws), full(wo), full(nf), full(wr), full(br)],
        out_specs=[row(D_MODEL), row(D_MODEL), row(LANES)],
        out_shape=[jax.ShapeDtypeStruct((r, D_MODEL), F32),
                   jax.ShapeDtypeStruct((r, D_MODEL), BF16),
                   jax.ShapeDtypeStruct((r, LANES), F32)],
        compiler_params=_cparams(("parallel",)),
        name="merge",
    )(x2d, oa, osm, ga, gs, wa, ws, wo, nf, wr, br)


def _moe_body(vb_ref, comb_ref, h2_ref, wg_ref, wu_ref, wd_ref, nf_ref, out_ref, acc_ref):
    e = pl.program_id(1)

    @pl.when(e == 0)
    def _():
        acc_ref[...] = jnp.zeros_like(acc_ref)

    vb = vb_ref[...]
    g = _dot(vb, wg_ref[...])
    u = _dot(vb, wu_ref[...])
    lane = lax.broadcasted_iota(jnp.int32, comb_ref.shape, 1)
    cw = jnp.sum(jnp.where(lane == EXPERT_LANE0 + e, comb_ref[...], 0.0), axis=1, keepdims=True)
    hid = (g * _sigmoid(g) * u * cw).astype(BF16)
    acc_ref[...] += _dot(hid, wd_ref[...])

    @pl.when(e == N_EXPERTS - 1)
    def _():
        out_ref[...] = _rms(h2_ref[...] + acc_ref[...], nf_ref[...])


def _moe(vb, comb, h2, wg, wu, wd, nf, tm):
    r = vb.shape[0]
    return pl.pallas_call(
        _moe_body,
        grid=(r // tm, N_EXPERTS),
        in_specs=[
            pl.BlockSpec((tm, D_MODEL), lambda i, e: (i, 0)),
            pl.BlockSpec((tm, LANES), lambda i, e: (i, 0)),
            pl.BlockSpec((tm, D_MODEL), lambda i, e: (i, 0)),
            pl.BlockSpec((None, D_MODEL, D_EXPERT), lambda i, e: (e, 0, 0)),
            pl.BlockSpec((None, D_MODEL, D_EXPERT), lambda i, e: (e, 0, 0)),
            pl.BlockSpec((None, D_EXPERT, D_MODEL), lambda i, e: (e, 0, 0)),
            pl.BlockSpec((1, D_MODEL), lambda i, e: (0, 0)),
        ],
        out_specs=pl.BlockSpec((tm, D_MODEL), lambda i, e: (i, 0)),
        out_shape=jax.ShapeDtypeStruct((r, D_MODEL), F32),
        scratch_shapes=[pltpu.VMEM((tm, D_MODEL), F32)],
        compiler_params=_cparams(("parallel", "arbitrary")),
        name="moe",
    )(vb, comb, h2, wg, wu, wd, nf)


def _prep_w_in(w):
    o = np.cumsum([0, Q_RANK, KV_RANK, ROPE_DIM, D_INNER, CONV_DIM, SSM_HEADS, D_MODEL, D_MODEL])
    cq, ckv, kr, z, xbc, dt, ga, gs = (w[:, o[i]:o[i + 1]] for i in range(8))
    hr = ROPE_DIM // 2
    zeros = lambda n: jnp.zeros((w.shape[0], n), w.dtype)
    kra = jnp.concatenate([zeros(NOPE_DIM), kr, zeros(HEAD_SLOT - NOPE_DIM - ROPE_DIM)], axis=1)
    krb = jnp.concatenate([zeros(NOPE_DIM), -kr[:, hr:], kr[:, :hr], dt,
                           zeros(HEAD_SLOT - DT_LANE0 - SSM_HEADS)], axis=1)
    return jnp.concatenate([cq, ckv, kra, krb, z, xbc, ga, gs], axis=1).astype(BF16)


def _prep_w_uq(w):
    w = w.reshape(Q_RANK, MLA_HEADS, NOPE_DIM + ROPE_DIM)
    hr = ROPE_DIM // 2
    nope, r1, r2 = w[..., :NOPE_DIM], w[..., NOPE_DIM:NOPE_DIM + hr], w[..., NOPE_DIM + hr:]
    z = jnp.zeros((Q_RANK, MLA_HEADS, HEAD_SLOT - NOPE_DIM - ROPE_DIM), w.dtype)
    wa = jnp.concatenate([nope, r1, r2, z], axis=-1).reshape(Q_RANK, -1)
    wb = jnp.concatenate([jnp.zeros_like(nope), -r2, r1, z], axis=-1).reshape(Q_RANK, -1)
    return jnp.concatenate([wa, wb], axis=1).astype(BF16)


def _prep_w_ukv(w):
    w = w.reshape(KV_RANK, MLA_HEADS, NOPE_DIM + V_DIM)
    kn, v = w[..., :NOPE_DIM], w[..., NOPE_DIM:]
    wk = jnp.concatenate([kn, jnp.zeros((KV_RANK, MLA_HEADS, HEAD_SLOT - NOPE_DIM), w.dtype)], axis=-1)
    return wk.reshape(KV_RANK, -1).astype(BF16), v.reshape(KV_RANK, -1).astype(BF16)


def _rope_tables(pos):
    inv = ROPE_THETA ** (-jnp.arange(0, ROPE_DIM, 2, dtype=F32) / ROPE_DIM)
    ang = pos.astype(F32)[:, None] * inv[None, :]
    cos, sin = jnp.cos(ang), jnp.sin(ang)
    n = pos.shape[0]
    pad = jnp.zeros((n, HEAD_SLOT - NOPE_DIM - ROPE_DIM), F32)
    scale = (NOPE_DIM + ROPE_DIM) ** -0.5
    cq = jnp.concatenate([jnp.ones((n, NOPE_DIM), F32), cos, cos, pad], axis=1) * scale
    sq = jnp.concatenate([jnp.zeros((n, NOPE_DIM), F32), sin, sin, pad], axis=1) * scale
    ck = jnp.concatenate([jnp.zeros((n, NOPE_DIM), F32), cos, cos, pad], axis=1)
    sk = jnp.concatenate([jnp.zeros((n, NOPE_DIM), F32), sin, sin, pad], axis=1)
    return jnp.tile(cq, (1, 2)), jnp.tile(sq, (1, 2)), ck, sk


def _head_lanes(v):
    return jnp.zeros((1, LANES), F32).at[0, DT_LANE0:DT_LANE0 + SSM_HEADS].set(v.astype(F32))


def _pick(n, prefs):
    for t in prefs:
        if n % t == 0:
            return t
    raise ValueError(f"no tile for {n}")


def kernel(x, meta_tokens, norm_mix, w_in, mla_q_norm, mla_w_uq, mla_kv_norm, mla_w_ukv, ssm_conv_w, ssm_conv_b, ssm_dt_bias, ssm_a_log, ssm_d_skip, ssm_norm, w_branch_attn, w_branch_ssm, w_out, norm_ffn, moe_w_group, moe_b_group, moe_w_expert, moe_b_expert, moe_w_gate, moe_w_up, moe_w_down, norm_final):
    b, seq, _ = x.shape
    assert w_in.shape[0] == 1, "one layer"
    rows = b * seq
    x2d = x.reshape(rows, D_MODEL)
    tm = _pick(seq, (512, 256, 128))
    tq = _pick(seq, (512, 256, 128))
    tk = _pick(tq, (256, 128))
    chunk = 128

    w_in_r = _prep_w_in(w_in[0])
    wq = _prep_w_uq(mla_w_uq[0])
    wk, wv = _prep_w_ukv(mla_w_ukv[0])
    g_mix = norm_mix[0].reshape(1, D_MODEL)
    qn = mla_q_norm[0].reshape(1, Q_RANK)
    kvn = mla_kv_norm[0].reshape(1, KV_RANK)
    cw = ssm_conv_w[0]
    cb = ssm_conv_b[0].reshape(1, CONV_DIM)
    dtb = _head_lanes(ssm_dt_bias[0])
    aneg = _head_lanes(-jnp.exp(ssm_a_log[0].astype(F32)))
    dsk = jnp.repeat(ssm_d_skip[0].astype(F32), SSM_HEAD_DIM).reshape(1, D_INNER)
    ng = ssm_norm[0].reshape(1, D_INNER)
    wa = w_branch_attn[0].astype(BF16)
    ws = w_branch_ssm[0].astype(BF16)
    wo = w_out[0].astype(BF16)
    nffn = norm_ffn[0].reshape(1, D_MODEL)
    wr = jnp.zeros((D_MODEL, LANES), F32)
    wr = wr.at[:, :N_EXPERT_GROUPS].set(moe_w_group[0]).at[:, EXPERT_LANE0:EXPERT_LANE0 + N_EXPERTS].set(moe_w_expert[0])
    wr = wr.astype(BF16)
    br = jnp.zeros((1, LANES), F32)
    br = br.at[0, :N_EXPERT_GROUPS].set(moe_b_group[0]).at[0, EXPERT_LANE0:EXPERT_LANE0 + N_EXPERTS].set(moe_b_expert[0])
    wg = moe_w_gate[0].astype(BF16)
    wu = moe_w_up[0].astype(BF16)
    wd = moe_w_down[0].astype(BF16)
    nfin = norm_final.reshape(1, D_MODEL)

    meta_blk = jnp.concatenate([jnp.zeros((META_PAD, D_MODEL), F32), meta_tokens.astype(F32)], axis=0)
    m_cq, m_ckv, m_small, m_sz, m_xbc, _, _ = _inproj(meta_blk, g_mix, w_in_r, META_BLOCK)
    meta_pos = jnp.maximum(jnp.arange(META_BLOCK) - META_PAD, 0)
    _, kmeta, vmeta = _mla_prep(m_cq, m_ckv, m_small, qn, kvn, wq, wk, wv, _rope_tables(meta_pos), META_BLOCK, 1)
    zero_state = jnp.zeros((LANES, D_INNER), F32)
    zero_halo = jnp.zeros((HALO, CONV_DIM), BF16)
    _, meta_state = _ssd(m_xbc[None], m_small[None], m_sz[None], zero_halo, zero_state,
                         cw, cb, dtb, aneg, dsk, ng, META_BLOCK, META_PAD)
    meta_halo = m_xbc[META_BLOCK - HALO:]

    cq, ckv, small, sz, xbc, ga, gs = _inproj(x2d, g_mix, w_in_r, tm)
    q, k, v = _mla_prep(cq, ckv, small, qn, kvn, wq, wk, wv, _rope_tables(N_META + jnp.arange(seq)), tm, seq // tm)
    o_attn = _flash(q.reshape(b, seq, -1), k.reshape(b, seq, -1), v.reshape(b, seq, -1), kmeta, vmeta, tq, tk)
    o_ssm, _ = _ssd(xbc.reshape(b, seq, -1), small.reshape(b, seq, -1), sz.reshape(b, seq, -1), meta_halo,
                    meta_state[0], cw, cb, dtb, aneg, dsk, ng, chunk, 0)
    h2, vb, comb = _merge(x2d, o_attn.reshape(rows, -1), o_ssm.reshape(rows, -1), ga, gs, wa, ws, wo, nffn, wr, br, tm)
    out = _moe(vb, comb, h2, wg, wu, wd, nfin, _pick(rows, (1024, 512, 256, 128)))
    return out.reshape(b, seq, D_MODEL)
```

```python
import functools

import numpy as np
import jax
import jax.numpy as jnp
from jax import lax
from jax.experimental import pallas as pl
from jax.experimental.pallas import tpu as pltpu

F32 = jnp.float32
BF16 = jnp.bfloat16

D_MODEL = 1024
N_META = 16
EPS = 1e-6
NEG = -1e30
MLA_HEADS = 16
Q_RANK = 256
KV_RANK = 128
NOPE_DIM = 64
ROPE_DIM = 32
V_DIM = 64
ROPE_THETA = 10000.0
SSM_HEADS = 16
SSM_HEAD_DIM = 64
D_INNER = SSM_HEADS * SSM_HEAD_DIM
SSM_GROUPS = 2
D_STATE = 64
CONV_WIDTH = 4
CONV_DIM = D_INNER + 2 * SSM_GROUPS * D_STATE
N_EXPERT_GROUPS = 4
EXPERTS_PER_GROUP = 8
N_EXPERTS = N_EXPERT_GROUPS * EXPERTS_PER_GROUP
D_EXPERT = 256

LANES = 128
META_BLOCK = 128
META_PAD = META_BLOCK - N_META
HEAD_SLOT = 128
EXPERT_LANE0 = 32
DT_LANE0 = 96
VMEM_LIMIT = 48 * 1024 * 1024

A_CQ, A_CKV, A_SMALL, A_Z, A_XBC, A_GA, A_GS = 256, 128, 256, D_INNER, CONV_DIM, D_MODEL, D_MODEL
A_OFFS = np.cumsum([0, A_CQ, A_CKV, A_SMALL, A_Z, A_XBC, A_GA, A_GS])


def _cparams(sem):
    return pltpu.CompilerParams(dimension_semantics=sem, vmem_limit_bytes=VMEM_LIMIT)


def _rms(x, g):
    return x * lax.rsqrt(jnp.mean(x * x, axis=-1, keepdims=True) + EPS) * g


def _sigmoid(x):
    return 1.0 / (1.0 + jnp.exp(-x))


def _dot(a, b):
    return jnp.dot(a, b, preferred_element_type=F32)


def _dot_nt(a, b):
    return lax.dot_general(a, b, (((1,), (1,)), ((), ())), preferred_element_type=F32)


def _dot_tn(a, b):
    return lax.dot_general(a, b, (((0,), (0,)), ((), ())), preferred_element_type=F32)


def _split3(x):
    hi = x.astype(BF16)
    r1 = x - hi.astype(F32)
    mid = r1.astype(BF16)
    lo = (r1 - mid.astype(F32)).astype(BF16)
    return hi, mid, lo


def _dot_exact_rhs(x, m):
    hi, mid, lo = _split3(x)
    return _dot(hi, m) + _dot(mid, m) + _dot(lo, m)


def _dot_exact_lhs(m, x):
    hi, mid, lo = _split3(x)
    return _dot(m, hi) + _dot(m, mid) + _dot(m, lo)


def _inproj_body(x_ref, g_ref, w_ref, cq_ref, ckv_ref, small_ref, sz_ref, xbc_ref, ga_ref, gs_ref):
    u = _rms(x_ref[...], g_ref[...]).astype(BF16)

    def mm(i):
        return _dot(u, w_ref[:, A_OFFS[i]:A_OFFS[i + 1]])

    cq_ref[...] = mm(0).astype(BF16)
    ckv_ref[...] = mm(1).astype(BF16)
    small_ref[...] = mm(2)
    z = mm(3)
    sz_ref[...] = (z * _sigmoid(z)).astype(BF16)
    xbc_ref[...] = mm(4).astype(BF16)
    ga_ref[...] = _sigmoid(mm(5)).astype(BF16)
    gs_ref[...] = _sigmoid(mm(6)).astype(BF16)


def _inproj(x2d, g, w, tm):
    r = x2d.shape[0]
    widths = (A_CQ, A_CKV, A_SMALL, A_Z, A_XBC, A_GA, A_GS)
    dtypes = (BF16, BF16, F32, BF16, BF16, BF16, BF16)
    return pl.pallas_call(
        _inproj_body,
        grid=(r // tm,),
        in_specs=[
            pl.BlockSpec((tm, D_MODEL), lambda i: (i, 0)),
            pl.BlockSpec((1, D_MODEL), lambda i: (0, 0)),
            pl.BlockSpec(w.shape, lambda i: (0, 0), pipeline_mode=pl.Buffered(1)),
        ],
        out_specs=[pl.BlockSpec((tm, n), lambda i: (i, 0)) for n in widths],
        out_shape=[jax.ShapeDtypeStruct((r, n), dt) for n, dt in zip(widths, dtypes)],
        compiler_params=_cparams(("parallel",)),
        name="inproj",
    )(x2d, g, w)


def _mla_prep_body(cq_ref, ckv_ref, small_ref, qn_ref, kvn_ref, wq_ref, wk_ref, wvt_ref,
                   cqt_ref, sqt_ref, ckt_ref, skt_ref, q_ref, k_ref, vt_ref):
    cn = _rms(cq_ref[...].astype(F32), qn_ref[...]).astype(BF16)
    kn = _rms(ckv_ref[...].astype(F32), kvn_ref[...]).astype(BF16)
    krp = small_ref[:, 0:LANES] * ckt_ref[...] + small_ref[:, LANES:2 * LANES] * skt_ref[...]
    krp2 = jnp.concatenate([krp, krp], axis=1)
    cqt = cqt_ref[...]
    sqt = sqt_ref[...]
    half = MLA_HEADS * HEAD_SLOT
    for hp in range(MLA_HEADS // 2):
        lo, hi = hp * 2 * HEAD_SLOT, (hp + 1) * 2 * HEAD_SLOT
        qa = _dot(cn, wq_ref[:, lo:hi])
        qb = _dot(cn, wq_ref[:, half + lo:half + hi])
        q_ref[:, lo:hi] = (qa * cqt + qb * sqt).astype(BF16)
        k_ref[:, lo:hi] = (_dot(kn, wk_ref[:, lo:hi]) + krp2).astype(BF16)
    vt = _dot_nt(wvt_ref[...], kn)
    tkv = vt_ref.shape[-1]
    for j in range(vt_ref.shape[0]):
        vt_ref[j] = vt[:, j * tkv:(j + 1) * tkv].astype(BF16)


def _mla_prep(cq, ckv, small, qn, kvn, wq, wk, wvt, tabs, tm, seq_blocks, tkv):
    r = cq.shape[0]
    nv = MLA_HEADS * V_DIM
    cqt, sqt, ckt, skt = tabs
    row = lambda n: pl.BlockSpec((tm, n), lambda i: (i, 0))
    tab = lambda n: pl.BlockSpec((tm, n), lambda i: (i % seq_blocks, 0))
    full = lambda a: pl.BlockSpec(a.shape, lambda i: (0, 0))
    return pl.pallas_call(
        _mla_prep_body,
        grid=(r // tm,),
        in_specs=[row(Q_RANK), row(KV_RANK), row(A_SMALL), full(qn), full(kvn), full(wq), full(wk), full(wvt),
                  tab(2 * HEAD_SLOT), tab(2 * HEAD_SLOT), tab(HEAD_SLOT), tab(HEAD_SLOT)],
        out_specs=[row(MLA_HEADS * HEAD_SLOT), row(MLA_HEADS * HEAD_SLOT),
                   pl.BlockSpec((tm // tkv, nv, tkv), lambda i: (i, 0, 0))],
        out_shape=[jax.ShapeDtypeStruct((r, MLA_HEADS * HEAD_SLOT), BF16),
                   jax.ShapeDtypeStruct((r, MLA_HEADS * HEAD_SLOT), BF16),
                   jax.ShapeDtypeStruct((r // tkv, nv, tkv), BF16)],
        compiler_params=_cparams(("parallel",)),
        name="mla_prep",
    )(cq, ckv, small, qn, kvn, wq, wk, wvt, cqt, sqt, ckt, skt)


def _flash_body(q_ref, k_ref, vt_ref, km_ref, vmt_ref, o_ref, *scratch, tq, tk, seq):
    s_scr = (scratch[0:2], scratch[2:4])
    p_scr = (scratch[4:6], scratch[6:8])
    acc_scr = scratch[8:10]
    key_i = lax.broadcasted_iota(jnp.int32, (tk, tq), 0)
    qry_i = lax.broadcasted_iota(jnp.int32, (tk, tq), 1)
    meta_row = lax.broadcasted_iota(jnp.int32, (META_BLOCK, 1), 0)
    val_row = lax.broadcasted_iota(jnp.int32, (2 * V_DIM, 1), 0)
    heads = [slice(hh * HEAD_SLOT, (hh + 1) * HEAD_SLOT) for hh in range(2)]
    ndiag = tq // tk
    assert ndiag % 2 == 0, "slot parity is static only when a query block spans an even number of key blocks"

    def qblock(qi, _):
        q0 = pl.multiple_of(qi * tq, tq)
        qs = [q_ref[pl.ds(q0, tq), hs] for hs in heads]
        nfull = qi * ndiag
        last = nfull + ndiag - 1

        def scores_into(slot, kj):
            k0 = pl.multiple_of(kj * tk, tk)
            for hh in range(2):
                s_scr[slot][hh][...] = _dot_nt(k_ref[pl.ds(k0, tk), heads[hh]], qs[hh])

        ms, ls = [], []
        for hh in range(2):
            st = jnp.where(meta_row >= META_PAD, _dot_nt(km_ref[:, heads[hh]], qs[hh]), NEG)
            m = jnp.max(st, axis=0, keepdims=True)
            p = jnp.exp2(st - m)
            ms.append(m)
            ls.append(jnp.sum(p, axis=0, keepdims=True))
            acc_scr[hh][...] = _dot(vmt_ref[...], p.astype(BF16))
            p_scr[1][hh][...] = jnp.zeros((tk, tq), BF16)
        scores_into(0, 0)

        def kvstep(kj, slot, carry, masked):
            ms, ls, alphas = carry
            vt_prev = vt_ref[jnp.maximum(kj - 1, 0)]
            pvs = [_dot(vt_prev, p_scr[1 - slot][hh][...]) for hh in range(2)]
            scores_into(1 - slot, jnp.minimum(kj + 1, last))
            out = ([], [], [])
            for hh in range(2):
                st = s_scr[slot][hh][...]
                if masked:
                    st = jnp.where(kj * tk + key_i <= q0 + qry_i, st, NEG)
                m_new = jnp.maximum(ms[hh], jnp.max(st, axis=0, keepdims=True))
                alpha = jnp.exp2(ms[hh] - m_new)
                p = jnp.exp2(st - m_new)
                p_scr[slot][hh][...] = p.astype(BF16)
                acc_scr[hh][...] = alphas[hh] * acc_scr[hh][...] + pvs[hh]
                for lst, val in zip(out, (m_new, alpha * ls[hh] + jnp.sum(p, axis=0, keepdims=True), alpha)):
                    lst.append(val)
            return tuple(tuple(x) for x in out)

        def two_steps(t, carry):
            carry = kvstep(2 * t, 0, carry, False)
            return kvstep(2 * t + 1, 1, carry, False)

        ones = jnp.ones((1, tq), F32)
        carry = lax.fori_loop(0, nfull // 2, two_steps, (tuple(ms), tuple(ls), (ones, ones)))
        for d in range(ndiag):
            carry = kvstep(nfull + d, d % 2, carry, True)
        _, ls, alphas = carry
        vt_last = vt_ref[last]
        outs = [(alphas[hh] * acc_scr[hh][...] + _dot(vt_last, p_scr[(ndiag - 1) % 2][hh][...])) / ls[hh]
                for hh in range(2)]
        o_ref[pl.ds(q0, tq), :] = jnp.where(val_row < V_DIM, outs[0], outs[1]).T.astype(BF16)
        return 0

    lax.fori_loop(0, seq // tq, qblock, 0)


def _flash(q, k, vt, kmeta, vmeta_t, tq):
    b, seq, _ = q.shape
    tk = vt.shape[-1]
    pairs = MLA_HEADS // 2
    return pl.pallas_call(
        functools.partial(_flash_body, tq=tq, tk=tk, seq=seq),
        grid=(b, pairs),
        in_specs=[
            pl.BlockSpec((None, seq, 2 * HEAD_SLOT), lambda i, p: (i, 0, p)),
            pl.BlockSpec((None, seq, 2 * HEAD_SLOT), lambda i, p: (i, 0, p)),
            pl.BlockSpec((None, seq // tk, 2 * V_DIM, tk), lambda i, p: (i, 0, p, 0)),
            pl.BlockSpec((META_BLOCK, 2 * HEAD_SLOT), lambda i, p: (0, p)),
            pl.BlockSpec((2 * V_DIM, META_BLOCK), lambda i, p: (p, 0)),
        ],
        out_specs=pl.BlockSpec((None, seq, 2 * V_DIM), lambda i, p: (i, 0, p)),
        out_shape=jax.ShapeDtypeStruct((b, seq, MLA_HEADS * V_DIM), BF16),
        scratch_shapes=([pltpu.VMEM((tk, tq), F32)] * 4 + [pltpu.VMEM((tk, tq), BF16)] * 4
                        + [pltpu.VMEM((2 * V_DIM, tq), F32)] * 2),
        compiler_params=_cparams(("parallel", "parallel")),
        name="flash",
    )(q, k, vt, kmeta, vmeta_t)


HALO = 8


def _ssd_body(xbc_ref, halo_ref, mh_ref, small_ref, sz_ref, cw_ref, cb_ref, dtb_ref, aneg_ref, dsk_ref,
              ng_ref, init_ref, o_ref, fin_ref, scr_ref, st_ref, y_ref, *, q, n_pad):
    c = pl.program_id(1)
    nc = pl.num_programs(1)

    @pl.when(c == 0)
    def _():
        st_ref[...] = init_ref[...]
        scr_ref[0:HALO, :] = mh_ref[...].astype(F32)

    @pl.when(c > 0)
    def _():
        scr_ref[0:HALO, :] = halo_ref[...].astype(F32)

    scr_ref[HALO:HALO + q, :] = xbc_ref[...].astype(F32)

    xc = cb_ref[...] + cw_ref[0:1, :] * scr_ref[pl.ds(HALO - 3, q), :]
    for w in range(1, CONV_WIDTH):
        xc = xc + cw_ref[w:w + 1, :] * scr_ref[pl.ds(HALO - 3 + w, q), :]
    xc = xc * _sigmoid(xc)

    lane = lax.broadcasted_iota(jnp.int32, (q, LANES), 1)
    dt = small_ref[:, LANES:2 * LANES] + dtb_ref[...]
    dt = jnp.maximum(dt, 0.0) + jnp.log1p(jnp.exp(-jnp.abs(dt)))
    dt = jnp.where((lane >= DT_LANE0) & (lane < DT_LANE0 + SSM_HEADS), dt, 0.0)
    if n_pad:
        rowv = lax.broadcasted_iota(jnp.int32, (q, 1), 0) >= n_pad
        xc = jnp.where(rowv, xc, 0.0)
        dt = jnp.where(rowv, dt, 0.0)

    xs = xc[:, :D_INNER]
    bm = xc[:, D_INNER:D_INNER + LANES]
    cm = xc[:, D_INNER + LANES:]

    a = dt * aneg_ref[...]
    ri = lax.broadcasted_iota(jnp.int32, (q, q), 0)
    ci = lax.broadcasted_iota(jnp.int32, (q, q), 1)
    tril = ri >= ci
    a_cs = _dot_exact_lhs(tril.astype(BF16), a)
    a_cs_t = a_cs.T

    er = lax.broadcasted_iota(jnp.int32, (LANES, D_INNER), 0)
    ec = lax.broadcasted_iota(jnp.int32, (LANES, D_INNER), 1)
    expand = (er - DT_LANE0 == ec // SSM_HEAD_DIM).astype(BF16)
    dt_x = _dot_exact_rhs(dt, expand)
    acs_x = _dot_exact_rhs(a_cs, expand)
    a_last = acs_x[q - 1:q, :]

    xdt = xs * dt_x
    xdt_b = xdt.astype(BF16)
    xd_b = (xdt * jnp.exp(a_last - acs_x)).astype(BF16)

    bm_b = bm.astype(BF16)
    cm_b = cm.astype(BF16)
    lane_q = lax.broadcasted_iota(jnp.int32, (q, LANES), 1)

    for g in range(SSM_GROUPS):
        in_g = (lane_q >= g * D_STATE) & (lane_q < (g + 1) * D_STATE)
        cb = _dot_nt(jnp.where(in_g, cm, 0.0).astype(BF16), bm_b)
        for pr in range(SSM_HEADS // SSM_GROUPS // 2):
            pair = g * (SSM_HEADS // SSM_GROUPS // 2) + pr
            rhs = xdt_b[:, pair * LANES:(pair + 1) * LANES]
            ys = []
            for hh in range(2):
                e = DT_LANE0 + 2 * pair + hh
                seg = a_cs[:, e:e + 1] - a_cs_t[e:e + 1, :]
                lmat = jnp.exp(jnp.where(tril, seg, NEG))
                ys.append(_dot((cb * lmat).astype(BF16), rhs))
            y_ref[:, pair * LANES:(pair + 1) * LANES] = jnp.where(lane_q < SSM_HEAD_DIM, ys[0], ys[1])

    sr = lax.broadcasted_iota(jnp.int32, (LANES, D_INNER), 0)
    sc = lax.broadcasted_iota(jnp.int32, (LANES, D_INNER), 1)
    same_group = (sr // D_STATE) == (sc // (D_INNER // SSM_GROUPS))
    prev = st_ref[...]
    y_off = _dot(cm_b, prev.astype(BF16)) * jnp.exp(acs_x)
    st_new = prev * jnp.exp(a_last) + jnp.where(same_group, _dot_tn(bm_b, xd_b), 0.0)
    st_ref[...] = st_new

    @pl.when(c == nc - 1)
    def _():
        fin_ref[...] = st_new

    y = y_ref[...] + y_off + xs * dsk_ref[...]
    yz = y * sz_ref[...].astype(F32)
    half = D_INNER // SSM_GROUPS
    outs = []
    for g in range(SSM_GROUPS):
        part = yz[:, g * half:(g + 1) * half]
        outs.append(part * lax.rsqrt(jnp.mean(part * part, axis=-1, keepdims=True) + EPS))
    o_ref[...] = (jnp.concatenate(outs, axis=1) * ng_ref[...]).astype(BF16)


def _ssd(xbc, small, sz, meta_halo, init_state, cw, cb, dtb, aneg, dsk, ng, q, n_pad):
    b, seq, _ = xbc.shape
    nc = seq // q
    hb = q // HALO
    full = lambda a: pl.BlockSpec(a.shape, lambda i, c: (0,) * a.ndim)
    return pl.pallas_call(
        functools.partial(_ssd_body, q=q, n_pad=n_pad),
        grid=(b, nc),
        in_specs=[
            pl.BlockSpec((None, q, CONV_DIM), lambda i, c: (i, c, 0)),
            pl.BlockSpec((None, HALO, CONV_DIM), lambda i, c: (i, jnp.maximum(c * hb - 1, 0), 0)),
            full(meta_halo),
            pl.BlockSpec((None, q, A_SMALL), lambda i, c: (i, c, 0)),
            pl.BlockSpec((None, q, D_INNER), lambda i, c: (i, c, 0)),
            full(cw), full(cb), full(dtb), full(aneg), full(dsk), full(ng), full(init_state),
        ],
        out_specs=[
            pl.BlockSpec((None, q, D_INNER), lambda i, c: (i, c, 0)),
            pl.BlockSpec((None, LANES, D_INNER), lambda i, c: (i, 0, 0)),
        ],
        out_shape=[jax.ShapeDtypeStruct((b, seq, D_INNER), BF16),
                   jax.ShapeDtypeStruct((b, LANES, D_INNER), F32)],
        scratch_shapes=[pltpu.VMEM((HALO + q, CONV_DIM), F32),
                        pltpu.VMEM((LANES, D_INNER), F32),
                        pltpu.VMEM((q, D_INNER), F32)],
        compiler_params=_cparams(("parallel", "arbitrary")),
        name="ssd",
    )(xbc, xbc, meta_halo, small, sz, cw, cb, dtb, aneg, dsk, ng, init_state)


def _route(logits):
    lane = lax.broadcasted_iota(jnp.int32, logits.shape, 1).astype(F32)
    gl = jnp.where(lane < N_EXPERT_GROUPS, logits, NEG)
    gmax = jnp.max(gl, axis=1, keepdims=True)
    gidx = jnp.min(jnp.where(gl == gmax, lane, float(LANES)), axis=1, keepdims=True)
    p_g = 1.0 / jnp.sum(jnp.exp(gl - gmax), axis=1, keepdims=True)
    lo = EXPERT_LANE0 + EXPERTS_PER_GROUP * gidx
    el = jnp.where((lane >= lo) & (lane < lo + EXPERTS_PER_GROUP), logits, NEG)
    t1 = jnp.max(el, axis=1, keepdims=True)
    i1 = jnp.min(jnp.where(el == t1, lane, float(LANES)), axis=1, keepdims=True)
    el2 = jnp.where(lane == i1, NEG, el)
    t2 = jnp.max(el2, axis=1, keepdims=True)
    i2 = jnp.min(jnp.where(el2 == t2, lane, float(LANES)), axis=1, keepdims=True)
    e21 = jnp.exp(t2 - t1)
    w1 = p_g / (1.0 + e21)
    w2 = w1 * e21
    return jnp.where(lane == i1, w1, jnp.where(lane == i2, w2, 0.0))


def _merge_body(x_ref, oa_ref, os_ref, ga_ref, gs_ref, wa_ref, ws_ref, wo_ref, nf_ref, wr_ref, br_ref,
                h2_ref, vb_ref, comb_ref):
    merged = (ga_ref[...].astype(F32) * _dot(oa_ref[...], wa_ref[...])
              + gs_ref[...].astype(F32) * _dot(os_ref[...], ws_ref[...]))
    h2 = x_ref[...] + _dot(merged.astype(BF16), wo_ref[...])
    h2_ref[...] = h2
    vb = _rms(h2, nf_ref[...]).astype(BF16)
    vb_ref[...] = vb
    comb_ref[...] = _route(_dot(vb, wr_ref[...]) + br_ref[...])


def _merge(x2d, oa, osm, ga, gs, wa, ws, wo, nf, wr, br, tm):
    r = x2d.shape[0]
    row = lambda n: pl.BlockSpec((tm, n), lambda i: (i, 0))
    full = lambda a: pl.BlockSpec(a.shape, lambda i: (0, 0))
    return pl.pallas_call(
        _merge_body,
        grid=(r // tm,),
        in_specs=[row(D_MODEL)] * 5 + [full(wa), full(ws), full(wo), full(nf), full(wr), full(br)],
        out_specs=[row(D_MODEL), row(D_MODEL), row(LANES)],
        out_shape=[jax.ShapeDtypeStruct((r, D_MODEL), F32),
                   jax.ShapeDtypeStruct((r, D_MODEL), BF16),
                   jax.ShapeDtypeStruct((r, LANES), F32)],
        compiler_params=_cparams(("parallel",)),
        name="merge",
    )(x2d, oa, osm, ga, gs, wa, ws, wo, nf, wr, br)


def _moe_body(vb_ref, comb_ref, h2_ref, wg_ref, wu_ref, wd_ref, nf_ref, out_ref, acc_ref):
    e = pl.program_id(1)

    @pl.when(e == 0)
    def _():
        acc_ref[...] = jnp.zeros_like(acc_ref)

    vb = vb_ref[...]
    g = _dot(vb, wg_ref[...])
    u = _dot(vb, wu_ref[...])
    lane = lax.broadcasted_iota(jnp.int32, comb_ref.shape, 1)
    cw = jnp.sum(jnp.where(lane == EXPERT_LANE0 + e, comb_ref[...], 0.0), axis=1, keepdims=True)
    hid = (g * _sigmoid(g) * u * cw).astype(BF16)
    acc_ref[...] += _dot(hid, wd_ref[...])

    @pl.when(e == N_EXPERTS - 1)
    def _():
        out_ref[...] = _rms(h2_ref[...] + acc_ref[...], nf_ref[...])


def _moe(vb, comb, h2, wg, wu, wd, nf, tm):
    r = vb.shape[0]
    return pl.pallas_call(
        _moe_body,
        grid=(r // tm, N_EXPERTS),
        in_specs=[
            pl.BlockSpec((tm, D_MODEL), lambda i, e: (i, 0)),
            pl.BlockSpec((tm, LANES), lambda i, e: (i, 0)),
            pl.BlockSpec((tm, D_MODEL), lambda i, e: (i, 0)),
            pl.BlockSpec((None, D_MODEL, D_EXPERT), lambda i, e: (e, 0, 0)),
            pl.BlockSpec((None, D_MODEL, D_EXPERT), lambda i, e: (e, 0, 0)),
            pl.BlockSpec((None, D_EXPERT, D_MODEL), lambda i, e: (e, 0, 0)),
            pl.BlockSpec((1, D_MODEL), lambda i, e: (0, 0)),
        ],
        out_specs=pl.BlockSpec((tm, D_MODEL), lambda i, e: (i, 0)),
        out_shape=jax.ShapeDtypeStruct((r, D_MODEL), F32),
        scratch_shapes=[pltpu.VMEM((tm, D_MODEL), F32)],
        compiler_params=_cparams(("parallel", "arbitrary")),
        name="moe",
    )(vb, comb, h2, wg, wu, wd, nf)


def _prep_w_in(w):
    o = np.cumsum([0, Q_RANK, KV_RANK, ROPE_DIM, D_INNER, CONV_DIM, SSM_HEADS, D_MODEL, D_MODEL])
    cq, ckv, kr, z, xbc, dt, ga, gs = (w[:, o[i]:o[i + 1]] for i in range(8))
    hr = ROPE_DIM // 2
    zeros = lambda n: jnp.zeros((w.shape[0], n), w.dtype)
    kra = jnp.concatenate([zeros(NOPE_DIM), kr, zeros(HEAD_SLOT - NOPE_DIM - ROPE_DIM)], axis=1)
    krb = jnp.concatenate([zeros(NOPE_DIM), -kr[:, hr:], kr[:, :hr], dt,
                           zeros(HEAD_SLOT - DT_LANE0 - SSM_HEADS)], axis=1)
    return jnp.concatenate([cq, ckv, kra, krb, z, xbc, ga, gs], axis=1).astype(BF16)


def _prep_w_uq(w):
    w = w.reshape(Q_RANK, MLA_HEADS, NOPE_DIM + ROPE_DIM)
    hr = ROPE_DIM // 2
    nope, r1, r2 = w[..., :NOPE_DIM], w[..., NOPE_DIM:NOPE_DIM + hr], w[..., NOPE_DIM + hr:]
    z = jnp.zeros((Q_RANK, MLA_HEADS, HEAD_SLOT - NOPE_DIM - ROPE_DIM), w.dtype)
    wa = jnp.concatenate([nope, r1, r2, z], axis=-1).reshape(Q_RANK, -1)
    wb = jnp.concatenate([jnp.zeros_like(nope), -r2, r1, z], axis=-1).reshape(Q_RANK, -1)
    return jnp.concatenate([wa, wb], axis=1).astype(BF16)


def _prep_w_ukv(w):
    w = w.reshape(KV_RANK, MLA_HEADS, NOPE_DIM + V_DIM)
    kn, v = w[..., :NOPE_DIM], w[..., NOPE_DIM:]
    wk = jnp.concatenate([kn, jnp.zeros((KV_RANK, MLA_HEADS, HEAD_SLOT - NOPE_DIM), w.dtype)], axis=-1)
    return wk.reshape(KV_RANK, -1).astype(BF16), v.reshape(KV_RANK, -1).T.astype(BF16)


def _rope_tables(pos):
    inv = ROPE_THETA ** (-jnp.arange(0, ROPE_DIM, 2, dtype=F32) / ROPE_DIM)
    ang = pos.astype(F32)[:, None] * inv[None, :]
    cos, sin = jnp.cos(ang), jnp.sin(ang)
    n = pos.shape[0]
    pad = jnp.zeros((n, HEAD_SLOT - NOPE_DIM - ROPE_DIM), F32)
    scale = (NOPE_DIM + ROPE_DIM) ** -0.5 * float(np.log2(np.e))
    cq = jnp.concatenate([jnp.ones((n, NOPE_DIM), F32), cos, cos, pad], axis=1) * scale
    sq = jnp.concatenate([jnp.zeros((n, NOPE_DIM), F32), sin, sin, pad], axis=1) * scale
    ck = jnp.concatenate([jnp.zeros((n, NOPE_DIM), F32), cos, cos, pad], axis=1)
    sk = jnp.concatenate([jnp.zeros((n, NOPE_DIM), F32), sin, sin, pad], axis=1)
    return jnp.tile(cq, (1, 2)), jnp.tile(sq, (1, 2)), ck, sk


def _head_lanes(v):
    return jnp.zeros((1, LANES), F32).at[0, DT_LANE0:DT_LANE0 + SSM_HEADS].set(v.astype(F32))


def _pick(n, prefs):
    for t in prefs:
        if n % t == 0:
            return t
    raise ValueError(f"no tile for {n}")


def kernel(x, meta_tokens, norm_mix, w_in, mla_q_norm, mla_w_uq, mla_kv_norm, mla_w_ukv, ssm_conv_w, ssm_conv_b, ssm_dt_bias, ssm_a_log, ssm_d_skip, ssm_norm, w_branch_attn, w_branch_ssm, w_out, norm_ffn, moe_w_group, moe_b_group, moe_w_expert, moe_b_expert, moe_w_gate, moe_w_up, moe_w_down, norm_final):
    b, seq, _ = x.shape
    assert w_in.shape[0] == 1, "one layer"
    rows = b * seq
    x2d = x.reshape(rows, D_MODEL)
    tm = _pick(seq, (512, 256, 128))
    tq = _pick(seq, (512, 256, 128))
    tk = _pick(tq, (256, 128))
    chunk = 128

    w_in_r = _prep_w_in(w_in[0])
    wq = _prep_w_uq(mla_w_uq[0])
    wk, wvt = _prep_w_ukv(mla_w_ukv[0])
    g_mix = norm_mix[0].reshape(1, D_MODEL)
    qn = mla_q_norm[0].reshape(1, Q_RANK)
    kvn = mla_kv_norm[0].reshape(1, KV_RANK)
    cw = ssm_conv_w[0]
    cb = ssm_conv_b[0].reshape(1, CONV_DIM)
    dtb = _head_lanes(ssm_dt_bias[0])
    aneg = _head_lanes(-jnp.exp(ssm_a_log[0].astype(F32)))
    dsk = jnp.repeat(ssm_d_skip[0].astype(F32), SSM_HEAD_DIM).reshape(1, D_INNER)
    ng = ssm_norm[0].reshape(1, D_INNER)
    wa = w_branch_attn[0].astype(BF16)
    ws = w_branch_ssm[0].astype(BF16)
    wo = w_out[0].astype(BF16)
    nffn = norm_ffn[0].reshape(1, D_MODEL)
    wr = jnp.zeros((D_MODEL, LANES), F32)
    wr = wr.at[:, :N_EXPERT_GROUPS].set(moe_w_group[0]).at[:, EXPERT_LANE0:EXPERT_LANE0 + N_EXPERTS].set(moe_w_expert[0])
    wr = wr.astype(BF16)
    br = jnp.zeros((1, LANES), F32)
    br = br.at[0, :N_EXPERT_GROUPS].set(moe_b_group[0]).at[0, EXPERT_LANE0:EXPERT_LANE0 + N_EXPERTS].set(moe_b_expert[0])
    wg = moe_w_gate[0].astype(BF16)
    wu = moe_w_up[0].astype(BF16)
    wd = moe_w_down[0].astype(BF16)
    nfin = norm_final.reshape(1, D_MODEL)

    meta_blk = jnp.concatenate([jnp.zeros((META_PAD, D_MODEL), F32), meta_tokens.astype(F32)], axis=0)
    m_cq, m_ckv, m_small, m_sz, m_xbc, _, _ = _inproj(meta_blk, g_mix, w_in_r, META_BLOCK)
    meta_pos = jnp.maximum(jnp.arange(META_BLOCK) - META_PAD, 0)
    _, kmeta, vmeta_t = _mla_prep(m_cq, m_ckv, m_small, qn, kvn, wq, wk, wvt, _rope_tables(meta_pos),
                                  META_BLOCK, 1, META_BLOCK)
    zero_state = jnp.zeros((LANES, D_INNER), F32)
    zero_halo = jnp.zeros((HALO, CONV_DIM), BF16)
    _, meta_state = _ssd(m_xbc[None], m_small[None], m_sz[None], zero_halo, zero_state,
                         cw, cb, dtb, aneg, dsk, ng, META_BLOCK, META_PAD)
    meta_halo = m_xbc[META_BLOCK - HALO:]

    cq, ckv, small, sz, xbc, ga, gs = _inproj(x2d, g_mix, w_in_r, tm)
    q, k, vt = _mla_prep(cq, ckv, small, qn, kvn, wq, wk, wvt, _rope_tables(N_META + jnp.arange(seq)),
                         tm, seq // tm, tk)
    o_attn = _flash(q.reshape(b, seq, -1), k.reshape(b, seq, -1), vt.reshape(b, seq // tk, -1, tk),
                    kmeta, vmeta_t[0], tq)
    o_ssm, _ = _ssd(xbc.reshape(b, seq, -1), small.reshape(b, seq, -1), sz.reshape(b, seq, -1), meta_halo,
                    meta_state[0], cw, cb, dtb, aneg, dsk, ng, chunk, 0)
    h2, vb, comb = _merge(x2d, o_attn.reshape(rows, -1), o_ssm.reshape(rows, -1), ga, gs, wa, ws, wo, nffn, wr, br, tm)
    out = _moe(vb, comb, h2, wg, wu, wd, nfin, _pick(rows, (1024, 512, 256, 128)))
    return out.reshape(b, seq, D_MODEL)
```

```python
import functools

import numpy as np
import jax
import jax.numpy as jnp
from jax import lax
from jax.experimental import pallas as pl
from jax.experimental.pallas import tpu as pltpu

F32 = jnp.float32
BF16 = jnp.bfloat16

D_MODEL = 1024
N_META = 16
EPS = 1e-6
NEG = -1e30
MLA_HEADS = 16
Q_RANK = 256
KV_RANK = 128
NOPE_DIM = 64
ROPE_DIM = 32
V_DIM = 64
ROPE_THETA = 10000.0
SSM_HEADS = 16
SSM_HEAD_DIM = 64
D_INNER = SSM_HEADS * SSM_HEAD_DIM
SSM_GROUPS = 2
D_STATE = 64
CONV_WIDTH = 4
CONV_DIM = D_INNER + 2 * SSM_GROUPS * D_STATE
N_EXPERT_GROUPS = 4
EXPERTS_PER_GROUP = 8
N_EXPERTS = N_EXPERT_GROUPS * EXPERTS_PER_GROUP
D_EXPERT = 256

LANES = 128
META_BLOCK = 128
META_PAD = META_BLOCK - N_META
HEAD_SLOT = 128
EXPERT_LANE0 = 32
DT_LANE0 = 96
VMEM_LIMIT = 48 * 1024 * 1024

A_CQ, A_CKV, A_SMALL, A_Z, A_XBC, A_GA, A_GS = 256, 128, 256, D_INNER, CONV_DIM, D_MODEL, D_MODEL
A_OFFS = np.cumsum([0, A_CQ, A_CKV, A_SMALL, A_Z, A_XBC, A_GA, A_GS])


def _cparams(sem):
    return pltpu.CompilerParams(dimension_semantics=sem, vmem_limit_bytes=VMEM_LIMIT)


def _rms(x, g):
    return x * lax.rsqrt(jnp.mean(x * x, axis=-1, keepdims=True) + EPS) * g


def _sigmoid(x):
    return 1.0 / (1.0 + jnp.exp(-x))


def _dot(a, b):
    return jnp.dot(a, b, preferred_element_type=F32)


def _dot_nt(a, b):
    return lax.dot_general(a, b, (((1,), (1,)), ((), ())), preferred_element_type=F32)


def _dot_tn(a, b):
    return lax.dot_general(a, b, (((0,), (0,)), ((), ())), preferred_element_type=F32)


def _split3(x):
    hi = x.astype(BF16)
    r1 = x - hi.astype(F32)
    mid = r1.astype(BF16)
    lo = (r1 - mid.astype(F32)).astype(BF16)
    return hi, mid, lo


def _dot_exact_rhs(x, m):
    hi, mid, lo = _split3(x)
    return _dot(hi, m) + _dot(mid, m) + _dot(lo, m)


def _dot_exact_lhs(m, x):
    hi, mid, lo = _split3(x)
    return _dot(m, hi) + _dot(m, mid) + _dot(m, lo)


def _inproj_body(x_ref, g_ref, w_ref, cq_ref, ckv_ref, small_ref, sz_ref, xbc_ref, ga_ref, gs_ref):
    u = _rms(x_ref[...], g_ref[...]).astype(BF16)

    def mm(i):
        return _dot(u, w_ref[:, A_OFFS[i]:A_OFFS[i + 1]])

    cq_ref[...] = mm(0).astype(BF16)
    ckv_ref[...] = mm(1).astype(BF16)
    small_ref[...] = mm(2)
    z = mm(3)
    sz_ref[...] = (z * _sigmoid(z)).astype(BF16)
    xbc_ref[...] = mm(4).astype(BF16)
    ga_ref[...] = _sigmoid(mm(5)).astype(BF16)
    gs_ref[...] = _sigmoid(mm(6)).astype(BF16)


def _inproj(x2d, g, w, tm):
    r = x2d.shape[0]
    widths = (A_CQ, A_CKV, A_SMALL, A_Z, A_XBC, A_GA, A_GS)
    dtypes = (BF16, BF16, F32, BF16, BF16, BF16, BF16)
    return pl.pallas_call(
        _inproj_body,
        grid=(r // tm,),
        in_specs=[
            pl.BlockSpec((tm, D_MODEL), lambda i: (i, 0)),
            pl.BlockSpec((1, D_MODEL), lambda i: (0, 0)),
            pl.BlockSpec(w.shape, lambda i: (0, 0), pipeline_mode=pl.Buffered(1)),
        ],
        out_specs=[pl.BlockSpec((tm, n), lambda i: (i, 0)) for n in widths],
        out_shape=[jax.ShapeDtypeStruct((r, n), dt) for n, dt in zip(widths, dtypes)],
        compiler_params=_cparams(("parallel",)),
        name="inproj",
    )(x2d, g, w)


def _mla_prep_body(cq_ref, ckv_ref, small_ref, qn_ref, kvn_ref, wq_ref, wk_ref, wvt_ref,
                   cqt_ref, sqt_ref, ckt_ref, skt_ref, q_ref, k_ref, vt_ref):
    cn = _rms(cq_ref[...].astype(F32), qn_ref[...]).astype(BF16)
    kn = _rms(ckv_ref[...].astype(F32), kvn_ref[...]).astype(BF16)
    krp = small_ref[:, 0:LANES] * ckt_ref[...] + small_ref[:, LANES:2 * LANES] * skt_ref[...]
    krp2 = jnp.concatenate([krp, krp], axis=1)
    cqt = cqt_ref[...]
    sqt = sqt_ref[...]
    half = MLA_HEADS * HEAD_SLOT
    for hp in range(MLA_HEADS // 2):
        lo, hi = hp * 2 * HEAD_SLOT, (hp + 1) * 2 * HEAD_SLOT
        qa = _dot(cn, wq_ref[:, lo:hi])
        qb = _dot(cn, wq_ref[:, half + lo:half + hi])
        q_ref[:, lo:hi] = (qa * cqt + qb * sqt).astype(BF16)
        k_ref[:, lo:hi] = (_dot(kn, wk_ref[:, lo:hi]) + krp2).astype(BF16)
    vt = _dot_nt(wvt_ref[...], kn)
    tkv = vt_ref.shape[-1]
    for j in range(vt_ref.shape[0]):
        vt_ref[j] = vt[:, j * tkv:(j + 1) * tkv].astype(BF16)


def _mla_prep(cq, ckv, small, qn, kvn, wq, wk, wvt, tabs, tm, seq_blocks, tkv):
    r = cq.shape[0]
    nv = MLA_HEADS * V_DIM
    cqt, sqt, ckt, skt = tabs
    row = lambda n: pl.BlockSpec((tm, n), lambda i: (i, 0))
    tab = lambda n: pl.BlockSpec((tm, n), lambda i: (i % seq_blocks, 0))
    full = lambda a: pl.BlockSpec(a.shape, lambda i: (0, 0))
    return pl.pallas_call(
        _mla_prep_body,
        grid=(r // tm,),
        in_specs=[row(Q_RANK), row(KV_RANK), row(A_SMALL), full(qn), full(kvn), full(wq), full(wk), full(wvt),
                  tab(2 * HEAD_SLOT), tab(2 * HEAD_SLOT), tab(HEAD_SLOT), tab(HEAD_SLOT)],
        out_specs=[row(MLA_HEADS * HEAD_SLOT), row(MLA_HEADS * HEAD_SLOT),
                   pl.BlockSpec((tm // tkv, nv, tkv), lambda i: (i, 0, 0))],
        out_shape=[jax.ShapeDtypeStruct((r, MLA_HEADS * HEAD_SLOT), BF16),
                   jax.ShapeDtypeStruct((r, MLA_HEADS * HEAD_SLOT), BF16),
                   jax.ShapeDtypeStruct((r // tkv, nv, tkv), BF16)],
        compiler_params=_cparams(("parallel",)),
        name="mla_prep",
    )(cq, ckv, small, qn, kvn, wq, wk, wvt, cqt, sqt, ckt, skt)


def _flash_body(q_ref, k_ref, vt_ref, km_ref, vmt_ref, o_ref, *scratch, tq, tk, seq):
    s_scr = (scratch[0:2], scratch[2:4])
    p_scr = (scratch[4:6], scratch[6:8])
    acc_scr = scratch[8:10]
    key_i = lax.broadcasted_iota(jnp.int32, (tk, tq), 0)
    qry_i = lax.broadcasted_iota(jnp.int32, (tk, tq), 1)
    meta_row = lax.broadcasted_iota(jnp.int32, (META_BLOCK, 1), 0)
    val_row = lax.broadcasted_iota(jnp.int32, (2 * V_DIM, 1), 0)
    heads = [slice(hh * HEAD_SLOT, (hh + 1) * HEAD_SLOT) for hh in range(2)]
    ndiag = tq // tk
    assert ndiag % 2 == 0, "slot parity is static only when a query block spans an even number of key blocks"

    def qblock(qi, _):
        q0 = pl.multiple_of(qi * tq, tq)
        qs = [q_ref[pl.ds(q0, tq), hs] for hs in heads]
        nfull = qi * ndiag
        last = nfull + ndiag - 1

        def scores_into(slot, kj):
            k0 = pl.multiple_of(kj * tk, tk)
            for hh in range(2):
                s_scr[slot][hh][...] = _dot_nt(k_ref[pl.ds(k0, tk), heads[hh]], qs[hh])

        ms, ls = [], []
        for hh in range(2):
            st = jnp.where(meta_row >= META_PAD, _dot_nt(km_ref[:, heads[hh]], qs[hh]), NEG)
            m = jnp.max(st, axis=0, keepdims=True)
            p = jnp.exp2(st - m)
            ms.append(m)
            ls.append(jnp.sum(p, axis=0, keepdims=True))
            acc_scr[hh][...] = _dot(vmt_ref[...], p.astype(BF16))
            p_scr[1][hh][...] = jnp.zeros((tk, tq), BF16)
        scores_into(0, 0)

        def kvstep(kj, slot, carry, masked):
            ms, ls, alphas = carry
            vt_prev = vt_ref[jnp.maximum(kj - 1, 0)]
            pvs = [_dot(vt_prev, p_scr[1 - slot][hh][...]) for hh in range(2)]
            scores_into(1 - slot, jnp.minimum(kj + 1, last))
            out = ([], [], [])
            for hh in range(2):
                st = s_scr[slot][hh][...]
                if masked:
                    st = jnp.where(kj * tk + key_i <= q0 + qry_i, st, NEG)
                m_new = jnp.maximum(ms[hh], jnp.max(st, axis=0, keepdims=True))
                alpha = jnp.exp2(ms[hh] - m_new)
                p = jnp.exp2(st - m_new)
                p_scr[slot][hh][...] = p.astype(BF16)
                acc_scr[hh][...] = alphas[hh] * acc_scr[hh][...] + pvs[hh]
                for lst, val in zip(out, (m_new, alpha * ls[hh] + jnp.sum(p, axis=0, keepdims=True), alpha)):
                    lst.append(val)
            return tuple(tuple(x) for x in out)

        def two_steps(t, carry):
            carry = kvstep(2 * t, 0, carry, False)
            return kvstep(2 * t + 1, 1, carry, False)

        ones = jnp.ones((1, tq), F32)
        carry = lax.fori_loop(0, nfull // 2, two_steps, (tuple(ms), tuple(ls), (ones, ones)))
        for d in range(ndiag):
            carry = kvstep(nfull + d, d % 2, carry, True)
        _, ls, alphas = carry
        vt_last = vt_ref[last]
        outs = [(alphas[hh] * acc_scr[hh][...] + _dot(vt_last, p_scr[(ndiag - 1) % 2][hh][...])) / ls[hh]
                for hh in range(2)]
        o_ref[pl.ds(q0, tq), :] = jnp.where(val_row < V_DIM, outs[0], outs[1]).T.astype(BF16)
        return 0

    lax.fori_loop(0, seq // tq, qblock, 0)


def _flash(q, k, vt, kmeta, vmeta_t, tq):
    b, seq, _ = q.shape
    tk = vt.shape[-1]
    pairs = MLA_HEADS // 2
    return pl.pallas_call(
        functools.partial(_flash_body, tq=tq, tk=tk, seq=seq),
        grid=(b, pairs),
        in_specs=[
            pl.BlockSpec((None, seq, 2 * HEAD_SLOT), lambda i, p: (i, 0, p)),
            pl.BlockSpec((None, seq, 2 * HEAD_SLOT), lambda i, p: (i, 0, p)),
            pl.BlockSpec((None, seq // tk, 2 * V_DIM, tk), lambda i, p: (i, 0, p, 0)),
            pl.BlockSpec((META_BLOCK, 2 * HEAD_SLOT), lambda i, p: (0, p)),
            pl.BlockSpec((2 * V_DIM, META_BLOCK), lambda i, p: (p, 0)),
        ],
        out_specs=pl.BlockSpec((None, seq, 2 * V_DIM), lambda i, p: (i, 0, p)),
        out_shape=jax.ShapeDtypeStruct((b, seq, MLA_HEADS * V_DIM), BF16),
        scratch_shapes=([pltpu.VMEM((tk, tq), F32)] * 4 + [pltpu.VMEM((tk, tq), BF16)] * 4
                        + [pltpu.VMEM((2 * V_DIM, tq), F32)] * 2),
        compiler_params=_cparams(("parallel", "parallel")),
        name="flash",
    )(q, k, vt, kmeta, vmeta_t)


HALO = 8


def _ssd_body(xbc_ref, halo_ref, mh_ref, small_ref, sz_ref, cw_ref, cb_ref, dtb_ref, aneg_ref, dsk_ref,
              ng_ref, init_ref, o_ref, fin_ref, scr_ref, st_ref, y_ref, *, q, n_pad):
    c = pl.program_id(1)
    nc = pl.num_programs(1)

    @pl.when(c == 0)
    def _():
        st_ref[...] = init_ref[...]
        scr_ref[0:HALO, :] = mh_ref[...].astype(F32)

    @pl.when(c > 0)
    def _():
        scr_ref[0:HALO, :] = halo_ref[...].astype(F32)

    scr_ref[HALO:HALO + q, :] = xbc_ref[...].astype(F32)

    xc = cb_ref[...] + cw_ref[0:1, :] * scr_ref[pl.ds(HALO - 3, q), :]
    for w in range(1, CONV_WIDTH):
        xc = xc + cw_ref[w:w + 1, :] * scr_ref[pl.ds(HALO - 3 + w, q), :]
    xc = xc * _sigmoid(xc)

    lane = lax.broadcasted_iota(jnp.int32, (q, LANES), 1)
    dt = small_ref[:, LANES:2 * LANES] + dtb_ref[...]
    dt = jnp.maximum(dt, 0.0) + jnp.log1p(jnp.exp(-jnp.abs(dt)))
    dt = jnp.where((lane >= DT_LANE0) & (lane < DT_LANE0 + SSM_HEADS), dt, 0.0)
    if n_pad:
        rowv = lax.broadcasted_iota(jnp.int32, (q, 1), 0) >= n_pad
        xc = jnp.where(rowv, xc, 0.0)
        dt = jnp.where(rowv, dt, 0.0)

    xs = xc[:, :D_INNER]
    bm = xc[:, D_INNER:D_INNER + LANES]
    cm = xc[:, D_INNER + LANES:]

    a = dt * aneg_ref[...]
    ri = lax.broadcasted_iota(jnp.int32, (q, q), 0)
    ci = lax.broadcasted_iota(jnp.int32, (q, q), 1)
    tril = ri >= ci
    a_cs = _dot_exact_lhs(tril.astype(BF16), a)
    a_cs_t = a_cs.T

    er = lax.broadcasted_iota(jnp.int32, (LANES, D_INNER), 0)
    ec = lax.broadcasted_iota(jnp.int32, (LANES, D_INNER), 1)
    expand = (er - DT_LANE0 == ec // SSM_HEAD_DIM).astype(BF16)
    dt_x = _dot_exact_rhs(dt, expand)
    acs_x = _dot_exact_rhs(a_cs, expand)
    a_last = acs_x[q - 1:q, :]

    xdt = xs * dt_x
    xdt_b = xdt.astype(BF16)
    xd_b = (xdt * jnp.exp(a_last - acs_x)).astype(BF16)

    bm_b = bm.astype(BF16)
    cm_b = cm.astype(BF16)
    lane_q = lax.broadcasted_iota(jnp.int32, (q, LANES), 1)

    for g in range(SSM_GROUPS):
        in_g = (lane_q >= g * D_STATE) & (lane_q < (g + 1) * D_STATE)
        cb = _dot_nt(jnp.where(in_g, cm, 0.0).astype(BF16), bm_b)
        for pr in range(SSM_HEADS // SSM_GROUPS // 2):
            pair = g * (SSM_HEADS // SSM_GROUPS // 2) + pr
            rhs = xdt_b[:, pair * LANES:(pair + 1) * LANES]
            ys = []
            for hh in range(2):
                e = DT_LANE0 + 2 * pair + hh
                seg = a_cs[:, e:e + 1] - a_cs_t[e:e + 1, :]
                lmat = jnp.exp(jnp.where(tril, seg, NEG))
                ys.append(_dot((cb * lmat).astype(BF16), rhs))
            y_ref[:, pair * LANES:(pair + 1) * LANES] = jnp.where(lane_q < SSM_HEAD_DIM, ys[0], ys[1])

    sr = lax.broadcasted_iota(jnp.int32, (LANES, D_INNER), 0)
    sc = lax.broadcasted_iota(jnp.int32, (LANES, D_INNER), 1)
    same_group = (sr // D_STATE) == (sc // (D_INNER // SSM_GROUPS))
    prev = st_ref[...]
    y_off = _dot(cm_b, prev.astype(BF16)) * jnp.exp(acs_x)
    st_new = prev * jnp.exp(a_last) + jnp.where(same_group, _dot_tn(bm_b, xd_b), 0.0)
    st_ref[...] = st_new

    @pl.when(c == nc - 1)
    def _():
        fin_ref[...] = st_new

    y = y_ref[...] + y_off + xs * dsk_ref[...]
    yz = y * sz_ref[...].astype(F32)
    half = D_INNER // SSM_GROUPS
    outs = []
    for g in range(SSM_GROUPS):
        part = yz[:, g * half:(g + 1) * half]
        outs.append(part * lax.rsqrt(jnp.mean(part * part, axis=-1, keepdims=True) + EPS))
    o_ref[...] = (jnp.concatenate(outs, axis=1) * ng_ref[...]).astype(BF16)


def _ssd(xbc, small, sz, meta_halo, init_state, cw, cb, dtb, aneg, dsk, ng, q, n_pad):
    b, seq, _ = xbc.shape
    nc = seq // q
    hb = q // HALO
    full = lambda a: pl.BlockSpec(a.shape, lambda i, c: (0,) * a.ndim)
    return pl.pallas_call(
        functools.partial(_ssd_body, q=q, n_pad=n_pad),
        grid=(b, nc),
        in_specs=[
            pl.BlockSpec((None, q, CONV_DIM), lambda i, c: (i, c, 0)),
            pl.BlockSpec((None, HALO, CONV_DIM), lambda i, c: (i, jnp.maximum(c * hb - 1, 0), 0)),
            full(meta_halo),
            pl.BlockSpec((None, q, A_SMALL), lambda i, c: (i, c, 0)),
            pl.BlockSpec((None, q, D_INNER), lambda i, c: (i, c, 0)),
            full(cw), full(cb), full(dtb), full(aneg), full(dsk), full(ng), full(init_state),
        ],
        out_specs=[
            pl.BlockSpec((None, q, D_INNER), lambda i, c: (i, c, 0)),
            pl.BlockSpec((None, LANES, D_INNER), lambda i, c: (i, 0, 0)),
        ],
        out_shape=[jax.ShapeDtypeStruct((b, seq, D_INNER), BF16),
                   jax.ShapeDtypeStruct((b, LANES, D_INNER), F32)],
        scratch_shapes=[pltpu.VMEM((HALO + q, CONV_DIM), F32),
                        pltpu.VMEM((LANES, D_INNER), F32),
                        pltpu.VMEM((q, D_INNER), F32)],
        compiler_params=_cparams(("parallel", "arbitrary")),
        name="ssd",
    )(xbc, xbc, meta_halo, small, sz, cw, cb, dtb, aneg, dsk, ng, init_state)


def _route(logits):
    lane = lax.broadcasted_iota(jnp.int32, logits.shape, 1).astype(F32)
    gl = jnp.where(lane < N_EXPERT_GROUPS, logits, NEG)
    gmax = jnp.max(gl, axis=1, keepdims=True)
    gidx = jnp.min(jnp.where(gl == gmax, lane, float(LANES)), axis=1, keepdims=True)
    p_g = 1.0 / jnp.sum(jnp.exp(gl - gmax), axis=1, keepdims=True)
    lo = EXPERT_LANE0 + EXPERTS_PER_GROUP * gidx
    el = jnp.where((lane >= lo) & (lane < lo + EXPERTS_PER_GROUP), logits, NEG)
    t1 = jnp.max(el, axis=1, keepdims=True)
    i1 = jnp.min(jnp.where(el == t1, lane, float(LANES)), axis=1, keepdims=True)
    el2 = jnp.where(lane == i1, NEG, el)
    t2 = jnp.max(el2, axis=1, keepdims=True)
    i2 = jnp.min(jnp.where(el2 == t2, lane, float(LANES)), axis=1, keepdims=True)
    e21 = jnp.exp(t2 - t1)
    w1 = p_g / (1.0 + e21)
    w2 = w1 * e21
    return lane, i1, i2, w1, w2


RT_E1, RT_E2, RT_R1, RT_R2, RT_W1, RT_W2 = range(6)


def _merge_body(x_ref, oa_ref, os_ref, ga_ref, gs_ref, wa_ref, ws_ref, wo_ref, nf_ref, wr_ref, br_ref,
                h2_ref, v_ref, rt_ref, cnt_out_ref, cnt_ref):
    @pl.when(pl.program_id(0) == 0)
    def _():
        cnt_ref[...] = jnp.zeros_like(cnt_ref)

    merged = (ga_ref[...].astype(F32) * _dot(oa_ref[...], wa_ref[...])
              + gs_ref[...].astype(F32) * _dot(os_ref[...], ws_ref[...]))
    h2 = x_ref[...] + _dot(merged.astype(BF16), wo_ref[...])
    h2_ref[...] = h2
    v = _rms(h2, nf_ref[...])
    v_ref[...] = v
    lane, i1, i2, w1, w2 = _route(_dot(v.astype(BF16), wr_ref[...]) + br_ref[...])

    tm = lane.shape[0]
    chosen = jnp.where((lane == i1) | (lane == i2), 1.0, 0.0)
    ri = lax.broadcasted_iota(jnp.int32, (tm, tm), 0)
    ci = lax.broadcasted_iota(jnp.int32, (tm, tm), 1)
    earlier = _dot((ri > ci).astype(BF16), chosen.astype(BF16)) + cnt_ref[...]
    r1 = jnp.sum(jnp.where(lane == i1, earlier, 0.0), axis=1, keepdims=True)
    r2 = jnp.sum(jnp.where(lane == i2, earlier, 0.0), axis=1, keepdims=True)
    cnt_ref[...] += jnp.sum(chosen, axis=0, keepdims=True)
    cnt_out_ref[...] = jnp.broadcast_to(cnt_ref[...], cnt_out_ref.shape)

    rec = jnp.zeros_like(lane)
    for col, val in ((RT_E1, i1 - EXPERT_LANE0), (RT_E2, i2 - EXPERT_LANE0), (RT_R1, r1), (RT_R2, r2),
                     (RT_W1, w1), (RT_W2, w2)):
        rec = jnp.where(lane == col, val, rec)
    rt_ref[...] = rec


def _merge(x2d, oa, osm, ga, gs, wa, ws, wo, nf, wr, br, tm):
    r = x2d.shape[0]
    row = lambda n: pl.BlockSpec((tm, n), lambda i: (i, 0))
    full = lambda a: pl.BlockSpec(a.shape, lambda i: (0, 0))
    return pl.pallas_call(
        _merge_body,
        grid=(r // tm,),
        in_specs=[row(D_MODEL)] * 5 + [full(wa), full(ws), full(wo), full(nf), full(wr), full(br)],
        out_specs=[row(D_MODEL), row(D_MODEL), row(LANES), pl.BlockSpec((8, LANES), lambda i: (0, 0))],
        out_shape=[jax.ShapeDtypeStruct((r, D_MODEL), F32),
                   jax.ShapeDtypeStruct((r, D_MODEL), F32),
                   jax.ShapeDtypeStruct((r, LANES), F32),
                   jax.ShapeDtypeStruct((8, LANES), F32)],
        scratch_shapes=[pltpu.VMEM((1, LANES), F32)],
        compiler_params=_cparams(("arbitrary",)),
        name="merge",
    )(x2d, oa, osm, ga, gs, wa, ws, wo, nf, wr, br)


def _row_copy(src_ref, src_row, dst_ref, dst_row, sem):
    return pltpu.make_async_copy(src_ref.at[pl.ds(src_row, 1)], dst_ref.at[pl.ds(dst_row, 1)], sem)


def _dispatch_body(p1_ref, p2_ref, v_ref, xs_ref, sem):
    tm = v_ref.shape[0]

    def issue(t, _):
        _row_copy(v_ref, t, xs_ref, p1_ref[0, t], sem).start()
        _row_copy(v_ref, t, xs_ref, p2_ref[0, t], sem).start()
        return 0

    lax.fori_loop(0, tm, issue, 0, unroll=8)
    for _ in range(2):
        pltpu.make_async_copy(v_ref, xs_ref.at[pl.ds(0, tm)], sem).wait()


def _dispatch(pos1, pos2, v, tm):
    r = v.shape[0]
    slot = pl.BlockSpec((None, 1, tm), lambda i: (i, 0, 0), memory_space=pltpu.SMEM)
    return pl.pallas_call(
        _dispatch_body,
        grid=(r // tm,),
        in_specs=[slot, slot, pl.BlockSpec((tm, D_MODEL), lambda i: (i, 0))],
        out_specs=pl.BlockSpec(memory_space=pl.ANY),
        out_shape=jax.ShapeDtypeStruct((2 * r, D_MODEL), F32),
        scratch_shapes=[pltpu.SemaphoreType.DMA],
        compiler_params=_cparams(("arbitrary",)),
        name="moe_dispatch",
    )(pos1.reshape(r // tm, 1, tm), pos2.reshape(r // tm, 1, tm), v)


def _experts_body(tile_ref, exp_ref, off_ref, nwork_ref, xs_ref, wg_ref, wu_ref, wd_ref, ys_ref,
                  wg_s, wu_s, wd_s):
    w = pl.program_id(0)
    tme = xs_ref.shape[0]

    @pl.when(w < nwork_ref[0])
    def _():
        e = exp_ref[w]
        tile = tile_ref[w]
        prev = jnp.maximum(w - 1, 0)
        new_expert = (w == 0) | (exp_ref[prev] != e)
        new_tile = (w == 0) | (tile_ref[prev] != tile)

        @pl.when(new_expert)
        def _():
            wg_s[...] = wg_ref[...].astype(BF16)
            wu_s[...] = wu_ref[...].astype(BF16)
            wd_s[...] = wd_ref[...].astype(BF16)

        x = xs_ref[...].astype(BF16)
        g = _dot(x, wg_s[...])
        u = _dot(x, wu_s[...])
        y = _dot((g * _sigmoid(g) * u).astype(BF16), wd_s[...])
        rows = tile * tme + lax.broadcasted_iota(jnp.int32, (tme, 1), 0)
        mine = (rows >= off_ref[e]) & (rows < off_ref[e + 1])

        @pl.when(new_tile)
        def _():
            ys_ref[...] = jnp.where(mine, y, 0.0)

        @pl.when(jnp.logical_not(new_tile))
        def _():
            ys_ref[...] = jnp.where(mine, y, ys_ref[...])


def _experts(tile_of, exp_of, offs, nwork, xs, wg, wu, wd, tme):
    s = xs.shape[0]
    nw = tile_of.shape[0]
    grid_spec = pltpu.PrefetchScalarGridSpec(
        num_scalar_prefetch=4,
        grid=(nw,),
        in_specs=[
            pl.BlockSpec((tme, D_MODEL), lambda w, t, e, o, n: (t[w], 0)),
            pl.BlockSpec((None, D_MODEL, D_EXPERT), lambda w, t, e, o, n: (e[w], 0, 0)),
            pl.BlockSpec((None, D_MODEL, D_EXPERT), lambda w, t, e, o, n: (e[w], 0, 0)),
            pl.BlockSpec((None, D_EXPERT, D_MODEL), lambda w, t, e, o, n: (e[w], 0, 0)),
        ],
        out_specs=pl.BlockSpec((tme, D_MODEL), lambda w, t, e, o, n: (t[w], 0)),
        scratch_shapes=[pltpu.VMEM((D_MODEL, D_EXPERT), BF16), pltpu.VMEM((D_MODEL, D_EXPERT), BF16),
                        pltpu.VMEM((D_EXPERT, D_MODEL), BF16)],
    )
    return pl.pallas_call(
        _experts_body,
        grid_spec=grid_spec,
        out_shape=jax.ShapeDtypeStruct((s, D_MODEL), F32),
        compiler_params=_cparams(("arbitrary",)),
        name="moe_experts",
    )(tile_of, exp_of, offs, nwork, xs, wg, wu, wd)


def _combine_body(p1_ref, p2_ref, rt_ref, h2_ref, nf_ref, ys_ref, out_ref, y1_ref, y2_ref, sem):
    tm = h2_ref.shape[0]

    def issue(t, _):
        _row_copy(ys_ref, p1_ref[0, t], y1_ref, t, sem).start()
        _row_copy(ys_ref, p2_ref[0, t], y2_ref, t, sem).start()
        return 0

    lax.fori_loop(0, tm, issue, 0, unroll=8)
    for buf in (y1_ref, y2_ref):
        pltpu.make_async_copy(ys_ref.at[pl.ds(0, tm)], buf, sem).wait()
    lane = lax.broadcasted_iota(jnp.int32, rt_ref.shape, 1)
    rt = rt_ref[...]
    w1 = jnp.sum(jnp.where(lane == RT_W1, rt, 0.0), axis=1, keepdims=True)
    w2 = jnp.sum(jnp.where(lane == RT_W2, rt, 0.0), axis=1, keepdims=True)
    out_ref[...] = _rms(h2_ref[...] + w1 * y1_ref[...] + w2 * y2_ref[...], nf_ref[...])


def _combine(pos1, pos2, rt, h2, nf, ys, tm):
    r = h2.shape[0]
    slot = pl.BlockSpec((None, 1, tm), lambda i: (i, 0, 0), memory_space=pltpu.SMEM)
    return pl.pallas_call(
        _combine_body,
        grid=(r // tm,),
        in_specs=[slot, slot, pl.BlockSpec((tm, LANES), lambda i: (i, 0)),
                  pl.BlockSpec((tm, D_MODEL), lambda i: (i, 0)), pl.BlockSpec((1, D_MODEL), lambda i: (0, 0)),
                  pl.BlockSpec(memory_space=pl.ANY)],
        out_specs=pl.BlockSpec((tm, D_MODEL), lambda i: (i, 0)),
        out_shape=jax.ShapeDtypeStruct((r, D_MODEL), F32),
        scratch_shapes=[pltpu.VMEM((tm, D_MODEL), F32), pltpu.VMEM((tm, D_MODEL), F32), pltpu.SemaphoreType.DMA],
        compiler_params=_cparams(("arbitrary",)),
        name="moe_combine",
    )(pos1.reshape(r // tm, 1, tm), pos2.reshape(r // tm, 1, tm), rt, h2, nf, ys)


def _moe_plan(rt, counts, tme):
    rows = rt.shape[0]
    e1 = rt[:, RT_E1].astype(jnp.int32)
    e2 = rt[:, RT_E2].astype(jnp.int32)
    r1 = rt[:, RT_R1].astype(jnp.int32)
    r2 = rt[:, RT_R2].astype(jnp.int32)
    cnt = counts.astype(jnp.int32)
    ends = jnp.cumsum(cnt)
    offs = jnp.concatenate([jnp.zeros((1,), jnp.int32), ends])
    pos1 = offs[e1] + r1
    pos2 = offs[e2] + r2
    first_tile = offs[:-1] // tme
    last_tile = (ends - 1) // tme
    n_items = jnp.where(cnt > 0, last_tile - first_tile + 1, 0)
    item_end = jnp.cumsum(n_items)
    nwork = item_end[-1]
    nw = 2 * rows // tme + N_EXPERTS - 1
    w = jnp.arange(nw, dtype=jnp.int32)
    wc = jnp.minimum(w, nwork - 1)
    exp_of = jnp.sum((wc[:, None] >= item_end[None, :]).astype(jnp.int32), axis=1)
    tile_of = (first_tile[exp_of] + (wc - (item_end[exp_of] - n_items[exp_of]))).astype(jnp.int32)
    return pos1, pos2, tile_of, exp_of, offs, nwork.reshape(1).astype(jnp.int32)


def _prep_w_in(w):
    o = np.cumsum([0, Q_RANK, KV_RANK, ROPE_DIM, D_INNER, CONV_DIM, SSM_HEADS, D_MODEL, D_MODEL])
    cq, ckv, kr, z, xbc, dt, ga, gs = (w[:, o[i]:o[i + 1]] for i in range(8))
    hr = ROPE_DIM // 2
    zeros = lambda n: jnp.zeros((w.shape[0], n), w.dtype)
    kra = jnp.concatenate([zeros(NOPE_DIM), kr, zeros(HEAD_SLOT - NOPE_DIM - ROPE_DIM)], axis=1)
    krb = jnp.concatenate([zeros(NOPE_DIM), -kr[:, hr:], kr[:, :hr], dt,
                           zeros(HEAD_SLOT - DT_LANE0 - SSM_HEADS)], axis=1)
    return jnp.concatenate([cq, ckv, kra, krb, z, xbc, ga, gs], axis=1).astype(BF16)


def _prep_w_uq(w):
    w = w.reshape(Q_RANK, MLA_HEADS, NOPE_DIM + ROPE_DIM)
    hr = ROPE_DIM // 2
    nope, r1, r2 = w[..., :NOPE_DIM], w[..., NOPE_DIM:NOPE_DIM + hr], w[..., NOPE_DIM + hr:]
    z = jnp.zeros((Q_RANK, MLA_HEADS, HEAD_SLOT - NOPE_DIM - ROPE_DIM), w.dtype)
    wa = jnp.concatenate([nope, r1, r2, z], axis=-1).reshape(Q_RANK, -1)
    wb = jnp.concatenate([jnp.zeros_like(nope), -r2, r1, z], axis=-1).reshape(Q_RANK, -1)
    return jnp.concatenate([wa, wb], axis=1).astype(BF16)


def _prep_w_ukv(w):
    w = w.reshape(KV_RANK, MLA_HEADS, NOPE_DIM + V_DIM)
    kn, v = w[..., :NOPE_DIM], w[..., NOPE_DIM:]
    wk = jnp.concatenate([kn, jnp.zeros((KV_RANK, MLA_HEADS, HEAD_SLOT - NOPE_DIM), w.dtype)], axis=-1)
    return wk.reshape(KV_RANK, -1).astype(BF16), v.reshape(KV_RANK, -1).T.astype(BF16)


def _rope_tables(pos):
    inv = ROPE_THETA ** (-jnp.arange(0, ROPE_DIM, 2, dtype=F32) / ROPE_DIM)
    ang = pos.astype(F32)[:, None] * inv[None, :]
    cos, sin = jnp.cos(ang), jnp.sin(ang)
    n = pos.shape[0]
    pad = jnp.zeros((n, HEAD_SLOT - NOPE_DIM - ROPE_DIM), F32)
    scale = (NOPE_DIM + ROPE_DIM) ** -0.5 * float(np.log2(np.e))
    cq = jnp.concatenate([jnp.ones((n, NOPE_DIM), F32), cos, cos, pad], axis=1) * scale
    sq = jnp.concatenate([jnp.zeros((n, NOPE_DIM), F32), sin, sin, pad], axis=1) * scale
    ck = jnp.concatenate([jnp.zeros((n, NOPE_DIM), F32), cos, cos, pad], axis=1)
    sk = jnp.concatenate([jnp.zeros((n, NOPE_DIM), F32), sin, sin, pad], axis=1)
    return jnp.tile(cq, (1, 2)), jnp.tile(sq, (1, 2)), ck, sk


def _head_lanes(v):
    return jnp.zeros((1, LANES), F32).at[0, DT_LANE0:DT_LANE0 + SSM_HEADS].set(v.astype(F32))


def _pick(n, prefs):
    for t in prefs:
        if n % t == 0:
            return t
    raise ValueError(f"no tile for {n}")


def kernel(x, meta_tokens, norm_mix, w_in, mla_q_norm, mla_w_uq, mla_kv_norm, mla_w_ukv, ssm_conv_w, ssm_conv_b, ssm_dt_bias, ssm_a_log, ssm_d_skip, ssm_norm, w_branch_attn, w_branch_ssm, w_out, norm_ffn, moe_w_group, moe_b_group, moe_w_expert, moe_b_expert, moe_w_gate, moe_w_up, moe_w_down, norm_final):
    b, seq, _ = x.shape
    assert w_in.shape[0] == 1, "one layer"
    rows = b * seq
    x2d = x.reshape(rows, D_MODEL)
    tm = _pick(seq, (512, 256, 128))
    tq = _pick(seq, (512, 256, 128))
    tk = _pick(tq, (256, 128))
    chunk = 128

    w_in_r = _prep_w_in(w_in[0])
    wq = _prep_w_uq(mla_w_uq[0])
    wk, wvt = _prep_w_ukv(mla_w_ukv[0])
    g_mix = norm_mix[0].reshape(1, D_MODEL)
    qn = mla_q_norm[0].reshape(1, Q_RANK)
    kvn = mla_kv_norm[0].reshape(1, KV_RANK)
    cw = ssm_conv_w[0]
    cb = ssm_conv_b[0].reshape(1, CONV_DIM)
    dtb = _head_lanes(ssm_dt_bias[0])
    aneg = _head_lanes(-jnp.exp(ssm_a_log[0].astype(F32)))
    dsk = jnp.repeat(ssm_d_skip[0].astype(F32), SSM_HEAD_DIM).reshape(1, D_INNER)
    ng = ssm_norm[0].reshape(1, D_INNER)
    wa = w_branch_attn[0].astype(BF16)
    ws = w_branch_ssm[0].astype(BF16)
    wo = w_out[0].astype(BF16)
    nffn = norm_ffn[0].reshape(1, D_MODEL)
    wr = jnp.zeros((D_MODEL, LANES), F32)
    wr = wr.at[:, :N_EXPERT_GROUPS].set(moe_w_group[0]).at[:, EXPERT_LANE0:EXPERT_LANE0 + N_EXPERTS].set(moe_w_expert[0])
    wr = wr.astype(BF16)
    br = jnp.zeros((1, LANES), F32)
    br = br.at[0, :N_EXPERT_GROUPS].set(moe_b_group[0]).at[0, EXPERT_LANE0:EXPERT_LANE0 + N_EXPERTS].set(moe_b_expert[0])
    wg, wu, wd = moe_w_gate[0], moe_w_up[0], moe_w_down[0]
    nfin = norm_final.reshape(1, D_MODEL)

    meta_blk = jnp.concatenate([jnp.zeros((META_PAD, D_MODEL), F32), meta_tokens.astype(F32)], axis=0)
    m_cq, m_ckv, m_small, m_sz, m_xbc, _, _ = _inproj(meta_blk, g_mix, w_in_r, META_BLOCK)
    meta_pos = jnp.maximum(jnp.arange(META_BLOCK) - META_PAD, 0)
    _, kmeta, vmeta_t = _mla_prep(m_cq, m_ckv, m_small, qn, kvn, wq, wk, wvt, _rope_tables(meta_pos),
                                  META_BLOCK, 1, META_BLOCK)
    zero_state = jnp.zeros((LANES, D_INNER), F32)
    zero_halo = jnp.zeros((HALO, CONV_DIM), BF16)
    _, meta_state = _ssd(m_xbc[None], m_small[None], m_sz[None], zero_halo, zero_state,
                         cw, cb, dtb, aneg, dsk, ng, META_BLOCK, META_PAD)
    meta_halo = m_xbc[META_BLOCK - HALO:]

    cq, ckv, small, sz, xbc, ga, gs = _inproj(x2d, g_mix, w_in_r, tm)
    q, k, vt = _mla_prep(cq, ckv, small, qn, kvn, wq, wk, wvt, _rope_tables(N_META + jnp.arange(seq)),
                         tm, seq // tm, tk)
    o_attn = _flash(q.reshape(b, seq, -1), k.reshape(b, seq, -1), vt.reshape(b, seq // tk, -1, tk),
                    kmeta, vmeta_t[0], tq)
    o_ssm, _ = _ssd(xbc.reshape(b, seq, -1), small.reshape(b, seq, -1), sz.reshape(b, seq, -1), meta_halo,
                    meta_state[0], cw, cb, dtb, aneg, dsk, ng, chunk, 0)
    h2, v, rt, counts = _merge(x2d, o_attn.reshape(rows, -1), o_ssm.reshape(rows, -1), ga, gs, wa, ws, wo,
                               nffn, wr, br, tm)
    tme = _pick(2 * rows, (256, 128))
    pos1, pos2, tile_of, exp_of, offs, nwork = _moe_plan(
        rt, counts[0, EXPERT_LANE0:EXPERT_LANE0 + N_EXPERTS], tme)
    xs = _dispatch(pos1, pos2, v, tm)
    ys = _experts(tile_of, exp_of, offs, nwork, xs, wg, wu, wd, tme)
    out = _combine(pos1, pos2, rt, h2, nfin, ys, tm)
    return out.reshape(b, seq, D_MODEL)
```

```python
import functools

import numpy as np
import jax
import jax.numpy as jnp
from jax import lax
from jax.experimental import pallas as pl
from jax.experimental.pallas import tpu as pltpu

F32 = jnp.float32
BF16 = jnp.bfloat16

D_MODEL = 1024
N_META = 16
EPS = 1e-6
NEG = -1e30
MLA_HEADS = 16
Q_RANK = 256
KV_RANK = 128
NOPE_DIM = 64
ROPE_DIM = 32
V_DIM = 64
ROPE_THETA = 10000.0
SSM_HEADS = 16
SSM_HEAD_DIM = 64
D_INNER = SSM_HEADS * SSM_HEAD_DIM
SSM_GROUPS = 2
D_STATE = 64
CONV_WIDTH = 4
CONV_DIM = D_INNER + 2 * SSM_GROUPS * D_STATE
N_EXPERT_GROUPS = 4
EXPERTS_PER_GROUP = 8
N_EXPERTS = N_EXPERT_GROUPS * EXPERTS_PER_GROUP
D_EXPERT = 256

LANES = 128
META_BLOCK = 128
META_PAD = META_BLOCK - N_META
HEAD_SLOT = 128
V_SLOT = 80
FLASH_HEADS = 4
EXPERT_LANE0 = 32
DT_LANE0 = 96
VMEM_LIMIT = 48 * 1024 * 1024

A_CQ, A_CKV, A_SMALL, A_Z, A_XBC, A_GA, A_GS = 256, 128, 256, D_INNER, CONV_DIM, D_MODEL, D_MODEL
A_OFFS = np.cumsum([0, A_CQ, A_CKV, A_SMALL, A_Z, A_XBC, A_GA, A_GS])


def _cparams(sem):
    return pltpu.CompilerParams(dimension_semantics=sem, vmem_limit_bytes=VMEM_LIMIT)


def _rms(x, g):
    return x * lax.rsqrt(jnp.mean(x * x, axis=-1, keepdims=True) + EPS) * g


def _sigmoid(x):
    return 1.0 / (1.0 + jnp.exp(-x))


def _dot(a, b):
    return jnp.dot(a, b, preferred_element_type=F32)


def _dot_nt(a, b):
    return lax.dot_general(a, b, (((1,), (1,)), ((), ())), preferred_element_type=F32)


def _dot_tn(a, b):
    return lax.dot_general(a, b, (((0,), (0,)), ((), ())), preferred_element_type=F32)


def _split3(x):
    hi = x.astype(BF16)
    r1 = x - hi.astype(F32)
    mid = r1.astype(BF16)
    lo = (r1 - mid.astype(F32)).astype(BF16)
    return hi, mid, lo


def _dot_exact_rhs(x, m):
    hi, mid, lo = _split3(x)
    return _dot(hi, m) + _dot(mid, m) + _dot(lo, m)


def _dot_exact_lhs(m, x):
    hi, mid, lo = _split3(x)
    return _dot(m, hi) + _dot(m, mid) + _dot(m, lo)


def _inproj_body(x_ref, g_ref, w_ref, cq_ref, ckv_ref, small_ref, sz_ref, xbc_ref, ga_ref, gs_ref):
    u = _rms(x_ref[...], g_ref[...]).astype(BF16)

    def mm(i):
        return _dot(u, w_ref[:, A_OFFS[i]:A_OFFS[i + 1]])

    cq_ref[...] = mm(0).astype(BF16)
    ckv_ref[...] = mm(1).astype(BF16)
    small_ref[...] = mm(2)
    z = mm(3)
    sz_ref[...] = (z * _sigmoid(z)).astype(BF16)
    xbc_ref[...] = mm(4).astype(BF16)
    ga_ref[...] = _sigmoid(mm(5)).astype(BF16)
    gs_ref[...] = _sigmoid(mm(6)).astype(BF16)


def _inproj(x2d, g, w, tm):
    r = x2d.shape[0]
    widths = (A_CQ, A_CKV, A_SMALL, A_Z, A_XBC, A_GA, A_GS)
    dtypes = (BF16, BF16, F32, BF16, BF16, BF16, BF16)
    return pl.pallas_call(
        _inproj_body,
        grid=(r // tm,),
        in_specs=[
            pl.BlockSpec((tm, D_MODEL), lambda i: (i, 0)),
            pl.BlockSpec((1, D_MODEL), lambda i: (0, 0)),
            pl.BlockSpec(w.shape, lambda i: (0, 0), pipeline_mode=pl.Buffered(1)),
        ],
        out_specs=[pl.BlockSpec((tm, n), lambda i: (i, 0)) for n in widths],
        out_shape=[jax.ShapeDtypeStruct((r, n), dt) for n, dt in zip(widths, dtypes)],
        compiler_params=_cparams(("parallel",)),
        name="inproj",
    )(x2d, g, w)


def _mla_prep_body(cq_ref, ckv_ref, small_ref, qn_ref, kvn_ref, wq_ref, wk_ref, wvt_ref, vone_ref,
                   cqt_ref, sqt_ref, ckt_ref, skt_ref, q_ref, k_ref, vt_ref):
    cn = _rms(cq_ref[...].astype(F32), qn_ref[...]).astype(BF16)
    kn = _rms(ckv_ref[...].astype(F32), kvn_ref[...]).astype(BF16)
    krp = small_ref[:, 0:LANES] * ckt_ref[...] + small_ref[:, LANES:2 * LANES] * skt_ref[...]
    krp2 = jnp.concatenate([krp, krp], axis=1)
    cqt = cqt_ref[...]
    sqt = sqt_ref[...]
    half = MLA_HEADS * HEAD_SLOT
    for hp in range(MLA_HEADS // 2):
        lo, hi = hp * 2 * HEAD_SLOT, (hp + 1) * 2 * HEAD_SLOT
        qa = _dot(cn, wq_ref[:, lo:hi])
        qb = _dot(cn, wq_ref[:, half + lo:half + hi])
        q_ref[:, lo:hi] = (qa * cqt + qb * sqt).astype(BF16)
        k_ref[:, lo:hi] = (_dot(kn, wk_ref[:, lo:hi]) + krp2).astype(BF16)
    vt = _dot_nt(wvt_ref[...], kn) + vone_ref[...]
    tkv = vt_ref.shape[-1]
    for j in range(vt_ref.shape[0]):
        vt_ref[j] = vt[:, j * tkv:(j + 1) * tkv].astype(BF16)


def _mla_prep(cq, ckv, small, qn, kvn, wq, wk, wvt, tabs, tm, seq_blocks, tkv):
    r = cq.shape[0]
    nv = MLA_HEADS * V_SLOT
    cqt, sqt, ckt, skt = tabs
    ones_row = (jnp.arange(nv) % V_SLOT == V_DIM).astype(F32)
    vone = jnp.broadcast_to(ones_row[:, None], (nv, tm))
    row = lambda n: pl.BlockSpec((tm, n), lambda i: (i, 0))
    tab = lambda n: pl.BlockSpec((tm, n), lambda i: (i % seq_blocks, 0))
    full = lambda a: pl.BlockSpec(a.shape, lambda i: (0, 0))
    return pl.pallas_call(
        _mla_prep_body,
        grid=(r // tm,),
        in_specs=[row(Q_RANK), row(KV_RANK), row(A_SMALL), full(qn), full(kvn), full(wq), full(wk), full(wvt),
                  full(vone), tab(2 * HEAD_SLOT), tab(2 * HEAD_SLOT), tab(HEAD_SLOT), tab(HEAD_SLOT)],
        out_specs=[row(MLA_HEADS * HEAD_SLOT), row(MLA_HEADS * HEAD_SLOT),
                   pl.BlockSpec((tm // tkv, nv, tkv), lambda i: (i, 0, 0))],
        out_shape=[jax.ShapeDtypeStruct((r, MLA_HEADS * HEAD_SLOT), BF16),
                   jax.ShapeDtypeStruct((r, MLA_HEADS * HEAD_SLOT), BF16),
                   jax.ShapeDtypeStruct((r // tkv, nv, tkv), BF16)],
        compiler_params=_cparams(("parallel",)),
        name="mla_prep",
    )(cq, ckv, small, qn, kvn, wq, wk, wvt, vone, cqt, sqt, ckt, skt)


def _flash_body(q_ref, k_ref, vt_ref, km_ref, vmt_ref, o_ref, *scratch, tq, tk, seq):
    nh = FLASH_HEADS
    s_scr = (scratch[0:nh], scratch[nh:2 * nh])
    p_scr = (scratch[2 * nh:3 * nh], scratch[3 * nh:4 * nh])
    acc_scr = scratch[4 * nh:5 * nh]
    key_i = lax.broadcasted_iota(jnp.int32, (tk, tq), 0)
    qry_i = lax.broadcasted_iota(jnp.int32, (tk, tq), 1)
    meta_row = lax.broadcasted_iota(jnp.int32, (META_BLOCK, 1), 0)
    heads = [slice(hh * HEAD_SLOT, (hh + 1) * HEAD_SLOT) for hh in range(nh)]
    vals = [slice(hh * V_SLOT, (hh + 1) * V_SLOT) for hh in range(nh)]
    ndiag = tq // tk
    assert ndiag % 2 == 0, "slot parity is static only when a query block spans an even number of key blocks"

    def probs(st, m):
        return jnp.exp2((st - m).astype(BF16))

    def qblock(qi, _):
        q0 = pl.multiple_of(qi * tq, tq)
        qs = [q_ref[pl.ds(q0, tq), hs] for hs in heads]
        nfull = qi * ndiag
        last = nfull + ndiag - 1

        def scores_into(slot, kj):
            k0 = pl.multiple_of(kj * tk, tk)
            for hh in range(nh):
                s_scr[slot][hh][...] = _dot_nt(k_ref[pl.ds(k0, tk), heads[hh]], qs[hh])

        ms = []
        for hh in range(nh):
            st = jnp.where(meta_row >= META_PAD, _dot_nt(km_ref[:, heads[hh]], qs[hh]), NEG)
            m = jnp.max(st, axis=0, keepdims=True)
            ms.append(m)
            acc_scr[hh][...] = _dot(vmt_ref[vals[hh], :], probs(st, m))
            p_scr[1][hh][...] = jnp.zeros((tk, tq), BF16)
        scores_into(0, 0)

        def kvstep(kj, slot, carry, masked):
            ms, alphas = carry
            kprev = jnp.maximum(kj - 1, 0)
            pvs = [_dot(vt_ref[kprev, vals[hh], :], p_scr[1 - slot][hh][...]) for hh in range(nh)]
            scores_into(1 - slot, jnp.minimum(kj + 1, last))
            out = ([], [])
            for hh in range(nh):
                st = s_scr[slot][hh][...]
                if masked:
                    st = jnp.where(kj * tk + key_i <= q0 + qry_i, st, NEG)
                m_new = jnp.maximum(ms[hh], jnp.max(st, axis=0, keepdims=True))
                p_scr[slot][hh][...] = probs(st, m_new)
                acc_scr[hh][...] = alphas[hh] * acc_scr[hh][...] + pvs[hh]
                out[0].append(m_new)
                out[1].append(jnp.exp2(ms[hh] - m_new))
            return tuple(tuple(x) for x in out)

        def two_steps(t, carry):
            carry = kvstep(2 * t, 0, carry, False)
            return kvstep(2 * t + 1, 1, carry, False)

        ones = jnp.ones((1, tq), F32)
        carry = lax.fori_loop(0, nfull // 2, two_steps, (tuple(ms), (ones,) * nh))
        for d in range(ndiag):
            carry = kvstep(nfull + d, d % 2, carry, True)
        _, alphas = carry
        outs = []
        for hh in range(nh):
            acc = alphas[hh] * acc_scr[hh][...] + _dot(vt_ref[last, vals[hh], :], p_scr[(ndiag - 1) % 2][hh][...])
            outs.append(acc[:V_DIM] / acc[V_DIM:V_DIM + 1])
        o_ref[pl.ds(q0, tq), :] = jnp.concatenate(outs, axis=0).T.astype(BF16)
        return 0

    lax.fori_loop(0, seq // tq, qblock, 0)


def _flash(q, k, vt, kmeta, vmeta_t, tq):
    b, seq, _ = q.shape
    tk = vt.shape[-1]
    nh = FLASH_HEADS
    return pl.pallas_call(
        functools.partial(_flash_body, tq=tq, tk=tk, seq=seq),
        grid=(b, MLA_HEADS // nh),
        in_specs=[
            pl.BlockSpec((None, seq, nh * HEAD_SLOT), lambda i, p: (i, 0, p)),
            pl.BlockSpec((None, seq, nh * HEAD_SLOT), lambda i, p: (i, 0, p)),
            pl.BlockSpec((None, seq // tk, nh * V_SLOT, tk), lambda i, p: (i, 0, p, 0)),
            pl.BlockSpec((META_BLOCK, nh * HEAD_SLOT), lambda i, p: (0, p)),
            pl.BlockSpec((nh * V_SLOT, META_BLOCK), lambda i, p: (p, 0)),
        ],
        out_specs=pl.BlockSpec((None, seq, nh * V_DIM), lambda i, p: (i, 0, p)),
        out_shape=jax.ShapeDtypeStruct((b, seq, MLA_HEADS * V_DIM), BF16),
        scratch_shapes=([pltpu.VMEM((tk, tq), F32)] * (2 * nh) + [pltpu.VMEM((tk, tq), BF16)] * (2 * nh)
                        + [pltpu.VMEM((V_SLOT, tq), F32)] * nh),
        compiler_params=_cparams(("parallel", "parallel")),
        name="flash",
    )(q, k, vt, kmeta, vmeta_t)


HALO = 8


def _ssd_body(xbc_ref, halo_ref, mh_ref, small_ref, sz_ref, cw_ref, cb_ref, dtb_ref, aneg_ref, dsk_ref,
              ng_ref, init_ref, o_ref, fin_ref, scr_ref, st_ref, y_ref, *, q, n_pad):
    c = pl.program_id(1)
    nc = pl.num_programs(1)

    @pl.when(c == 0)
    def _():
        st_ref[...] = init_ref[...]
        scr_ref[0:HALO, :] = mh_ref[...].astype(F32)

    @pl.when(c > 0)
    def _():
        scr_ref[0:HALO, :] = halo_ref[...].astype(F32)

    scr_ref[HALO:HALO + q, :] = xbc_ref[...].astype(F32)

    xc = cb_ref[...] + cw_ref[0:1, :] * scr_ref[pl.ds(HALO - 3, q), :]
    for w in range(1, CONV_WIDTH):
        xc = xc + cw_ref[w:w + 1, :] * scr_ref[pl.ds(HALO - 3 + w, q), :]
    xc = xc * _sigmoid(xc)

    lane = lax.broadcasted_iota(jnp.int32, (q, LANES), 1)
    dt = small_ref[:, LANES:2 * LANES] + dtb_ref[...]
    dt = jnp.maximum(dt, 0.0) + jnp.log1p(jnp.exp(-jnp.abs(dt)))
    dt = jnp.where((lane >= DT_LANE0) & (lane < DT_LANE0 + SSM_HEADS), dt, 0.0)
    if n_pad:
        rowv = lax.broadcasted_iota(jnp.int32, (q, 1), 0) >= n_pad
        xc = jnp.where(rowv, xc, 0.0)
        dt = jnp.where(rowv, dt, 0.0)

    xs = xc[:, :D_INNER]
    bm = xc[:, D_INNER:D_INNER + LANES]
    cm = xc[:, D_INNER + LANES:]

    a = dt * aneg_ref[...]
    ri = lax.broadcasted_iota(jnp.int32, (q, q), 0)
    ci = lax.broadcasted_iota(jnp.int32, (q, q), 1)
    tril = ri >= ci
    a_cs = _dot_exact_lhs(tril.astype(BF16), a)
    a_cs_t = a_cs.T

    er = lax.broadcasted_iota(jnp.int32, (LANES, D_INNER), 0)
    ec = lax.broadcasted_iota(jnp.int32, (LANES, D_INNER), 1)
    expand = (er - DT_LANE0 == ec // SSM_HEAD_DIM).astype(BF16)
    dt_x = _dot_exact_rhs(dt, expand)
    acs_x = _dot_exact_rhs(a_cs, expand)
    a_last = acs_x[q - 1:q, :]

    xdt = xs * dt_x
    xdt_b = xdt.astype(BF16)
    xd_b = (xdt * jnp.exp(a_last - acs_x)).astype(BF16)

    bm_b = bm.astype(BF16)
    cm_b = cm.astype(BF16)
    lane_q = lax.broadcasted_iota(jnp.int32, (q, LANES), 1)

    for g in range(SSM_GROUPS):
        in_g = (lane_q >= g * D_STATE) & (lane_q < (g + 1) * D_STATE)
        cb = _dot_nt(jnp.where(in_g, cm, 0.0).astype(BF16), bm_b)
        for pr in range(SSM_HEADS // SSM_GROUPS // 2):
            pair = g * (SSM_HEADS // SSM_GROUPS // 2) + pr
            rhs = xdt_b[:, pair * LANES:(pair + 1) * LANES]
            ys = []
            for hh in range(2):
                e = DT_LANE0 + 2 * pair + hh
                seg = a_cs[:, e:e + 1] - a_cs_t[e:e + 1, :]
                lmat = jnp.exp(jnp.where(tril, seg, NEG))
                ys.append(_dot((cb * lmat).astype(BF16), rhs))
            y_ref[:, pair * LANES:(pair + 1) * LANES] = jnp.where(lane_q < SSM_HEAD_DIM, ys[0], ys[1])

    sr = lax.broadcasted_iota(jnp.int32, (LANES, D_INNER), 0)
    sc = lax.broadcasted_iota(jnp.int32, (LANES, D_INNER), 1)
    same_group = (sr // D_STATE) == (sc // (D_INNER // SSM_GROUPS))
    prev = st_ref[...]
    y_off = _dot(cm_b, prev.astype(BF16)) * jnp.exp(acs_x)
    st_new = prev * jnp.exp(a_last) + jnp.where(same_group, _dot_tn(bm_b, xd_b), 0.0)
    st_ref[...] = st_new

    @pl.when(c == nc - 1)
    def _():
        fin_ref[...] = st_new

    y = y_ref[...] + y_off + xs * dsk_ref[...]
    yz = y * sz_ref[...].astype(F32)
    half = D_INNER // SSM_GROUPS
    outs = []
    for g in range(SSM_GROUPS):
        part = yz[:, g * half:(g + 1) * half]
        outs.append(part * lax.rsqrt(jnp.mean(part * part, axis=-1, keepdims=True) + EPS))
    o_ref[...] = (jnp.concatenate(outs, axis=1) * ng_ref[...]).astype(BF16)


def _ssd(xbc, small, sz, meta_halo, init_state, cw, cb, dtb, aneg, dsk, ng, q, n_pad):
    b, seq, _ = xbc.shape
    nc = seq // q
    hb = q // HALO
    full = lambda a: pl.BlockSpec(a.shape, lambda i, c: (0,) * a.ndim)
    return pl.pallas_call(
        functools.partial(_ssd_body, q=q, n_pad=n_pad),
        grid=(b, nc),
        in_specs=[
            pl.BlockSpec((None, q, CONV_DIM), lambda i, c: (i, c, 0)),
            pl.BlockSpec((None, HALO, CONV_DIM), lambda i, c: (i, jnp.maximum(c * hb - 1, 0), 0)),
            full(meta_halo),
            pl.BlockSpec((None, q, A_SMALL), lambda i, c: (i, c, 0)),
            pl.BlockSpec((None, q, D_INNER), lambda i, c: (i, c, 0)),
            full(cw), full(cb), full(dtb), full(aneg), full(dsk), full(ng), full(init_state),
        ],
        out_specs=[
            pl.BlockSpec((None, q, D_INNER), lambda i, c: (i, c, 0)),
            pl.BlockSpec((None, LANES, D_INNER), lambda i, c: (i, 0, 0)),
        ],
        out_shape=[jax.ShapeDtypeStruct((b, seq, D_INNER), BF16),
                   jax.ShapeDtypeStruct((b, LANES, D_INNER), F32)],
        scratch_shapes=[pltpu.VMEM((HALO + q, CONV_DIM), F32),
                        pltpu.VMEM((LANES, D_INNER), F32),
                        pltpu.VMEM((q, D_INNER), F32)],
        compiler_params=_cparams(("parallel", "arbitrary")),
        name="ssd",
    )(xbc, xbc, meta_halo, small, sz, cw, cb, dtb, aneg, dsk, ng, init_state)


def _route(logits):
    lane = lax.broadcasted_iota(jnp.int32, logits.shape, 1).astype(F32)
    gl = jnp.where(lane < N_EXPERT_GROUPS, logits, NEG)
    gmax = jnp.max(gl, axis=1, keepdims=True)
    gidx = jnp.min(jnp.where(gl == gmax, lane, float(LANES)), axis=1, keepdims=True)
    p_g = 1.0 / jnp.sum(jnp.exp(gl - gmax), axis=1, keepdims=True)
    lo = EXPERT_LANE0 + EXPERTS_PER_GROUP * gidx
    el = jnp.where((lane >= lo) & (lane < lo + EXPERTS_PER_GROUP), logits, NEG)
    t1 = jnp.max(el, axis=1, keepdims=True)
    i1 = jnp.min(jnp.where(el == t1, lane, float(LANES)), axis=1, keepdims=True)
    el2 = jnp.where(lane == i1, NEG, el)
    t2 = jnp.max(el2, axis=1, keepdims=True)
    i2 = jnp.min(jnp.where(el2 == t2, lane, float(LANES)), axis=1, keepdims=True)
    e21 = jnp.exp(t2 - t1)
    w1 = p_g / (1.0 + e21)
    w2 = w1 * e21
    return lane, i1, i2, w1, w2


RT_E1, RT_E2, RT_R1, RT_R2, RT_W1, RT_W2 = range(6)


def _merge_body(x_ref, oa_ref, os_ref, ga_ref, gs_ref, wa_ref, ws_ref, wo_ref, nf_ref, wr_ref, br_ref,
                h2_ref, v_ref, rt_ref, cnt_out_ref, cnt_ref):
    @pl.when(pl.program_id(0) == 0)
    def _():
        cnt_ref[...] = jnp.zeros_like(cnt_ref)

    merged = (ga_ref[...].astype(F32) * _dot(oa_ref[...], wa_ref[...])
              + gs_ref[...].astype(F32) * _dot(os_ref[...], ws_ref[...]))
    h2 = x_ref[...] + _dot(merged.astype(BF16), wo_ref[...])
    h2_ref[...] = h2
    v = _rms(h2, nf_ref[...])
    v_ref[...] = v
    lane, i1, i2, w1, w2 = _route(_dot(v.astype(BF16), wr_ref[...]) + br_ref[...])

    tm = lane.shape[0]
    chosen = jnp.where((lane == i1) | (lane == i2), 1.0, 0.0)
    ri = lax.broadcasted_iota(jnp.int32, (tm, tm), 0)
    ci = lax.broadcasted_iota(jnp.int32, (tm, tm), 1)
    earlier = _dot((ri > ci).astype(BF16), chosen.astype(BF16)) + cnt_ref[...]
    r1 = jnp.sum(jnp.where(lane == i1, earlier, 0.0), axis=1, keepdims=True)
    r2 = jnp.sum(jnp.where(lane == i2, earlier, 0.0), axis=1, keepdims=True)
    cnt_ref[...] += jnp.sum(chosen, axis=0, keepdims=True)
    cnt_out_ref[...] = jnp.broadcast_to(cnt_ref[...], cnt_out_ref.shape)

    rec = jnp.zeros_like(lane)
    for col, val in ((RT_E1, i1 - EXPERT_LANE0), (RT_E2, i2 - EXPERT_LANE0), (RT_R1, r1), (RT_R2, r2),
                     (RT_W1, w1), (RT_W2, w2)):
        rec = jnp.where(lane == col, val, rec)
    rt_ref[...] = rec


def _merge(x2d, oa, osm, ga, gs, wa, ws, wo, nf, wr, br, tm):
    r = x2d.shape[0]
    row = lambda n: pl.BlockSpec((tm, n), lambda i: (i, 0))
    full = lambda a: pl.BlockSpec(a.shape, lambda i: (0, 0))
    return pl.pallas_call(
        _merge_body,
        grid=(r // tm,),
        in_specs=[row(D_MODEL)] * 5 + [full(wa), full(ws), full(wo), full(nf), full(wr), full(br)],
        out_specs=[row(D_MODEL), row(D_MODEL), row(LANES), pl.BlockSpec((8, LANES), lambda i: (0, 0))],
        out_shape=[jax.ShapeDtypeStruct((r, D_MODEL), F32),
                   jax.ShapeDtypeStruct((r, D_MODEL), F32),
                   jax.ShapeDtypeStruct((r, LANES), F32),
                   jax.ShapeDtypeStruct((8, LANES), F32)],
        scratch_shapes=[pltpu.VMEM((1, LANES), F32)],
        compiler_params=_cparams(("arbitrary",)),
        name="merge",
    )(x2d, oa, osm, ga, gs, wa, ws, wo, nf, wr, br)


def _row_copy(src_ref, src_row, dst_ref, dst_row, sem):
    return pltpu.make_async_copy(src_ref.at[pl.ds(src_row, 1)], dst_ref.at[pl.ds(dst_row, 1)], sem)


def _dispatch_body(pos_ref, v_ref, xs_ref, sem):
    tm = v_ref.shape[0]

    def issue(t, _):
        _row_copy(v_ref, t, xs_ref, pos_ref[0, t], sem).start()
        _row_copy(v_ref, t, xs_ref, pos_ref[1, t], sem).start()
        return 0

    lax.fori_loop(0, tm, issue, 0, unroll=8)
    for _ in range(2):
        pltpu.make_async_copy(v_ref, xs_ref.at[pl.ds(0, tm)], sem).wait()


def _slot_spec(tm):
    return pl.BlockSpec((8, tm), lambda i: (0, i), memory_space=pltpu.SMEM)


def _dispatch(pos, v, tm):
    r = v.shape[0]
    return pl.pallas_call(
        _dispatch_body,
        grid=(r // tm,),
        in_specs=[_slot_spec(tm), pl.BlockSpec((tm, D_MODEL), lambda i: (i, 0))],
        out_specs=pl.BlockSpec(memory_space=pl.ANY),
        out_shape=jax.ShapeDtypeStruct((2 * r, D_MODEL), F32),
        scratch_shapes=[pltpu.SemaphoreType.DMA],
        compiler_params=_cparams(("arbitrary",)),
        name="moe_dispatch",
    )(pos, v)


def _experts_body(tile_ref, exp_ref, off_ref, nwork_ref, xs_ref, wg_ref, wu_ref, wd_ref, ys_ref,
                  wg_s, wu_s, wd_s):
    w = pl.program_id(0)
    tme = xs_ref.shape[0]

    @pl.when(w < nwork_ref[0])
    def _():
        e = exp_ref[w]
        tile = tile_ref[w]
        prev = jnp.maximum(w - 1, 0)
        new_expert = (w == 0) | (exp_ref[prev] != e)
        new_tile = (w == 0) | (tile_ref[prev] != tile)

        @pl.when(new_expert)
        def _():
            wg_s[...] = wg_ref[...].astype(BF16)
            wu_s[...] = wu_ref[...].astype(BF16)
            wd_s[...] = wd_ref[...].astype(BF16)

        x = xs_ref[...].astype(BF16)
        g = _dot(x, wg_s[...])
        u = _dot(x, wu_s[...])
        y = _dot((g * _sigmoid(g) * u).astype(BF16), wd_s[...])
        rows = tile * tme + lax.broadcasted_iota(jnp.int32, (tme, 1), 0)
        mine = (rows >= off_ref[e]) & (rows < off_ref[e + 1])

        @pl.when(new_tile)
        def _():
            ys_ref[...] = jnp.where(mine, y, 0.0)

        @pl.when(jnp.logical_not(new_tile))
        def _():
            ys_ref[...] = jnp.where(mine, y, ys_ref[...])


def _experts(tile_of, exp_of, offs, nwork, xs, wg, wu, wd, tme):
    s = xs.shape[0]
    nw = tile_of.shape[0]
    grid_spec = pltpu.PrefetchScalarGridSpec(
        num_scalar_prefetch=4,
        grid=(nw,),
        in_specs=[
            pl.BlockSpec((tme, D_MODEL), lambda w, t, e, o, n: (t[w], 0)),
            pl.BlockSpec((None, D_MODEL, D_EXPERT), lambda w, t, e, o, n: (e[w], 0, 0)),
            pl.BlockSpec((None, D_MODEL, D_EXPERT), lambda w, t, e, o, n: (e[w], 0, 0)),
            pl.BlockSpec((None, D_EXPERT, D_MODEL), lambda w, t, e, o, n: (e[w], 0, 0)),
        ],
        out_specs=pl.BlockSpec((tme, D_MODEL), lambda w, t, e, o, n: (t[w], 0)),
        scratch_shapes=[pltpu.VMEM((D_MODEL, D_EXPERT), BF16), pltpu.VMEM((D_MODEL, D_EXPERT), BF16),
                        pltpu.VMEM((D_EXPERT, D_MODEL), BF16)],
    )
    return pl.pallas_call(
        _experts_body,
        grid_spec=grid_spec,
        out_shape=jax.ShapeDtypeStruct((s, D_MODEL), F32),
        compiler_params=_cparams(("arbitrary",)),
        name="moe_experts",
    )(tile_of, exp_of, offs, nwork, xs, wg, wu, wd)


def _combine_body(pos_ref, rt_ref, h2_ref, nf_ref, ys_ref, out_ref, y1_ref, y2_ref, sem):
    tm = h2_ref.shape[0]

    def issue(t, _):
        _row_copy(ys_ref, pos_ref[0, t], y1_ref, t, sem).start()
        _row_copy(ys_ref, pos_ref[1, t], y2_ref, t, sem).start()
        return 0

    lax.fori_loop(0, tm, issue, 0, unroll=8)
    for buf in (y1_ref, y2_ref):
        pltpu.make_async_copy(ys_ref.at[pl.ds(0, tm)], buf, sem).wait()
    lane = lax.broadcasted_iota(jnp.int32, rt_ref.shape, 1)
    rt = rt_ref[...]
    w1 = jnp.sum(jnp.where(lane == RT_W1, rt, 0.0), axis=1, keepdims=True)
    w2 = jnp.sum(jnp.where(lane == RT_W2, rt, 0.0), axis=1, keepdims=True)
    out_ref[...] = _rms(h2_ref[...] + w1 * y1_ref[...] + w2 * y2_ref[...], nf_ref[...])


def _combine(pos, rt, h2, nf, ys, tm):
    r = h2.shape[0]
    return pl.pallas_call(
        _combine_body,
        grid=(r // tm,),
        in_specs=[_slot_spec(tm), pl.BlockSpec((tm, LANES), lambda i: (i, 0)),
                  pl.BlockSpec((tm, D_MODEL), lambda i: (i, 0)), pl.BlockSpec((1, D_MODEL), lambda i: (0, 0)),
                  pl.BlockSpec(memory_space=pl.ANY)],
        out_specs=pl.BlockSpec((tm, D_MODEL), lambda i: (i, 0)),
        out_shape=jax.ShapeDtypeStruct((r, D_MODEL), F32),
        scratch_shapes=[pltpu.VMEM((tm, D_MODEL), F32), pltpu.VMEM((tm, D_MODEL), F32), pltpu.SemaphoreType.DMA],
        compiler_params=_cparams(("arbitrary",)),
        name="moe_combine",
    )(pos, rt, h2, nf, ys)


def _slots_body(rt_ref, cnt_ref, pos_ref):
    rt = rt_ref[...]
    lane = lax.broadcasted_iota(jnp.int32, rt.shape, 1).astype(F32)
    col = lambda c: jnp.sum(jnp.where(lane == c, rt, 0.0), axis=1, keepdims=True)
    ri = lax.broadcasted_iota(jnp.int32, (LANES, LANES), 0)
    ci = lax.broadcasted_iota(jnp.int32, (LANES, LANES), 1)
    first = _dot_exact_rhs(cnt_ref[...], (ri < ci).astype(BF16))[0:1, :]
    slot = lambda e, r: col(r) + jnp.sum(jnp.where(lane == col(e) + EXPERT_LANE0, first, 0.0),
                                         axis=1, keepdims=True)
    rec = jnp.where(lane == 0, slot(RT_E1, RT_R1), jnp.where(lane == 1, slot(RT_E2, RT_R2), 0.0))
    sr = lax.broadcasted_iota(jnp.int32, (8, LANES), 0)
    sc = lax.broadcasted_iota(jnp.int32, (8, LANES), 1)
    sel = (sr == sc).astype(BF16)
    pos_ref[...] = sum(_dot_nt(sel, part) for part in _split3(rec)).astype(jnp.int32)


def _slots(rt, counts, tm):
    r = rt.shape[0]
    return pl.pallas_call(
        _slots_body,
        grid=(r // tm,),
        in_specs=[pl.BlockSpec((tm, LANES), lambda i: (i, 0)), pl.BlockSpec((8, LANES), lambda i: (0, 0))],
        out_specs=pl.BlockSpec((8, tm), lambda i: (0, i)),
        out_shape=jax.ShapeDtypeStruct((8, r), jnp.int32),
        compiler_params=_cparams(("parallel",)),
        name="moe_slots",
    )(rt, counts)


def _moe_plan(counts, rows, tme):
    cnt = counts.astype(jnp.int32)
    ends = jnp.cumsum(cnt)
    offs = jnp.concatenate([jnp.zeros((1,), jnp.int32), ends])
    first_tile = offs[:-1] // tme
    last_tile = (ends - 1) // tme
    n_items = jnp.where(cnt > 0, last_tile - first_tile + 1, 0)
    item_end = jnp.cumsum(n_items)
    nwork = item_end[-1]
    nw = 2 * rows // tme + N_EXPERTS - 1
    w = jnp.arange(nw, dtype=jnp.int32)
    wc = jnp.minimum(w, nwork - 1)
    exp_of = jnp.sum((wc[:, None] >= item_end[None, :]).astype(jnp.int32), axis=1)
    tile_of = (first_tile[exp_of] + (wc - (item_end[exp_of] - n_items[exp_of]))).astype(jnp.int32)
    return tile_of, exp_of, offs, nwork.reshape(1).astype(jnp.int32)


def _prep_w_in(w):
    o = np.cumsum([0, Q_RANK, KV_RANK, ROPE_DIM, D_INNER, CONV_DIM, SSM_HEADS, D_MODEL, D_MODEL])
    cq, ckv, kr, z, xbc, dt, ga, gs = (w[:, o[i]:o[i + 1]] for i in range(8))
    hr = ROPE_DIM // 2
    zeros = lambda n: jnp.zeros((w.shape[0], n), w.dtype)
    kra = jnp.concatenate([zeros(NOPE_DIM), kr, zeros(HEAD_SLOT - NOPE_DIM - ROPE_DIM)], axis=1)
    krb = jnp.concatenate([zeros(NOPE_DIM), -kr[:, hr:], kr[:, :hr], dt,
                           zeros(HEAD_SLOT - DT_LANE0 - SSM_HEADS)], axis=1)
    return jnp.concatenate([cq, ckv, kra, krb, z, xbc, ga, gs], axis=1).astype(BF16)


def _prep_w_uq(w):
    w = w.reshape(Q_RANK, MLA_HEADS, NOPE_DIM + ROPE_DIM)
    hr = ROPE_DIM // 2
    nope, r1, r2 = w[..., :NOPE_DIM], w[..., NOPE_DIM:NOPE_DIM + hr], w[..., NOPE_DIM + hr:]
    z = jnp.zeros((Q_RANK, MLA_HEADS, HEAD_SLOT - NOPE_DIM - ROPE_DIM), w.dtype)
    wa = jnp.concatenate([nope, r1, r2, z], axis=-1).reshape(Q_RANK, -1)
    wb = jnp.concatenate([jnp.zeros_like(nope), -r2, r1, z], axis=-1).reshape(Q_RANK, -1)
    return jnp.concatenate([wa, wb], axis=1).astype(BF16)


def _prep_w_ukv(w):
    w = w.reshape(KV_RANK, MLA_HEADS, NOPE_DIM + V_DIM)
    kn, v = w[..., :NOPE_DIM], w[..., NOPE_DIM:]
    wk = jnp.concatenate([kn, jnp.zeros((KV_RANK, MLA_HEADS, HEAD_SLOT - NOPE_DIM), w.dtype)], axis=-1)
    vslot = jnp.concatenate([v, jnp.zeros((KV_RANK, MLA_HEADS, V_SLOT - V_DIM), w.dtype)], axis=-1)
    return wk.reshape(KV_RANK, -1).astype(BF16), vslot.reshape(KV_RANK, -1).T.astype(BF16)


def _rope_tables(pos):
    inv = ROPE_THETA ** (-jnp.arange(0, ROPE_DIM, 2, dtype=F32) / ROPE_DIM)
    ang = pos.astype(F32)[:, None] * inv[None, :]
    cos, sin = jnp.cos(ang), jnp.sin(ang)
    n = pos.shape[0]
    pad = jnp.zeros((n, HEAD_SLOT - NOPE_DIM - ROPE_DIM), F32)
    scale = (NOPE_DIM + ROPE_DIM) ** -0.5 * float(np.log2(np.e))
    cq = jnp.concatenate([jnp.ones((n, NOPE_DIM), F32), cos, cos, pad], axis=1) * scale
    sq = jnp.concatenate([jnp.zeros((n, NOPE_DIM), F32), sin, sin, pad], axis=1) * scale
    ck = jnp.concatenate([jnp.zeros((n, NOPE_DIM), F32), cos, cos, pad], axis=1)
    sk = jnp.concatenate([jnp.zeros((n, NOPE_DIM), F32), sin, sin, pad], axis=1)
    return jnp.tile(cq, (1, 2)), jnp.tile(sq, (1, 2)), ck, sk


def _head_lanes(v):
    return jnp.zeros((1, LANES), F32).at[0, DT_LANE0:DT_LANE0 + SSM_HEADS].set(v.astype(F32))


def _pick(n, prefs):
    for t in prefs:
        if n % t == 0:
            return t
    raise ValueError(f"no tile for {n}")


def kernel(x, meta_tokens, norm_mix, w_in, mla_q_norm, mla_w_uq, mla_kv_norm, mla_w_ukv, ssm_conv_w, ssm_conv_b, ssm_dt_bias, ssm_a_log, ssm_d_skip, ssm_norm, w_branch_attn, w_branch_ssm, w_out, norm_ffn, moe_w_group, moe_b_group, moe_w_expert, moe_b_expert, moe_w_gate, moe_w_up, moe_w_down, norm_final):
    b, seq, _ = x.shape
    assert w_in.shape[0] == 1, "one layer"
    rows = b * seq
    x2d = x.reshape(rows, D_MODEL)
    tm = _pick(seq, (512, 256, 128))
    tq = _pick(seq, (512, 256, 128))
    tk = _pick(tq, (256, 128))
    chunk = 128

    w_in_r = _prep_w_in(w_in[0])
    wq = _prep_w_uq(mla_w_uq[0])
    wk, wvt = _prep_w_ukv(mla_w_ukv[0])
    g_mix = norm_mix[0].reshape(1, D_MODEL)
    qn = mla_q_norm[0].reshape(1, Q_RANK)
    kvn = mla_kv_norm[0].reshape(1, KV_RANK)
    cw = ssm_conv_w[0]
    cb = ssm_conv_b[0].reshape(1, CONV_DIM)
    dtb = _head_lanes(ssm_dt_bias[0])
    aneg = _head_lanes(-jnp.exp(ssm_a_log[0].astype(F32)))
    dsk = jnp.repeat(ssm_d_skip[0].astype(F32), SSM_HEAD_DIM).reshape(1, D_INNER)
    ng = ssm_norm[0].reshape(1, D_INNER)
    wa = w_branch_attn[0].astype(BF16)
    ws = w_branch_ssm[0].astype(BF16)
    wo = w_out[0].astype(BF16)
    nffn = norm_ffn[0].reshape(1, D_MODEL)
    wr = jnp.zeros((D_MODEL, LANES), F32)
    wr = wr.at[:, :N_EXPERT_GROUPS].set(moe_w_group[0]).at[:, EXPERT_LANE0:EXPERT_LANE0 + N_EXPERTS].set(moe_w_expert[0])
    wr = wr.astype(BF16)
    br = jnp.zeros((1, LANES), F32)
    br = br.at[0, :N_EXPERT_GROUPS].set(moe_b_group[0]).at[0, EXPERT_LANE0:EXPERT_LANE0 + N_EXPERTS].set(moe_b_expert[0])
    wg, wu, wd = moe_w_gate[0], moe_w_up[0], moe_w_down[0]
    nfin = norm_final.reshape(1, D_MODEL)

    meta_blk = jnp.concatenate([jnp.zeros((META_PAD, D_MODEL), F32), meta_tokens.astype(F32)], axis=0)
    m_cq, m_ckv, m_small, m_sz, m_xbc, _, _ = _inproj(meta_blk, g_mix, w_in_r, META_BLOCK)
    meta_pos = jnp.maximum(jnp.arange(META_BLOCK) - META_PAD, 0)
    _, kmeta, vmeta_t = _mla_prep(m_cq, m_ckv, m_small, qn, kvn, wq, wk, wvt, _rope_tables(meta_pos),
                                  META_BLOCK, 1, META_BLOCK)
    zero_state = jnp.zeros((LANES, D_INNER), F32)
    zero_halo = jnp.zeros((HALO, CONV_DIM), BF16)
    _, meta_state = _ssd(m_xbc[None], m_small[None], m_sz[None], zero_halo, zero_state,
                         cw, cb, dtb, aneg, dsk, ng, META_BLOCK, META_PAD)
    meta_halo = m_xbc[META_BLOCK - HALO:]

    cq, ckv, small, sz, xbc, ga, gs = _inproj(x2d, g_mix, w_in_r, tm)
    q, k, vt = _mla_prep(cq, ckv, small, qn, kvn, wq, wk, wvt, _rope_tables(N_META + jnp.arange(seq)),
                         tm, seq // tm, tk)
    o_attn = _flash(q.reshape(b, seq, -1), k.reshape(b, seq, -1), vt.reshape(b, seq // tk, -1, tk),
                    kmeta, vmeta_t[0], tq)
    o_ssm, _ = _ssd(xbc.reshape(b, seq, -1), small.reshape(b, seq, -1), sz.reshape(b, seq, -1), meta_halo,
                    meta_state[0], cw, cb, dtb, aneg, dsk, ng, chunk, 0)
    h2, v, rt, counts = _merge(x2d, o_attn.reshape(rows, -1), o_ssm.reshape(rows, -1), ga, gs, wa, ws, wo,
                               nffn, wr, br, tm)
    tme = _pick(2 * rows, (256, 128))
    tile_of, exp_of, offs, nwork = _moe_plan(counts[0, EXPERT_LANE0:EXPERT_LANE0 + N_EXPERTS], rows, tme)
    pos = _slots(rt, counts, tm)
    xs = _dispatch(pos, v, tm)
    ys = _experts(tile_of, exp_of, offs, nwork, xs, wg, wu, wd, tme)
    out = _combine(pos, rt, h2, nfin, ys, tm)
    return out.reshape(b, seq, D_MODEL)
```

```python
import functools

import numpy as np
import jax
import jax.numpy as jnp
from jax import lax
from jax.experimental import pallas as pl
from jax.experimental.pallas import tpu as pltpu

F32 = jnp.float32
BF16 = jnp.bfloat16

D_MODEL = 1024
N_META = 16
EPS = 1e-6
NEG = -1e30
MLA_HEADS = 16
Q_RANK = 256
KV_RANK = 128
NOPE_DIM = 64
ROPE_DIM = 32
V_DIM = 64
ROPE_THETA = 10000.0
SSM_HEADS = 16
SSM_HEAD_DIM = 64
D_INNER = SSM_HEADS * SSM_HEAD_DIM
SSM_GROUPS = 2
D_STATE = 64
CONV_WIDTH = 4
CONV_DIM = D_INNER + 2 * SSM_GROUPS * D_STATE
N_EXPERT_GROUPS = 4
EXPERTS_PER_GROUP = 8
N_EXPERTS = N_EXPERT_GROUPS * EXPERTS_PER_GROUP
D_EXPERT = 256

LANES = 128
META_BLOCK = 128
META_PAD = META_BLOCK - N_META
HEAD_SLOT = 128
V_SLOT = 80
FLASH_HEADS = 4
EXPERT_LANE0 = 32
DT_LANE0 = 96
VMEM_LIMIT = 48 * 1024 * 1024

A_CQ, A_CKV, A_SMALL, A_Z, A_XBC, A_GA, A_GS = 256, 128, 256, D_INNER, CONV_DIM, D_MODEL, D_MODEL
A_OFFS = np.cumsum([0, A_CQ, A_CKV, A_SMALL, A_Z, A_XBC, A_GA, A_GS])


def _cparams(sem):
    return pltpu.CompilerParams(dimension_semantics=sem, vmem_limit_bytes=VMEM_LIMIT)


def _rms(x, g):
    return x * lax.rsqrt(jnp.mean(x * x, axis=-1, keepdims=True) + EPS) * g


def _sigmoid(x):
    return 1.0 / (1.0 + jnp.exp(-x))


def _dot(a, b):
    return jnp.dot(a, b, preferred_element_type=F32)


def _dot_nt(a, b):
    return lax.dot_general(a, b, (((1,), (1,)), ((), ())), preferred_element_type=F32)


def _dot_tn(a, b):
    return lax.dot_general(a, b, (((0,), (0,)), ((), ())), preferred_element_type=F32)


def _split3(x):
    hi = x.astype(BF16)
    r1 = x - hi.astype(F32)
    mid = r1.astype(BF16)
    lo = (r1 - mid.astype(F32)).astype(BF16)
    return hi, mid, lo


def _dot_exact_rhs(x, m):
    hi, mid, lo = _split3(x)
    return _dot(hi, m) + _dot(mid, m) + _dot(lo, m)


def _dot_exact_lhs(m, x):
    hi, mid, lo = _split3(x)
    return _dot(m, hi) + _dot(m, mid) + _dot(m, lo)


def _inproj_body(x_ref, g_ref, w_ref, cq_ref, ckv_ref, small_ref, sz_ref, xbc_ref, ga_ref, gs_ref):
    u = _rms(x_ref[...], g_ref[...]).astype(BF16)

    def mm(i):
        return _dot(u, w_ref[:, A_OFFS[i]:A_OFFS[i + 1]])

    cq_ref[...] = mm(0).astype(BF16)
    ckv_ref[...] = mm(1).astype(BF16)
    small_ref[...] = mm(2)
    z = mm(3)
    sz_ref[...] = (z * _sigmoid(z)).astype(BF16)
    xbc_ref[...] = mm(4).astype(BF16)
    ga_ref[...] = _sigmoid(mm(5)).astype(BF16)
    gs_ref[...] = _sigmoid(mm(6)).astype(BF16)


def _inproj(x2d, g, w, tm):
    r = x2d.shape[0]
    widths = (A_CQ, A_CKV, A_SMALL, A_Z, A_XBC, A_GA, A_GS)
    dtypes = (BF16, BF16, F32, BF16, BF16, BF16, BF16)
    return pl.pallas_call(
        _inproj_body,
        grid=(r // tm,),
        in_specs=[
            pl.BlockSpec((tm, D_MODEL), lambda i: (i, 0)),
            pl.BlockSpec((1, D_MODEL), lambda i: (0, 0)),
            pl.BlockSpec(w.shape, lambda i: (0, 0), pipeline_mode=pl.Buffered(1)),
        ],
        out_specs=[pl.BlockSpec((tm, n), lambda i: (i, 0)) for n in widths],
        out_shape=[jax.ShapeDtypeStruct((r, n), dt) for n, dt in zip(widths, dtypes)],
        compiler_params=_cparams(("parallel",)),
        name="inproj",
    )(x2d, g, w)


def _mla_prep_body(cq_ref, ckv_ref, small_ref, qn_ref, kvn_ref, wq_ref, wk_ref, wvt_ref, vone_ref,
                   cqt_ref, sqt_ref, ckt_ref, skt_ref, q_ref, k_ref, vt_ref):
    cn = _rms(cq_ref[...].astype(F32), qn_ref[...]).astype(BF16)
    kn = _rms(ckv_ref[...].astype(F32), kvn_ref[...]).astype(BF16)
    krp = small_ref[:, 0:LANES] * ckt_ref[...] + small_ref[:, LANES:2 * LANES] * skt_ref[...]
    krp2 = jnp.concatenate([krp, krp], axis=1)
    cqt = cqt_ref[...]
    sqt = sqt_ref[...]
    half = MLA_HEADS * HEAD_SLOT
    for hp in range(MLA_HEADS // 2):
        lo, hi = hp * 2 * HEAD_SLOT, (hp + 1) * 2 * HEAD_SLOT
        qa = _dot(cn, wq_ref[:, lo:hi])
        qb = _dot(cn, wq_ref[:, half + lo:half + hi])
        q_ref[:, lo:hi] = (qa * cqt + qb * sqt).astype(BF16)
        k_ref[:, lo:hi] = (_dot(kn, wk_ref[:, lo:hi]) + krp2).astype(BF16)
    vt = _dot_nt(wvt_ref[...], kn) + vone_ref[...]
    tkv = vt_ref.shape[-1]
    for j in range(vt_ref.shape[0]):
        vt_ref[j] = vt[:, j * tkv:(j + 1) * tkv].astype(BF16)


def _mla_prep(cq, ckv, small, qn, kvn, wq, wk, wvt, tabs, tm, seq_blocks, tkv):
    r = cq.shape[0]
    nv = MLA_HEADS * V_SLOT
    cqt, sqt, ckt, skt = tabs
    ones_row = (jnp.arange(nv) % V_SLOT == V_DIM).astype(F32)
    vone = jnp.broadcast_to(ones_row[:, None], (nv, tm))
    row = lambda n: pl.BlockSpec((tm, n), lambda i: (i, 0))
    tab = lambda n: pl.BlockSpec((tm, n), lambda i: (i % seq_blocks, 0))
    full = lambda a: pl.BlockSpec(a.shape, lambda i: (0, 0))
    return pl.pallas_call(
        _mla_prep_body,
        grid=(r // tm,),
        in_specs=[row(Q_RANK), row(KV_RANK), row(A_SMALL), full(qn), full(kvn), full(wq), full(wk), full(wvt),
                  full(vone), tab(2 * HEAD_SLOT), tab(2 * HEAD_SLOT), tab(HEAD_SLOT), tab(HEAD_SLOT)],
        out_specs=[row(MLA_HEADS * HEAD_SLOT), row(MLA_HEADS * HEAD_SLOT),
                   pl.BlockSpec((tm // tkv, nv, tkv), lambda i: (i, 0, 0))],
        out_shape=[jax.ShapeDtypeStruct((r, MLA_HEADS * HEAD_SLOT), BF16),
                   jax.ShapeDtypeStruct((r, MLA_HEADS * HEAD_SLOT), BF16),
                   jax.ShapeDtypeStruct((r // tkv, nv, tkv), BF16)],
        compiler_params=_cparams(("parallel",)),
        name="mla_prep",
    )(cq, ckv, small, qn, kvn, wq, wk, wvt, vone, cqt, sqt, ckt, skt)


def _flash_body(q_ref, k_ref, vt_ref, km_ref, vmt_ref, o_ref, *scratch, tq, tk, seq):
    nh = FLASH_HEADS
    s_scr = (scratch[0:nh], scratch[nh:2 * nh])
    p_scr = (scratch[2 * nh:3 * nh], scratch[3 * nh:4 * nh])
    acc_scr = scratch[4 * nh:5 * nh]
    key_i = lax.broadcasted_iota(jnp.int32, (tk, tq), 0)
    qry_i = lax.broadcasted_iota(jnp.int32, (tk, tq), 1)
    meta_row = lax.broadcasted_iota(jnp.int32, (META_BLOCK, 1), 0)
    heads = [slice(hh * HEAD_SLOT, (hh + 1) * HEAD_SLOT) for hh in range(nh)]
    vals = [slice(hh * V_SLOT, (hh + 1) * V_SLOT) for hh in range(nh)]
    ndiag = tq // tk
    assert ndiag % 2 == 0, "slot parity is static only when a query block spans an even number of key blocks"

    def probs(st, m):
        return jnp.exp2((st - m).astype(BF16))

    def qblock(qi, _):
        q0 = pl.multiple_of(qi * tq, tq)
        qs = [q_ref[pl.ds(q0, tq), hs] for hs in heads]
        nfull = qi * ndiag
        last = nfull + ndiag - 1

        def scores_into(slot, kj):
            k0 = pl.multiple_of(kj * tk, tk)
            for hh in range(nh):
                s_scr[slot][hh][...] = _dot_nt(k_ref[pl.ds(k0, tk), heads[hh]], qs[hh])

        ms = []
        for hh in range(nh):
            st = jnp.where(meta_row >= META_PAD, _dot_nt(km_ref[:, heads[hh]], qs[hh]), NEG)
            m = jnp.max(st, axis=0, keepdims=True)
            ms.append(m)
            acc_scr[hh][...] = _dot(vmt_ref[vals[hh], :], probs(st, m))
            p_scr[1][hh][...] = jnp.zeros((tk, tq), BF16)
        scores_into(0, 0)

        def kvstep(kj, slot, carry, masked):
            ms, alphas = carry
            kprev = jnp.maximum(kj - 1, 0)
            pvs = [_dot(vt_ref[kprev, vals[hh], :], p_scr[1 - slot][hh][...]) for hh in range(nh)]
            scores_into(1 - slot, jnp.minimum(kj + 1, last))
            out = ([], [])
            for hh in range(nh):
                st = s_scr[slot][hh][...]
                if masked:
                    st = jnp.where(kj * tk + key_i <= q0 + qry_i, st, NEG)
                m_new = jnp.maximum(ms[hh], jnp.max(st, axis=0, keepdims=True))
                p_scr[slot][hh][...] = probs(st, m_new)
                acc_scr[hh][...] = alphas[hh] * acc_scr[hh][...] + pvs[hh]
                out[0].append(m_new)
                out[1].append(jnp.exp2(ms[hh] - m_new))
            return tuple(tuple(x) for x in out)

        def two_steps(t, carry):
            carry = kvstep(2 * t, 0, carry, False)
            return kvstep(2 * t + 1, 1, carry, False)

        ones = jnp.ones((1, tq), F32)
        carry = lax.fori_loop(0, nfull // 2, two_steps, (tuple(ms), (ones,) * nh))
        for d in range(ndiag):
            carry = kvstep(nfull + d, d % 2, carry, True)
        _, alphas = carry
        outs = []
        for hh in range(nh):
            acc = alphas[hh] * acc_scr[hh][...] + _dot(vt_ref[last, vals[hh], :], p_scr[(ndiag - 1) % 2][hh][...])
            outs.append(acc[:V_DIM] / acc[V_DIM:V_DIM + 1])
        o_ref[pl.ds(q0, tq), :] = jnp.concatenate(outs, axis=0).T.astype(BF16)
        return 0

    lax.fori_loop(0, seq // tq, qblock, 0)


def _flash(q, k, vt, kmeta, vmeta_t, tq):
    b, seq, _ = q.shape
    tk = vt.shape[-1]
    nh = FLASH_HEADS
    return pl.pallas_call(
        functools.partial(_flash_body, tq=tq, tk=tk, seq=seq),
        grid=(b, MLA_HEADS // nh),
        in_specs=[
            pl.BlockSpec((None, seq, nh * HEAD_SLOT), lambda i, p: (i, 0, p)),
            pl.BlockSpec((None, seq, nh * HEAD_SLOT), lambda i, p: (i, 0, p)),
            pl.BlockSpec((None, seq // tk, nh * V_SLOT, tk), lambda i, p: (i, 0, p, 0)),
            pl.BlockSpec((META_BLOCK, nh * HEAD_SLOT), lambda i, p: (0, p)),
            pl.BlockSpec((nh * V_SLOT, META_BLOCK), lambda i, p: (p, 0)),
        ],
        out_specs=pl.BlockSpec((None, seq, nh * V_DIM), lambda i, p: (i, 0, p)),
        out_shape=jax.ShapeDtypeStruct((b, seq, MLA_HEADS * V_DIM), BF16),
        scratch_shapes=([pltpu.VMEM((tk, tq), F32)] * (2 * nh) + [pltpu.VMEM((tk, tq), BF16)] * (2 * nh)
                        + [pltpu.VMEM((V_SLOT, tq), F32)] * nh),
        compiler_params=_cparams(("parallel", "parallel")),
        name="flash",
    )(q, k, vt, kmeta, vmeta_t)


HALO = 16


def _expand2(x, expand):
    hi = x.astype(BF16)
    mid = (x - hi.astype(F32)).astype(BF16)
    return _dot(hi, expand) + _dot(mid, expand)


def _ssd_body(xbc_ref, halo_ref, mh_ref, small_ref, sz_ref, cw_ref, cb_ref, dtb_ref, aneg_ref, dsk_ref,
              ng_ref, init_ref, o_ref, fin_ref, st_ref, y_ref, *, q, n_pad):
    c = pl.program_id(1)
    nc = pl.num_programs(1)

    @pl.when(c == 0)
    def _():
        st_ref[...] = init_ref[...]

    state = st_ref[...]
    halo = jnp.where(c == 0, mh_ref[...], halo_ref[...])
    for j in range(xbc_ref.shape[0] // q):
        rows = slice(j * q, (j + 1) * q)
        x_b = xbc_ref[rows, :]
        state = _ssd_chunk(x_b, halo, small_ref[rows, LANES:2 * LANES], sz_ref[rows, :], state,
                           cw_ref, cb_ref, dtb_ref, aneg_ref, dsk_ref, ng_ref, o_ref.at[rows, :], y_ref.at[rows, :],
                           q=q, n_pad=n_pad - j * q)
        halo = x_b[q - HALO:, :]
    st_ref[...] = state

    @pl.when(c == nc - 1)
    def _():
        fin_ref[...] = state


def _ssd_chunk(x_b, halo, dt_raw, sz, prev, cw_ref, cb_ref, dtb_ref, aneg_ref, dsk_ref, ng_ref, o_ref, y_ref,
               *, q, n_pad):
    x_ext = jnp.concatenate([halo, x_b], axis=0)
    sh_r = lax.broadcasted_iota(jnp.int32, (q, HALO + q), 0)
    sh_c = lax.broadcasted_iota(jnp.int32, (q, HALO + q), 1)
    xc = cb_ref[...] + cw_ref[CONV_WIDTH - 1:CONV_WIDTH, :] * x_b.astype(F32)
    for d in range(1, CONV_WIDTH):
        shifted = _dot((sh_c == sh_r + (HALO - d)).astype(BF16), x_ext)
        xc = xc + cw_ref[CONV_WIDTH - 1 - d:CONV_WIDTH - d, :] * shifted
    xc = xc * _sigmoid(xc)

    lane = lax.broadcasted_iota(jnp.int32, (q, LANES), 1)
    dt = dt_raw + dtb_ref[...]
    dt = jnp.maximum(dt, 0.0) + jnp.log(1.0 + jnp.exp(-jnp.abs(dt)))
    dt = jnp.where((lane >= DT_LANE0) & (lane < DT_LANE0 + SSM_HEADS), dt, 0.0)
    if n_pad > 0:
        rowv = lax.broadcasted_iota(jnp.int32, (q, 1), 0) >= n_pad
        xc = jnp.where(rowv, xc, 0.0)
        dt = jnp.where(rowv, dt, 0.0)

    xs = xc[:, :D_INNER]
    bm = xc[:, D_INNER:D_INNER + LANES]
    cm = xc[:, D_INNER + LANES:]

    a = dt * aneg_ref[...]
    ri = lax.broadcasted_iota(jnp.int32, (q, q), 0)
    ci = lax.broadcasted_iota(jnp.int32, (q, q), 1)
    tril = ri >= ci
    a_cs = _dot_exact_lhs(tril.astype(BF16), a)
    a_cs_t = a_cs.T

    er = lax.broadcasted_iota(jnp.int32, (LANES, D_INNER), 0)
    ec = lax.broadcasted_iota(jnp.int32, (LANES, D_INNER), 1)
    expand = (er - DT_LANE0 == ec // SSM_HEAD_DIM).astype(BF16)
    dt_x = _expand2(dt, expand)
    grow_x = _expand2(jnp.exp(a_cs), expand)
    rest_x = _expand2(jnp.exp(a_cs[q - 1:q, :] - a_cs), expand)

    xdt = xs * dt_x
    xdt_b = xdt.astype(BF16)
    xd_b = (xdt * rest_x).astype(BF16)

    bm_b = bm.astype(BF16)
    cm_b = cm.astype(BF16)
    lane_q = lax.broadcasted_iota(jnp.int32, (q, LANES), 1)

    for g in range(SSM_GROUPS):
        in_g = (lane_q >= g * D_STATE) & (lane_q < (g + 1) * D_STATE)
        cb = _dot_nt(jnp.where(in_g, cm, 0.0).astype(BF16), bm_b)
        for pr in range(SSM_HEADS // SSM_GROUPS // 2):
            pair = g * (SSM_HEADS // SSM_GROUPS // 2) + pr
            rhs = xdt_b[:, pair * LANES:(pair + 1) * LANES]
            ys = []
            for hh in range(2):
                e = DT_LANE0 + 2 * pair + hh
                seg = a_cs[:, e:e + 1] - a_cs_t[e:e + 1, :]
                lmat = jnp.exp(jnp.where(tril, seg, NEG))
                ys.append(_dot((cb * lmat).astype(BF16), rhs))
            y_ref[:, pair * LANES:(pair + 1) * LANES] = jnp.where(lane_q < SSM_HEAD_DIM, ys[0], ys[1])

    sr = lax.broadcasted_iota(jnp.int32, (LANES, D_INNER), 0)
    sc = lax.broadcasted_iota(jnp.int32, (LANES, D_INNER), 1)
    same_group = (sr // D_STATE) == (sc // (D_INNER // SSM_GROUPS))
    y_off = _dot(cm_b, prev.astype(BF16)) * grow_x
    st_new = prev * grow_x[q - 1:q, :] + jnp.where(same_group, _dot_tn(bm_b, xd_b), 0.0)

    y = y_ref[...] + y_off + xs * dsk_ref[...]
    yz = y * sz.astype(F32)
    half = D_INNER // SSM_GROUPS
    outs = []
    for g in range(SSM_GROUPS):
        part = yz[:, g * half:(g + 1) * half]
        outs.append(part * lax.rsqrt(jnp.mean(part * part, axis=-1, keepdims=True) + EPS))
    o_ref[...] = (jnp.concatenate(outs, axis=1) * ng_ref[...]).astype(BF16)
    return st_new


def _ssd(xbc, small, sz, meta_halo, init_state, cw, cb, dtb, aneg, dsk, ng, blk, q, n_pad):
    b, seq, _ = xbc.shape
    nc = seq // blk
    hb = blk // HALO
    full = lambda a: pl.BlockSpec(a.shape, lambda i, c: (0,) * a.ndim)
    return pl.pallas_call(
        functools.partial(_ssd_body, q=q, n_pad=n_pad),
        grid=(b, nc),
        in_specs=[
            pl.BlockSpec((None, blk, CONV_DIM), lambda i, c: (i, c, 0)),
            pl.BlockSpec((None, HALO, CONV_DIM), lambda i, c: (i, jnp.maximum(c * hb - 1, 0), 0)),
            full(meta_halo),
            pl.BlockSpec((None, blk, A_SMALL), lambda i, c: (i, c, 0)),
            pl.BlockSpec((None, blk, D_INNER), lambda i, c: (i, c, 0)),
            full(cw), full(cb), full(dtb), full(aneg), full(dsk), full(ng), full(init_state),
        ],
        out_specs=[
            pl.BlockSpec((None, blk, D_INNER), lambda i, c: (i, c, 0)),
            pl.BlockSpec((None, LANES, D_INNER), lambda i, c: (i, 0, 0)),
        ],
        out_shape=[jax.ShapeDtypeStruct((b, seq, D_INNER), BF16),
                   jax.ShapeDtypeStruct((b, LANES, D_INNER), F32)],
        scratch_shapes=[pltpu.VMEM((LANES, D_INNER), F32),
                        pltpu.VMEM((blk, D_INNER), F32)],
        compiler_params=_cparams(("parallel", "arbitrary")),
        name="ssd",
    )(xbc, xbc, meta_halo, small, sz, cw, cb, dtb, aneg, dsk, ng, init_state)


def _route(logits):
    lane = lax.broadcasted_iota(jnp.int32, logits.shape, 1).astype(F32)
    gl = jnp.where(lane < N_EXPERT_GROUPS, logits, NEG)
    gmax = jnp.max(gl, axis=1, keepdims=True)
    gidx = jnp.min(jnp.where(gl == gmax, lane, float(LANES)), axis=1, keepdims=True)
    p_g = 1.0 / jnp.sum(jnp.exp(gl - gmax), axis=1, keepdims=True)
    lo = EXPERT_LANE0 + EXPERTS_PER_GROUP * gidx
    el = jnp.where((lane >= lo) & (lane < lo + EXPERTS_PER_GROUP), logits, NEG)
    t1 = jnp.max(el, axis=1, keepdims=True)
    i1 = jnp.min(jnp.where(el == t1, lane, float(LANES)), axis=1, keepdims=True)
    el2 = jnp.where(lane == i1, NEG, el)
    t2 = jnp.max(el2, axis=1, keepdims=True)
    i2 = jnp.min(jnp.where(el2 == t2, lane, float(LANES)), axis=1, keepdims=True)
    e21 = jnp.exp(t2 - t1)
    w1 = p_g / (1.0 + e21)
    w2 = w1 * e21
    return lane, i1, i2, w1, w2


RT_E1, RT_E2, RT_R1, RT_R2, RT_W1, RT_W2 = range(6)


def _merge_body(x_ref, oa_ref, os_ref, ga_ref, gs_ref, wa_ref, ws_ref, wo_ref, nf_ref, wr_ref, br_ref,
                h2_ref, v_ref, rt_ref, cnt_out_ref, cnt_ref):
    @pl.when(pl.program_id(0) == 0)
    def _():
        cnt_ref[...] = jnp.zeros_like(cnt_ref)

    merged = (ga_ref[...].astype(F32) * _dot(oa_ref[...], wa_ref[...])
              + gs_ref[...].astype(F32) * _dot(os_ref[...], ws_ref[...]))
    h2 = x_ref[...] + _dot(merged.astype(BF16), wo_ref[...])
    h2_ref[...] = h2
    v = _rms(h2, nf_ref[...])
    _store_token_slabs(v_ref, v)
    lane, i1, i2, w1, w2 = _route(_dot(v.astype(BF16), wr_ref[...]) + br_ref[...])

    tm = lane.shape[0]
    chosen = jnp.where((lane == i1) | (lane == i2), 1.0, 0.0)
    ri = lax.broadcasted_iota(jnp.int32, (tm, tm), 0)
    ci = lax.broadcasted_iota(jnp.int32, (tm, tm), 1)
    earlier = _dot((ri > ci).astype(BF16), chosen.astype(BF16)) + cnt_ref[...]
    r1 = jnp.sum(jnp.where(lane == i1, earlier, 0.0), axis=1, keepdims=True)
    r2 = jnp.sum(jnp.where(lane == i2, earlier, 0.0), axis=1, keepdims=True)
    cnt_ref[...] += jnp.sum(chosen, axis=0, keepdims=True)
    cnt_out_ref[...] = jnp.broadcast_to(cnt_ref[...], cnt_out_ref.shape)

    rec = jnp.zeros_like(lane)
    for col, val in ((RT_E1, i1 - EXPERT_LANE0), (RT_E2, i2 - EXPERT_LANE0), (RT_R1, r1), (RT_R2, r2),
                     (RT_W1, w1), (RT_W2, w2)):
        rec = jnp.where(lane == col, val, rec)
    rt_ref[...] = rec


def _merge(x2d, oa, osm, ga, gs, wa, ws, wo, nf, wr, br, tm):
    r = x2d.shape[0]
    row = lambda n: pl.BlockSpec((tm, n), lambda i: (i, 0))
    full = lambda a: pl.BlockSpec(a.shape, lambda i: (0, 0))
    return pl.pallas_call(
        _merge_body,
        grid=(r // tm,),
        in_specs=[row(D_MODEL)] * 5 + [full(wa), full(ws), full(wo), full(nf), full(wr), full(br)],
        out_specs=[row(D_MODEL), pl.BlockSpec((tm * SLAB, LANES), lambda i: (i, 0)), row(LANES),
                   pl.BlockSpec((8, LANES), lambda i: (0, 0))],
        out_shape=[jax.ShapeDtypeStruct((r, D_MODEL), F32),
                   jax.ShapeDtypeStruct((r * SLAB, LANES), F32),
                   jax.ShapeDtypeStruct((r, LANES), F32),
                   jax.ShapeDtypeStruct((8, LANES), F32)],
        scratch_shapes=[pltpu.VMEM((1, LANES), F32)],
        compiler_params=_cparams(("arbitrary",)),
        name="merge",
    )(x2d, oa, osm, ga, gs, wa, ws, wo, nf, wr, br)


SLAB = D_MODEL // LANES


def _store_token_slabs(ref, x):
    tm = x.shape[0]
    for s in range(SLAB):
        ref[pl.ds(s, tm, stride=SLAB), :] = x[:, s * LANES:(s + 1) * LANES]


def _load_token_slabs(ref):
    tm = ref.shape[0] // SLAB
    return jnp.concatenate([ref[pl.ds(s, tm, stride=SLAB), :] for s in range(SLAB)], axis=1)


def _row_copy(src_ref, src_row, dst_ref, dst_row, sem):
    src = src_ref.at[pl.ds(pl.multiple_of(src_row * SLAB, SLAB), SLAB)]
    dst = dst_ref.at[pl.ds(pl.multiple_of(dst_row * SLAB, SLAB), SLAB)]
    return pltpu.make_async_copy(src, dst, sem)


def _dispatch_body(pos_ref, v_ref, xs_ref, sem):
    tm = v_ref.shape[0] // SLAB

    def issue(t, _):
        _row_copy(v_ref, t, xs_ref, pos_ref[0, t], sem).start()
        _row_copy(v_ref, t, xs_ref, pos_ref[1, t], sem).start()
        return 0

    lax.fori_loop(0, tm, issue, 0, unroll=8)
    for _ in range(2):
        pltpu.make_async_copy(v_ref, xs_ref.at[pl.ds(0, tm * SLAB)], sem).wait()


def _slot_spec(tm):
    return pl.BlockSpec((8, tm), lambda i: (0, i), memory_space=pltpu.SMEM)


def _dispatch(pos, v, tm):
    r = v.shape[0] // SLAB
    return pl.pallas_call(
        _dispatch_body,
        grid=(r // tm,),
        in_specs=[_slot_spec(tm), pl.BlockSpec((tm * SLAB, LANES), lambda i: (i, 0))],
        out_specs=pl.BlockSpec(memory_space=pl.ANY),
        out_shape=jax.ShapeDtypeStruct((2 * r * SLAB, LANES), F32),
        scratch_shapes=[pltpu.SemaphoreType.DMA],
        compiler_params=_cparams(("arbitrary",)),
        name="moe_dispatch",
    )(pos, v)


def _experts_body(tile_ref, exp_ref, off_ref, nwork_ref, xs_ref, wg_ref, wu_ref, wd_ref, ys_ref,
                  wg_s, wu_s, wd_s):
    w = pl.program_id(0)
    tme = xs_ref.shape[0] // SLAB

    @pl.when(w < nwork_ref[0])
    def _():
        e = exp_ref[w]
        tile = tile_ref[w]
        prev = jnp.maximum(w - 1, 0)
        new_expert = (w == 0) | (exp_ref[prev] != e)
        new_tile = (w == 0) | (tile_ref[prev] != tile)

        @pl.when(new_expert)
        def _():
            wg_s[...] = wg_ref[...].astype(BF16)
            wu_s[...] = wu_ref[...].astype(BF16)
            wd_s[...] = wd_ref[...].astype(BF16)

        x = _load_token_slabs(xs_ref).astype(BF16)
        g = _dot(x, wg_s[...])
        u = _dot(x, wu_s[...])
        y = _dot((g * _sigmoid(g) * u).astype(BF16), wd_s[...])
        rows = tile * tme + lax.broadcasted_iota(jnp.int32, (tme, 1), 0)
        mine = (rows >= off_ref[e]) & (rows < off_ref[e + 1])

        @pl.when(new_tile)
        def _():
            _store_token_slabs(ys_ref, jnp.where(mine, y, 0.0))

        @pl.when(jnp.logical_not(new_tile))
        def _():
            _store_token_slabs(ys_ref, jnp.where(mine, y, _load_token_slabs(ys_ref)))


def _experts(tile_of, exp_of, offs, nwork, xs, wg, wu, wd, tme):
    nw = tile_of.shape[0]
    grid_spec = pltpu.PrefetchScalarGridSpec(
        num_scalar_prefetch=4,
        grid=(nw,),
        in_specs=[
            pl.BlockSpec((tme * SLAB, LANES), lambda w, t, e, o, n: (t[w], 0)),
            pl.BlockSpec((None, D_MODEL, D_EXPERT), lambda w, t, e, o, n: (e[w], 0, 0)),
            pl.BlockSpec((None, D_MODEL, D_EXPERT), lambda w, t, e, o, n: (e[w], 0, 0)),
            pl.BlockSpec((None, D_EXPERT, D_MODEL), lambda w, t, e, o, n: (e[w], 0, 0)),
        ],
        out_specs=pl.BlockSpec((tme * SLAB, LANES), lambda w, t, e, o, n: (t[w], 0)),
        scratch_shapes=[pltpu.VMEM((D_MODEL, D_EXPERT), BF16), pltpu.VMEM((D_MODEL, D_EXPERT), BF16),
                        pltpu.VMEM((D_EXPERT, D_MODEL), BF16)],
    )
    return pl.pallas_call(
        _experts_body,
        grid_spec=grid_spec,
        out_shape=jax.ShapeDtypeStruct(xs.shape, F32),
        compiler_params=_cparams(("arbitrary",)),
        name="moe_experts",
    )(tile_of, exp_of, offs, nwork, xs, wg, wu, wd)


def _combine_body(pos_ref, rt_ref, h2_ref, nf_ref, ys_ref, out_ref, y1_ref, y2_ref, sem):
    tm = h2_ref.shape[0]

    def issue(t, _):
        _row_copy(ys_ref, pos_ref[0, t], y1_ref, t, sem).start()
        _row_copy(ys_ref, pos_ref[1, t], y2_ref, t, sem).start()
        return 0

    lax.fori_loop(0, tm, issue, 0, unroll=8)
    for buf in (y1_ref, y2_ref):
        pltpu.make_async_copy(ys_ref.at[pl.ds(0, tm * SLAB)], buf, sem).wait()
    lane = lax.broadcasted_iota(jnp.int32, rt_ref.shape, 1)
    rt = rt_ref[...]
    w1 = jnp.sum(jnp.where(lane == RT_W1, rt, 0.0), axis=1, keepdims=True)
    w2 = jnp.sum(jnp.where(lane == RT_W2, rt, 0.0), axis=1, keepdims=True)
    moe = w1 * _load_token_slabs(y1_ref) + w2 * _load_token_slabs(y2_ref)
    out_ref[...] = _rms(h2_ref[...] + moe, nf_ref[...])


def _combine(pos, rt, h2, nf, ys, tm):
    r = h2.shape[0]
    return pl.pallas_call(
        _combine_body,
        grid=(r // tm,),
        in_specs=[_slot_spec(tm), pl.BlockSpec((tm, LANES), lambda i: (i, 0)),
                  pl.BlockSpec((tm, D_MODEL), lambda i: (i, 0)), pl.BlockSpec((1, D_MODEL), lambda i: (0, 0)),
                  pl.BlockSpec(memory_space=pl.ANY)],
        out_specs=pl.BlockSpec((tm, D_MODEL), lambda i: (i, 0)),
        out_shape=jax.ShapeDtypeStruct((r, D_MODEL), F32),
        scratch_shapes=[pltpu.VMEM((tm * SLAB, LANES), F32), pltpu.VMEM((tm * SLAB, LANES), F32),
                        pltpu.SemaphoreType.DMA],
        compiler_params=_cparams(("arbitrary",)),
        name="moe_combine",
    )(pos, rt, h2, nf, ys)


def _slots_body(rt_ref, cnt_ref, pos_ref):
    rt = rt_ref[...]
    lane = lax.broadcasted_iota(jnp.int32, rt.shape, 1).astype(F32)
    col = lambda c: jnp.sum(jnp.where(lane == c, rt, 0.0), axis=1, keepdims=True)
    ri = lax.broadcasted_iota(jnp.int32, (LANES, LANES), 0)
    ci = lax.broadcasted_iota(jnp.int32, (LANES, LANES), 1)
    first = _dot_exact_rhs(cnt_ref[...], (ri < ci).astype(BF16))[0:1, :]
    slot = lambda e, r: col(r) + jnp.sum(jnp.where(lane == col(e) + EXPERT_LANE0, first, 0.0),
                                         axis=1, keepdims=True)
    rec = jnp.where(lane == 0, slot(RT_E1, RT_R1), jnp.where(lane == 1, slot(RT_E2, RT_R2), 0.0))
    sr = lax.broadcasted_iota(jnp.int32, (8, LANES), 0)
    sc = lax.broadcasted_iota(jnp.int32, (8, LANES), 1)
    sel = (sr == sc).astype(BF16)
    pos_ref[...] = sum(_dot_nt(sel, part) for part in _split3(rec)).astype(jnp.int32)


def _slots(rt, counts, tm):
    r = rt.shape[0]
    return pl.pallas_call(
        _slots_body,
        grid=(r // tm,),
        in_specs=[pl.BlockSpec((tm, LANES), lambda i: (i, 0)), pl.BlockSpec((8, LANES), lambda i: (0, 0))],
        out_specs=pl.BlockSpec((8, tm), lambda i: (0, i)),
        out_shape=jax.ShapeDtypeStruct((8, r), jnp.int32),
        compiler_params=_cparams(("parallel",)),
        name="moe_slots",
    )(rt, counts)


def _moe_plan(counts, rows, tme):
    cnt = counts.astype(jnp.int32)
    ends = jnp.cumsum(cnt)
    offs = jnp.concatenate([jnp.zeros((1,), jnp.int32), ends])
    first_tile = offs[:-1] // tme
    last_tile = (ends - 1) // tme
    n_items = jnp.where(cnt > 0, last_tile - first_tile + 1, 0)
    item_end = jnp.cumsum(n_items)
    nwork = item_end[-1]
    nw = 2 * rows // tme + N_EXPERTS - 1
    w = jnp.arange(nw, dtype=jnp.int32)
    wc = jnp.minimum(w, nwork - 1)
    exp_of = jnp.sum((wc[:, None] >= item_end[None, :]).astype(jnp.int32), axis=1)
    tile_of = (first_tile[exp_of] + (wc - (item_end[exp_of] - n_items[exp_of]))).astype(jnp.int32)
    return tile_of, exp_of, offs, nwork.reshape(1).astype(jnp.int32)


def _prep_w_in(w):
    o = np.cumsum([0, Q_RANK, KV_RANK, ROPE_DIM, D_INNER, CONV_DIM, SSM_HEADS, D_MODEL, D_MODEL])
    cq, ckv, kr, z, xbc, dt, ga, gs = (w[:, o[i]:o[i + 1]] for i in range(8))
    hr = ROPE_DIM // 2
    zeros = lambda n: jnp.zeros((w.shape[0], n), w.dtype)
    kra = jnp.concatenate([zeros(NOPE_DIM), kr, zeros(HEAD_SLOT - NOPE_DIM - ROPE_DIM)], axis=1)
    krb = jnp.concatenate([zeros(NOPE_DIM), -kr[:, hr:], kr[:, :hr], dt,
                           zeros(HEAD_SLOT - DT_LANE0 - SSM_HEADS)], axis=1)
    return jnp.concatenate([cq, ckv, kra, krb, z, xbc, ga, gs], axis=1).astype(BF16)


def _prep_w_uq(w):
    w = w.reshape(Q_RANK, MLA_HEADS, NOPE_DIM + ROPE_DIM)
    hr = ROPE_DIM // 2
    nope, r1, r2 = w[..., :NOPE_DIM], w[..., NOPE_DIM:NOPE_DIM + hr], w[..., NOPE_DIM + hr:]
    z = jnp.zeros((Q_RANK, MLA_HEADS, HEAD_SLOT - NOPE_DIM - ROPE_DIM), w.dtype)
    wa = jnp.concatenate([nope, r1, r2, z], axis=-1).reshape(Q_RANK, -1)
    wb = jnp.concatenate([jnp.zeros_like(nope), -r2, r1, z], axis=-1).reshape(Q_RANK, -1)
    return jnp.concatenate([wa, wb], axis=1).astype(BF16)


def _prep_w_ukv(w):
    w = w.reshape(KV_RANK, MLA_HEADS, NOPE_DIM + V_DIM)
    kn, v = w[..., :NOPE_DIM], w[..., NOPE_DIM:]
    wk = jnp.concatenate([kn, jnp.zeros((KV_RANK, MLA_HEADS, HEAD_SLOT - NOPE_DIM), w.dtype)], axis=-1)
    vslot = jnp.concatenate([v, jnp.zeros((KV_RANK, MLA_HEADS, V_SLOT - V_DIM), w.dtype)], axis=-1)
    return wk.reshape(KV_RANK, -1).astype(BF16), vslot.reshape(KV_RANK, -1).T.astype(BF16)


def _rope_tables(pos):
    inv = ROPE_THETA ** (-jnp.arange(0, ROPE_DIM, 2, dtype=F32) / ROPE_DIM)
    ang = pos.astype(F32)[:, None] * inv[None, :]
    cos, sin = jnp.cos(ang), jnp.sin(ang)
    n = pos.shape[0]
    pad = jnp.zeros((n, HEAD_SLOT - NOPE_DIM - ROPE_DIM), F32)
    scale = (NOPE_DIM + ROPE_DIM) ** -0.5 * float(np.log2(np.e))
    cq = jnp.concatenate([jnp.ones((n, NOPE_DIM), F32), cos, cos, pad], axis=1) * scale
    sq = jnp.concatenate([jnp.zeros((n, NOPE_DIM), F32), sin, sin, pad], axis=1) * scale
    ck = jnp.concatenate([jnp.zeros((n, NOPE_DIM), F32), cos, cos, pad], axis=1)
    sk = jnp.concatenate([jnp.zeros((n, NOPE_DIM), F32), sin, sin, pad], axis=1)
    return jnp.tile(cq, (1, 2)), jnp.tile(sq, (1, 2)), ck, sk


def _head_lanes(v):
    return jnp.zeros((1, LANES), F32).at[0, DT_LANE0:DT_LANE0 + SSM_HEADS].set(v.astype(F32))


def _pick(n, prefs):
    for t in prefs:
        if n % t == 0:
            return t
    raise ValueError(f"no tile for {n}")


def kernel(x, meta_tokens, norm_mix, w_in, mla_q_norm, mla_w_uq, mla_kv_norm, mla_w_ukv, ssm_conv_w, ssm_conv_b, ssm_dt_bias, ssm_a_log, ssm_d_skip, ssm_norm, w_branch_attn, w_branch_ssm, w_out, norm_ffn, moe_w_group, moe_b_group, moe_w_expert, moe_b_expert, moe_w_gate, moe_w_up, moe_w_down, norm_final):
    b, seq, _ = x.shape
    assert w_in.shape[0] == 1, "one layer"
    rows = b * seq
    x2d = x.reshape(rows, D_MODEL)
    tm = _pick(seq, (512, 256, 128))
    tq = _pick(seq, (512, 256, 128))
    tk = _pick(tq, (256, 128))
    chunk = 128
    ssd_blk = _pick(seq, (256, 128))

    w_in_r = _prep_w_in(w_in[0])
    wq = _prep_w_uq(mla_w_uq[0])
    wk, wvt = _prep_w_ukv(mla_w_ukv[0])
    g_mix = norm_mix[0].reshape(1, D_MODEL)
    qn = mla_q_norm[0].reshape(1, Q_RANK)
    kvn = mla_kv_norm[0].reshape(1, KV_RANK)
    cw = ssm_conv_w[0]
    cb = ssm_conv_b[0].reshape(1, CONV_DIM)
    dtb = _head_lanes(ssm_dt_bias[0])
    aneg = _head_lanes(-jnp.exp(ssm_a_log[0].astype(F32)))
    dsk = jnp.repeat(ssm_d_skip[0].astype(F32), SSM_HEAD_DIM).reshape(1, D_INNER)
    ng = ssm_norm[0].reshape(1, D_INNER)
    wa = w_branch_attn[0].astype(BF16)
    ws = w_branch_ssm[0].astype(BF16)
    wo = w_out[0].astype(BF16)
    nffn = norm_ffn[0].reshape(1, D_MODEL)
    wr = jnp.zeros((D_MODEL, LANES), F32)
    wr = wr.at[:, :N_EXPERT_GROUPS].set(moe_w_group[0]).at[:, EXPERT_LANE0:EXPERT_LANE0 + N_EXPERTS].set(moe_w_expert[0])
    wr = wr.astype(BF16)
    br = jnp.zeros((1, LANES), F32)
    br = br.at[0, :N_EXPERT_GROUPS].set(moe_b_group[0]).at[0, EXPERT_LANE0:EXPERT_LANE0 + N_EXPERTS].set(moe_b_expert[0])
    wg, wu, wd = moe_w_gate[0], moe_w_up[0], moe_w_down[0]
    nfin = norm_final.reshape(1, D_MODEL)

    meta_blk = jnp.concatenate([jnp.zeros((META_PAD, D_MODEL), F32), meta_tokens.astype(F32)], axis=0)
    m_cq, m_ckv, m_small, m_sz, m_xbc, _, _ = _inproj(meta_blk, g_mix, w_in_r, META_BLOCK)
    meta_pos = jnp.maximum(jnp.arange(META_BLOCK) - META_PAD, 0)
    _, kmeta, vmeta_t = _mla_prep(m_cq, m_ckv, m_small, qn, kvn, wq, wk, wvt, _rope_tables(meta_pos),
                                  META_BLOCK, 1, META_BLOCK)
    zero_state = jnp.zeros((LANES, D_INNER), F32)
    zero_halo = jnp.zeros((HALO, CONV_DIM), BF16)
    _, meta_state = _ssd(m_xbc[None], m_small[None], m_sz[None], zero_halo, zero_state,
                         cw, cb, dtb, aneg, dsk, ng, META_BLOCK, META_BLOCK, META_PAD)
    meta_halo = m_xbc[META_BLOCK - HALO:]

    cq, ckv, small, sz, xbc, ga, gs = _inproj(x2d, g_mix, w_in_r, tm)
    q, k, vt = _mla_prep(cq, ckv, small, qn, kvn, wq, wk, wvt, _rope_tables(N_META + jnp.arange(seq)),
                         tm, seq // tm, tk)
    o_attn = _flash(q.reshape(b, seq, -1), k.reshape(b, seq, -1), vt.reshape(b, seq // tk, -1, tk),
                    kmeta, vmeta_t[0], tq)
    o_ssm, _ = _ssd(xbc.reshape(b, seq, -1), small.reshape(b, seq, -1), sz.reshape(b, seq, -1), meta_halo,
                    meta_state[0], cw, cb, dtb, aneg, dsk, ng, ssd_blk, chunk, 0)
    h2, v, rt, counts = _merge(x2d, o_attn.reshape(rows, -1), o_ssm.reshape(rows, -1), ga, gs, wa, ws, wo,
                               nffn, wr, br, tm)
    tme = _pick(2 * rows, (256, 128))
    tile_of, exp_of, offs, nwork = _moe_plan(counts[0, EXPERT_LANE0:EXPERT_LANE0 + N_EXPERTS], rows, tme)
    pos = _slots(rt, counts, tm)
    xs = _dispatch(pos, v, tm)
    ys = _experts(tile_of, exp_of, offs, nwork, xs, wg, wu, wd, tme)
    out = _combine(pos, rt, h2, nfin, ys, tm)
    return out.reshape(b, seq, D_MODEL)
```

```python
import functools

import numpy as np
import jax
import jax.numpy as jnp
from jax import lax
from jax.experimental import pallas as pl
from jax.experimental.pallas import tpu as pltpu

F32 = jnp.float32
BF16 = jnp.bfloat16

D_MODEL = 1024
N_META = 16
EPS = 1e-6
NEG = -1e30
MLA_HEADS = 16
Q_RANK = 256
KV_RANK = 128
NOPE_DIM = 64
ROPE_DIM = 32
V_DIM = 64
ROPE_THETA = 10000.0
SSM_HEADS = 16
SSM_HEAD_DIM = 64
D_INNER = SSM_HEADS * SSM_HEAD_DIM
SSM_GROUPS = 2
D_STATE = 64
CONV_WIDTH = 4
CONV_DIM = D_INNER + 2 * SSM_GROUPS * D_STATE
N_EXPERT_GROUPS = 4
EXPERTS_PER_GROUP = 8
N_EXPERTS = N_EXPERT_GROUPS * EXPERTS_PER_GROUP
D_EXPERT = 256

LANES = 128
META_BLOCK = 128
META_PAD = META_BLOCK - N_META
HEAD_SLOT = 128
V_SLOT = 80
FLASH_HEADS = 4
EXPERT_LANE0 = 32
DT_LANE0 = 96
VMEM_LIMIT = 48 * 1024 * 1024

A_CQ, A_CKV, A_SMALL, A_Z, A_XBC, A_GA, A_GS = 256, 128, 256, D_INNER, CONV_DIM, D_MODEL, D_MODEL
A_OFFS = np.cumsum([0, A_CQ, A_CKV, A_SMALL, A_Z, A_XBC, A_GA, A_GS])


def _cparams(sem):
    return pltpu.CompilerParams(dimension_semantics=sem, vmem_limit_bytes=VMEM_LIMIT)


def _rms(x, g):
    return x * lax.rsqrt(jnp.mean(x * x, axis=-1, keepdims=True) + EPS) * g


def _sigmoid(x):
    return 1.0 / (1.0 + jnp.exp(-x))


def _dot(a, b):
    return jnp.dot(a, b, preferred_element_type=F32)


def _dot_nt(a, b):
    return lax.dot_general(a, b, (((1,), (1,)), ((), ())), preferred_element_type=F32)


def _dot_tn(a, b):
    return lax.dot_general(a, b, (((0,), (0,)), ((), ())), preferred_element_type=F32)


def _split3(x):
    hi = x.astype(BF16)
    r1 = x - hi.astype(F32)
    mid = r1.astype(BF16)
    lo = (r1 - mid.astype(F32)).astype(BF16)
    return hi, mid, lo


def _dot_exact_rhs(x, m):
    hi, mid, lo = _split3(x)
    return _dot(hi, m) + _dot(mid, m) + _dot(lo, m)


def _dot_exact_lhs(m, x):
    hi, mid, lo = _split3(x)
    return _dot(m, hi) + _dot(m, mid) + _dot(m, lo)


def _inproj_body(x_ref, g_ref, w_ref, cq_ref, ckv_ref, small_ref, sz_ref, xbc_ref, ga_ref, gs_ref):
    u = _rms(x_ref[...], g_ref[...]).astype(BF16)

    def mm(i):
        return _dot(u, w_ref[:, A_OFFS[i]:A_OFFS[i + 1]])

    cq_ref[...] = mm(0).astype(BF16)
    ckv_ref[...] = mm(1).astype(BF16)
    small_ref[...] = mm(2)
    z = mm(3)
    sz_ref[...] = (z * _sigmoid(z)).astype(BF16)
    xbc_ref[...] = mm(4).astype(BF16)
    ga_ref[...] = _sigmoid(mm(5)).astype(BF16)
    gs_ref[...] = _sigmoid(mm(6)).astype(BF16)


def _inproj(x2d, g, w, tm):
    r = x2d.shape[0]
    widths = (A_CQ, A_CKV, A_SMALL, A_Z, A_XBC, A_GA, A_GS)
    dtypes = (BF16, BF16, F32, BF16, BF16, BF16, BF16)
    return pl.pallas_call(
        _inproj_body,
        grid=(r // tm,),
        in_specs=[
            pl.BlockSpec((tm, D_MODEL), lambda i: (i, 0)),
            pl.BlockSpec((1, D_MODEL), lambda i: (0, 0)),
            pl.BlockSpec(w.shape, lambda i: (0, 0), pipeline_mode=pl.Buffered(1)),
        ],
        out_specs=[pl.BlockSpec((tm, n), lambda i: (i, 0)) for n in widths],
        out_shape=[jax.ShapeDtypeStruct((r, n), dt) for n, dt in zip(widths, dtypes)],
        compiler_params=_cparams(("parallel",)),
        name="inproj",
    )(x2d, g, w)


def _mla_prep_body(cq_ref, ckv_ref, small_ref, qn_ref, kvn_ref, wqt_ref, wk_ref, wvt_ref, vone_ref,
                   cqt_ref, sqt_ref, ckt_ref, skt_ref, qt_ref, k_ref, vt_ref):
    cn = _rms(cq_ref[...].astype(F32), qn_ref[...]).astype(BF16)
    kn = _rms(ckv_ref[...].astype(F32), kvn_ref[...]).astype(BF16)
    cqt = cqt_ref[...]
    sqt = sqt_ref[...]
    all_heads = MLA_HEADS * HEAD_SLOT
    part = all_heads // 2
    tqb = qt_ref.shape[-1]
    for lo in range(0, all_heads, part):
        qa = _dot_nt(wqt_ref[lo:lo + part, :], cn)
        qb = _dot_nt(wqt_ref[all_heads + lo:all_heads + lo + part, :], cn)
        for r0 in range(0, part, HEAD_SLOT):
            qh = (qa[r0:r0 + HEAD_SLOT] * cqt + qb[r0:r0 + HEAD_SLOT] * sqt).astype(BF16)
            for j in range(qt_ref.shape[0]):
                qt_ref[j, lo + r0:lo + r0 + HEAD_SLOT, :] = qh[:, j * tqb:(j + 1) * tqb]
    krp = small_ref[:, 0:LANES] * ckt_ref[...] + small_ref[:, LANES:2 * LANES] * skt_ref[...]
    krp2 = jnp.concatenate([krp, krp], axis=1)
    for hp in range(MLA_HEADS // 2):
        lo, hi = hp * 2 * HEAD_SLOT, (hp + 1) * 2 * HEAD_SLOT
        k_ref[:, lo:hi] = (_dot(kn, wk_ref[:, lo:hi]) + krp2).astype(BF16)
    vt = _dot_nt(wvt_ref[...], kn) + vone_ref[...]
    tkv = vt_ref.shape[-1]
    for j in range(vt_ref.shape[0]):
        vt_ref[j] = vt[:, j * tkv:(j + 1) * tkv].astype(BF16)


def _mla_prep(cq, ckv, small, qn, kvn, wqt, wk, wvt, tabs, tm, seq_blocks, tqb, tkv):
    r = cq.shape[0]
    nq = MLA_HEADS * HEAD_SLOT
    nv = MLA_HEADS * V_SLOT
    cqt, sqt, ckt, skt = tabs
    ones_row = (jnp.arange(nv) % V_SLOT == V_DIM).astype(F32)
    vone = jnp.broadcast_to(ones_row[:, None], (nv, tm))
    row = lambda n: pl.BlockSpec((tm, n), lambda i: (i, 0))
    tab = lambda n: pl.BlockSpec((tm, n), lambda i: (i % seq_blocks, 0))
    tab_t = pl.BlockSpec((HEAD_SLOT, tm), lambda i: (0, i % seq_blocks))
    full = lambda a: pl.BlockSpec(a.shape, lambda i: (0, 0))
    return pl.pallas_call(
        _mla_prep_body,
        grid=(r // tm,),
        in_specs=[row(Q_RANK), row(KV_RANK), row(A_SMALL), full(qn), full(kvn), full(wqt), full(wk), full(wvt),
                  full(vone), tab_t, tab_t, tab(HEAD_SLOT), tab(HEAD_SLOT)],
        out_specs=[pl.BlockSpec((tm // tqb, nq, tqb), lambda i: (i, 0, 0)), row(nq),
                   pl.BlockSpec((tm // tkv, nv, tkv), lambda i: (i, 0, 0))],
        out_shape=[jax.ShapeDtypeStruct((r // tqb, nq, tqb), BF16),
                   jax.ShapeDtypeStruct((r, nq), BF16),
                   jax.ShapeDtypeStruct((r // tkv, nv, tkv), BF16)],
        compiler_params=_cparams(("parallel",)),
        name="mla_prep",
    )(cq, ckv, small, qn, kvn, wqt, wk, wvt, vone, cqt, sqt, ckt, skt)


def _flash_body(qt_ref, k_ref, vt_ref, km_ref, vmt_ref, o_ref, *scratch, tq, tk, seq):
    nh = FLASH_HEADS
    s_scr = (scratch[0:nh], scratch[nh:2 * nh])
    p_scr = (scratch[2 * nh:3 * nh], scratch[3 * nh:4 * nh])
    acc_scr = scratch[4 * nh:5 * nh]
    key_i = lax.broadcasted_iota(jnp.int32, (tk, tq), 0)
    qry_i = lax.broadcasted_iota(jnp.int32, (tk, tq), 1)
    meta_row = lax.broadcasted_iota(jnp.int32, (META_BLOCK, 1), 0)
    heads = [slice(hh * HEAD_SLOT, (hh + 1) * HEAD_SLOT) for hh in range(nh)]
    vals = [slice(hh * V_SLOT, (hh + 1) * V_SLOT) for hh in range(nh)]
    ndiag = tq // tk
    assert ndiag % 2 == 0, "slot parity is static only when a query block spans an even number of key blocks"

    def probs(st, m):
        return jnp.exp2((st - m).astype(BF16))

    def qblock(qi, _):
        q0 = pl.multiple_of(qi * tq, tq)
        qs = [qt_ref[qi, hs, :] for hs in heads]
        nfull = qi * ndiag
        last = nfull + ndiag - 1

        def scores_into(slot, kj):
            k0 = pl.multiple_of(kj * tk, tk)
            for hh in range(nh):
                s_scr[slot][hh][...] = _dot(k_ref[pl.ds(k0, tk), heads[hh]], qs[hh])

        ms = []
        for hh in range(nh):
            st = jnp.where(meta_row >= META_PAD, _dot(km_ref[:, heads[hh]], qs[hh]), NEG)
            m = jnp.max(st, axis=0, keepdims=True)
            ms.append(m)
            acc_scr[hh][...] = _dot(vmt_ref[vals[hh], :], probs(st, m))
            p_scr[1][hh][...] = jnp.zeros((tk, tq), BF16)
        scores_into(0, 0)

        def kvstep(kj, slot, carry, masked):
            ms, alphas = carry
            kprev = jnp.maximum(kj - 1, 0)
            pvs = [_dot(vt_ref[kprev, vals[hh], :], p_scr[1 - slot][hh][...]) for hh in range(nh)]
            scores_into(1 - slot, jnp.minimum(kj + 1, last))
            out = ([], [])
            for hh in range(nh):
                st = s_scr[slot][hh][...]
                if masked:
                    st = jnp.where(kj * tk + key_i <= q0 + qry_i, st, NEG)
                m_new = jnp.maximum(ms[hh], jnp.max(st, axis=0, keepdims=True))
                p_scr[slot][hh][...] = probs(st, m_new)
                acc_scr[hh][...] = alphas[hh] * acc_scr[hh][...] + pvs[hh]
                out[0].append(m_new)
                out[1].append(jnp.exp2(ms[hh] - m_new))
            return tuple(tuple(x) for x in out)

        def two_steps(t, carry):
            carry = kvstep(2 * t, 0, carry, False)
            return kvstep(2 * t + 1, 1, carry, False)

        ones = jnp.ones((1, tq), F32)
        carry = lax.fori_loop(0, nfull // 2, two_steps, (tuple(ms), (ones,) * nh))
        for d in range(ndiag):
            carry = kvstep(nfull + d, d % 2, carry, True)
        _, alphas = carry
        outs = []
        for hh in range(nh):
            acc = alphas[hh] * acc_scr[hh][...] + _dot(vt_ref[last, vals[hh], :], p_scr[(ndiag - 1) % 2][hh][...])
            outs.append(acc[:V_DIM] / acc[V_DIM:V_DIM + 1])
        o_ref[pl.ds(q0, tq), :] = jnp.concatenate(outs, axis=0).T.astype(BF16)
        return 0

    lax.fori_loop(0, seq // tq, qblock, 0)


def _flash(qt, k, vt, kmeta, vmeta_t):
    b, seq, _ = k.shape
    tq = qt.shape[-1]
    tk = vt.shape[-1]
    nh = FLASH_HEADS
    return pl.pallas_call(
        functools.partial(_flash_body, tq=tq, tk=tk, seq=seq),
        grid=(b, MLA_HEADS // nh),
        in_specs=[
            pl.BlockSpec((None, seq // tq, nh * HEAD_SLOT, tq), lambda i, p: (i, 0, p, 0)),
            pl.BlockSpec((None, seq, nh * HEAD_SLOT), lambda i, p: (i, 0, p)),
            pl.BlockSpec((None, seq // tk, nh * V_SLOT, tk), lambda i, p: (i, 0, p, 0)),
            pl.BlockSpec((META_BLOCK, nh * HEAD_SLOT), lambda i, p: (0, p)),
            pl.BlockSpec((nh * V_SLOT, META_BLOCK), lambda i, p: (p, 0)),
        ],
        out_specs=pl.BlockSpec((None, seq, nh * V_DIM), lambda i, p: (i, 0, p)),
        out_shape=jax.ShapeDtypeStruct((b, seq, MLA_HEADS * V_DIM), BF16),
        scratch_shapes=([pltpu.VMEM((tk, tq), F32)] * (2 * nh) + [pltpu.VMEM((tk, tq), BF16)] * (2 * nh)
                        + [pltpu.VMEM((V_SLOT, tq), F32)] * nh),
        compiler_params=_cparams(("parallel", "parallel")),
        name="flash",
    )(qt, k, vt, kmeta, vmeta_t)


HALO = 16


def _expand2(x, expand):
    hi = x.astype(BF16)
    mid = (x - hi.astype(F32)).astype(BF16)
    return _dot(hi, expand) + _dot(mid, expand)


def _ssd_body(xbc_ref, halo_ref, mh_ref, small_ref, sz_ref, cw_ref, cb_ref, dtb_ref, aneg_ref, dsk_ref,
              ng_ref, init_ref, o_ref, fin_ref, st_ref, y_ref, *, q, n_pad):
    c = pl.program_id(1)
    nc = pl.num_programs(1)

    @pl.when(c == 0)
    def _():
        st_ref[...] = init_ref[...]

    state = st_ref[...]
    halo = jnp.where(c == 0, mh_ref[...], halo_ref[...])
    for j in range(xbc_ref.shape[0] // q):
        rows = slice(j * q, (j + 1) * q)
        x_b = xbc_ref[rows, :]
        state = _ssd_chunk(x_b, halo, small_ref[rows, LANES:2 * LANES], sz_ref[rows, :], state,
                           cw_ref, cb_ref, dtb_ref, aneg_ref, dsk_ref, ng_ref, o_ref.at[rows, :], y_ref.at[rows, :],
                           q=q, n_pad=n_pad - j * q)
        halo = x_b[q - HALO:, :]
    st_ref[...] = state

    @pl.when(c == nc - 1)
    def _():
        fin_ref[...] = state


def _ssd_chunk(x_b, halo, dt_raw, sz, prev, cw_ref, cb_ref, dtb_ref, aneg_ref, dsk_ref, ng_ref, o_ref, y_ref,
               *, q, n_pad):
    x_ext = jnp.concatenate([halo, x_b], axis=0)
    sh_r = lax.broadcasted_iota(jnp.int32, (q, HALO + q), 0)
    sh_c = lax.broadcasted_iota(jnp.int32, (q, HALO + q), 1)
    xc = cb_ref[...] + cw_ref[CONV_WIDTH - 1:CONV_WIDTH, :] * x_b.astype(F32)
    for d in range(1, CONV_WIDTH):
        shifted = _dot((sh_c == sh_r + (HALO - d)).astype(BF16), x_ext)
        xc = xc + cw_ref[CONV_WIDTH - 1 - d:CONV_WIDTH - d, :] * shifted
    xc = xc * _sigmoid(xc)

    lane = lax.broadcasted_iota(jnp.int32, (q, LANES), 1)
    dt = dt_raw + dtb_ref[...]
    dt = jnp.maximum(dt, 0.0) + jnp.log(1.0 + jnp.exp(-jnp.abs(dt)))
    dt = jnp.where((lane >= DT_LANE0) & (lane < DT_LANE0 + SSM_HEADS), dt, 0.0)
    if n_pad > 0:
        rowv = lax.broadcasted_iota(jnp.int32, (q, 1), 0) >= n_pad
        xc = jnp.where(rowv, xc, 0.0)
        dt = jnp.where(rowv, dt, 0.0)

    xs = xc[:, :D_INNER]
    bm = xc[:, D_INNER:D_INNER + LANES]
    cm = xc[:, D_INNER + LANES:]

    a = dt * aneg_ref[...]
    ri = lax.broadcasted_iota(jnp.int32, (q, q), 0)
    ci = lax.broadcasted_iota(jnp.int32, (q, q), 1)
    tril = ri >= ci
    a_cs = _dot_exact_lhs(tril.astype(BF16), a)
    a_cs_t = a_cs.T

    er = lax.broadcasted_iota(jnp.int32, (LANES, D_INNER), 0)
    ec = lax.broadcasted_iota(jnp.int32, (LANES, D_INNER), 1)
    expand = (er - DT_LANE0 == ec // SSM_HEAD_DIM).astype(BF16)
    dt_x = _expand2(dt, expand)
    grow_x = _expand2(jnp.exp(a_cs), expand)
    rest_x = _expand2(jnp.exp(a_cs[q - 1:q, :] - a_cs), expand)

    xdt = xs * dt_x
    xdt_b = xdt.astype(BF16)
    xd_b = (xdt * rest_x).astype(BF16)

    bm_b = bm.astype(BF16)
    cm_b = cm.astype(BF16)
    lane_q = lax.broadcasted_iota(jnp.int32, (q, LANES), 1)

    for g in range(SSM_GROUPS):
        in_g = (lane_q >= g * D_STATE) & (lane_q < (g + 1) * D_STATE)
        cb = _dot_nt(jnp.where(in_g, cm, 0.0).astype(BF16), bm_b)
        for pr in range(SSM_HEADS // SSM_GROUPS // 2):
            pair = g * (SSM_HEADS // SSM_GROUPS // 2) + pr
            rhs = xdt_b[:, pair * LANES:(pair + 1) * LANES]
            ys = []
            for hh in range(2):
                e = DT_LANE0 + 2 * pair + hh
                seg = a_cs[:, e:e + 1] - a_cs_t[e:e + 1, :]
                lmat = jnp.exp(jnp.where(tril, seg, NEG))
                ys.append(_dot((cb * lmat).astype(BF16), rhs))
            y_ref[:, pair * LANES:(pair + 1) * LANES] = jnp.where(lane_q < SSM_HEAD_DIM, ys[0], ys[1])

    sr = lax.broadcasted_iota(jnp.int32, (LANES, D_INNER), 0)
    sc = lax.broadcasted_iota(jnp.int32, (LANES, D_INNER), 1)
    same_group = (sr // D_STATE) == (sc // (D_INNER // SSM_GROUPS))
    y_off = _dot(cm_b, prev.astype(BF16)) * grow_x
    st_new = prev * grow_x[q - 1:q, :] + jnp.where(same_group, _dot_tn(bm_b, xd_b), 0.0)

    y = y_ref[...] + y_off + xs * dsk_ref[...]
    yz = y * sz.astype(F32)
    half = D_INNER // SSM_GROUPS
    outs = []
    for g in range(SSM_GROUPS):
        part = yz[:, g * half:(g + 1) * half]
        outs.append(part * lax.rsqrt(jnp.mean(part * part, axis=-1, keepdims=True) + EPS))
    o_ref[...] = (jnp.concatenate(outs, axis=1) * ng_ref[...]).astype(BF16)
    return st_new


def _ssd(xbc, small, sz, meta_halo, init_state, cw, cb, dtb, aneg, dsk, ng, blk, q, n_pad):
    b, seq, _ = xbc.shape
    nc = seq // blk
    hb = blk // HALO
    full = lambda a: pl.BlockSpec(a.shape, lambda i, c: (0,) * a.ndim)
    return pl.pallas_call(
        functools.partial(_ssd_body, q=q, n_pad=n_pad),
        grid=(b, nc),
        in_specs=[
            pl.BlockSpec((None, blk, CONV_DIM), lambda i, c: (i, c, 0)),
            pl.BlockSpec((None, HALO, CONV_DIM), lambda i, c: (i, jnp.maximum(c * hb - 1, 0), 0)),
            full(meta_halo),
            pl.BlockSpec((None, blk, A_SMALL), lambda i, c: (i, c, 0)),
            pl.BlockSpec((None, blk, D_INNER), lambda i, c: (i, c, 0)),
            full(cw), full(cb), full(dtb), full(aneg), full(dsk), full(ng), full(init_state),
        ],
        out_specs=[
            pl.BlockSpec((None, blk, D_INNER), lambda i, c: (i, c, 0)),
            pl.BlockSpec((None, LANES, D_INNER), lambda i, c: (i, 0, 0)),
        ],
        out_shape=[jax.ShapeDtypeStruct((b, seq, D_INNER), BF16),
                   jax.ShapeDtypeStruct((b, LANES, D_INNER), F32)],
        scratch_shapes=[pltpu.VMEM((LANES, D_INNER), F32),
                        pltpu.VMEM((blk, D_INNER), F32)],
        compiler_params=_cparams(("parallel", "arbitrary")),
        name="ssd",
    )(xbc, xbc, meta_halo, small, sz, cw, cb, dtb, aneg, dsk, ng, init_state)


def _route(logits):
    lane = lax.broadcasted_iota(jnp.int32, logits.shape, 1).astype(F32)
    gl = jnp.where(lane < N_EXPERT_GROUPS, logits, NEG)
    gmax = jnp.max(gl, axis=1, keepdims=True)
    gidx = jnp.min(jnp.where(gl == gmax, lane, float(LANES)), axis=1, keepdims=True)
    p_g = 1.0 / jnp.sum(jnp.exp(gl - gmax), axis=1, keepdims=True)
    lo = EXPERT_LANE0 + EXPERTS_PER_GROUP * gidx
    el = jnp.where((lane >= lo) & (lane < lo + EXPERTS_PER_GROUP), logits, NEG)
    t1 = jnp.max(el, axis=1, keepdims=True)
    i1 = jnp.min(jnp.where(el == t1, lane, float(LANES)), axis=1, keepdims=True)
    el2 = jnp.where(lane == i1, NEG, el)
    t2 = jnp.max(el2, axis=1, keepdims=True)
    i2 = jnp.min(jnp.where(el2 == t2, lane, float(LANES)), axis=1, keepdims=True)
    e21 = jnp.exp(t2 - t1)
    w1 = p_g / (1.0 + e21)
    w2 = w1 * e21
    return lane, i1, i2, w1, w2


RT_E1, RT_E2, RT_R1, RT_R2, RT_W1, RT_W2 = range(6)


def _merge_body(x_ref, oa_ref, os_ref, ga_ref, gs_ref, wa_ref, ws_ref, wo_ref, nf_ref, wr_ref, br_ref,
                h2_ref, v_ref, rt_ref, cnt_out_ref, cnt_ref):
    @pl.when(pl.program_id(0) == 0)
    def _():
        cnt_ref[...] = jnp.zeros_like(cnt_ref)

    merged = (ga_ref[...].astype(F32) * _dot(oa_ref[...], wa_ref[...])
              + gs_ref[...].astype(F32) * _dot(os_ref[...], ws_ref[...]))
    h2 = x_ref[...] + _dot(merged.astype(BF16), wo_ref[...])
    h2_ref[...] = h2
    v = _rms(h2, nf_ref[...])
    _store_token_slabs(v_ref, v)
    lane, i1, i2, w1, w2 = _route(_dot(v.astype(BF16), wr_ref[...]) + br_ref[...])

    tm = lane.shape[0]
    chosen = jnp.where((lane == i1) | (lane == i2), 1.0, 0.0)
    ri = lax.broadcasted_iota(jnp.int32, (tm, tm), 0)
    ci = lax.broadcasted_iota(jnp.int32, (tm, tm), 1)
    earlier = _dot((ri > ci).astype(BF16), chosen.astype(BF16)) + cnt_ref[...]
    r1 = jnp.sum(jnp.where(lane == i1, earlier, 0.0), axis=1, keepdims=True)
    r2 = jnp.sum(jnp.where(lane == i2, earlier, 0.0), axis=1, keepdims=True)
    cnt_ref[...] += jnp.sum(chosen, axis=0, keepdims=True)
    cnt_out_ref[...] = jnp.broadcast_to(cnt_ref[...], cnt_out_ref.shape)

    rec = jnp.zeros_like(lane)
    for col, val in ((RT_E1, i1 - EXPERT_LANE0), (RT_E2, i2 - EXPERT_LANE0), (RT_R1, r1), (RT_R2, r2),
                     (RT_W1, w1), (RT_W2, w2)):
        rec = jnp.where(lane == col, val, rec)
    rt_ref[...] = rec


def _merge(x2d, oa, osm, ga, gs, wa, ws, wo, nf, wr, br, tm):
    r = x2d.shape[0]
    row = lambda n: pl.BlockSpec((tm, n), lambda i: (i, 0))
    full = lambda a: pl.BlockSpec(a.shape, lambda i: (0, 0))
    return pl.pallas_call(
        _merge_body,
        grid=(r // tm,),
        in_specs=[row(D_MODEL)] * 5 + [full(wa), full(ws), full(wo), full(nf), full(wr), full(br)],
        out_specs=[row(D_MODEL), pl.BlockSpec((tm * SLAB, LANES), lambda i: (i, 0)), row(LANES),
                   pl.BlockSpec((8, LANES), lambda i: (0, 0))],
        out_shape=[jax.ShapeDtypeStruct((r, D_MODEL), F32),
                   jax.ShapeDtypeStruct((r * SLAB, LANES), F32),
                   jax.ShapeDtypeStruct((r, LANES), F32),
                   jax.ShapeDtypeStruct((8, LANES), F32)],
        scratch_shapes=[pltpu.VMEM((1, LANES), F32)],
        compiler_params=_cparams(("arbitrary",)),
        name="merge",
    )(x2d, oa, osm, ga, gs, wa, ws, wo, nf, wr, br)


SLAB = D_MODEL // LANES


def _store_token_slabs(ref, x):
    tm = x.shape[0]
    for s in range(SLAB):
        ref[pl.ds(s, tm, stride=SLAB), :] = x[:, s * LANES:(s + 1) * LANES]


def _load_token_slabs(ref):
    tm = ref.shape[0] // SLAB
    return jnp.concatenate([ref[pl.ds(s, tm, stride=SLAB), :] for s in range(SLAB)], axis=1)


def _row_copy(src_ref, src_row, dst_ref, dst_row, sem):
    src = src_ref.at[pl.ds(pl.multiple_of(src_row * SLAB, SLAB), SLAB)]
    dst = dst_ref.at[pl.ds(pl.multiple_of(dst_row * SLAB, SLAB), SLAB)]
    return pltpu.make_async_copy(src, dst, sem)


def _dispatch_body(pos_ref, v_ref, xs_ref, sem):
    tm = v_ref.shape[0] // SLAB

    def issue(t, _):
        _row_copy(v_ref, t, xs_ref, pos_ref[0, t], sem).start(priority=0)
        _row_copy(v_ref, t, xs_ref, pos_ref[1, t], sem).start(priority=1)
        return 0

    lax.fori_loop(0, tm, issue, 0, unroll=8)
    for _ in range(2):
        pltpu.make_async_copy(v_ref, xs_ref.at[pl.ds(0, tm * SLAB)], sem).wait()


def _slot_spec(tm):
    return pl.BlockSpec((8, tm), lambda i: (0, i), memory_space=pltpu.SMEM)


def _dispatch(pos, v, tm):
    r = v.shape[0] // SLAB
    return pl.pallas_call(
        _dispatch_body,
        grid=(r // tm,),
        in_specs=[_slot_spec(tm), pl.BlockSpec((tm * SLAB, LANES), lambda i: (i, 0))],
        out_specs=pl.BlockSpec(memory_space=pl.ANY),
        out_shape=jax.ShapeDtypeStruct((2 * r * SLAB, LANES), F32),
        scratch_shapes=[pltpu.SemaphoreType.DMA],
        compiler_params=_cparams(("arbitrary",)),
        name="moe_dispatch",
    )(pos, v)


def _experts_body(tile_ref, exp_ref, off_ref, nwork_ref, xs_ref, wg_ref, wu_ref, wd_ref, ys_ref,
                  wg_s, wu_s, wd_s):
    w = pl.program_id(0)
    tme = xs_ref.shape[0] // SLAB

    @pl.when(w < nwork_ref[0])
    def _():
        e = exp_ref[w]
        tile = tile_ref[w]
        prev = jnp.maximum(w - 1, 0)
        new_expert = (w == 0) | (exp_ref[prev] != e)
        new_tile = (w == 0) | (tile_ref[prev] != tile)

        @pl.when(new_expert)
        def _():
            wg_s[...] = wg_ref[...].astype(BF16)
            wu_s[...] = wu_ref[...].astype(BF16)
            wd_s[...] = wd_ref[...].astype(BF16)

        x = _load_token_slabs(xs_ref).astype(BF16)
        g = _dot(x, wg_s[...])
        u = _dot(x, wu_s[...])
        y = _dot((g * _sigmoid(g) * u).astype(BF16), wd_s[...])
        rows = tile * tme + lax.broadcasted_iota(jnp.int32, (tme, 1), 0)
        mine = (rows >= off_ref[e]) & (rows < off_ref[e + 1])

        @pl.when(new_tile)
        def _():
            _store_token_slabs(ys_ref, jnp.where(mine, y, 0.0))

        @pl.when(jnp.logical_not(new_tile))
        def _():
            _store_token_slabs(ys_ref, jnp.where(mine, y, _load_token_slabs(ys_ref)))


def _experts(tile_of, exp_of, offs, nwork, xs, wg, wu, wd, tme):
    nw = tile_of.shape[0]
    grid_spec = pltpu.PrefetchScalarGridSpec(
        num_scalar_prefetch=4,
        grid=(nw,),
        in_specs=[
            pl.BlockSpec((tme * SLAB, LANES), lambda w, t, e, o, n: (t[w], 0)),
            pl.BlockSpec((None, D_MODEL, D_EXPERT), lambda w, t, e, o, n: (e[w], 0, 0)),
            pl.BlockSpec((None, D_MODEL, D_EXPERT), lambda w, t, e, o, n: (e[w], 0, 0)),
            pl.BlockSpec((None, D_EXPERT, D_MODEL), lambda w, t, e, o, n: (e[w], 0, 0)),
        ],
        out_specs=pl.BlockSpec((tme * SLAB, LANES), lambda w, t, e, o, n: (t[w], 0)),
        scratch_shapes=[pltpu.VMEM((D_MODEL, D_EXPERT), BF16), pltpu.VMEM((D_MODEL, D_EXPERT), BF16),
                        pltpu.VMEM((D_EXPERT, D_MODEL), BF16)],
    )
    return pl.pallas_call(
        _experts_body,
        grid_spec=grid_spec,
        out_shape=jax.ShapeDtypeStruct(xs.shape, F32),
        compiler_params=_cparams(("arbitrary",)),
        name="moe_experts",
    )(tile_of, exp_of, offs, nwork, xs, wg, wu, wd)


def _combine_body(pos_ref, rt_ref, h2_ref, nf_ref, ys_ref, out_ref, y1_ref, y2_ref, sem):
    tm = h2_ref.shape[0]

    def issue(t, _):
        _row_copy(ys_ref, pos_ref[0, t], y1_ref, t, sem).start(priority=0)
        _row_copy(ys_ref, pos_ref[1, t], y2_ref, t, sem).start(priority=1)
        return 0

    lax.fori_loop(0, tm, issue, 0, unroll=8)
    for buf in (y1_ref, y2_ref):
        pltpu.make_async_copy(ys_ref.at[pl.ds(0, tm * SLAB)], buf, sem).wait()
    lane = lax.broadcasted_iota(jnp.int32, rt_ref.shape, 1)
    rt = rt_ref[...]
    w1 = jnp.sum(jnp.where(lane == RT_W1, rt, 0.0), axis=1, keepdims=True)
    w2 = jnp.sum(jnp.where(lane == RT_W2, rt, 0.0), axis=1, keepdims=True)
    moe = w1 * _load_token_slabs(y1_ref) + w2 * _load_token_slabs(y2_ref)
    out_ref[...] = _rms(h2_ref[...] + moe, nf_ref[...])


def _combine(pos, rt, h2, nf, ys, tm):
    r = h2.shape[0]
    return pl.pallas_call(
        _combine_body,
        grid=(r // tm,),
        in_specs=[_slot_spec(tm), pl.BlockSpec((tm, LANES), lambda i: (i, 0)),
                  pl.BlockSpec((tm, D_MODEL), lambda i: (i, 0)), pl.BlockSpec((1, D_MODEL), lambda i: (0, 0)),
                  pl.BlockSpec(memory_space=pl.ANY)],
        out_specs=pl.BlockSpec((tm, D_MODEL), lambda i: (i, 0)),
        out_shape=jax.ShapeDtypeStruct((r, D_MODEL), F32),
        scratch_shapes=[pltpu.VMEM((tm * SLAB, LANES), F32), pltpu.VMEM((tm * SLAB, LANES), F32),
                        pltpu.SemaphoreType.DMA],
        compiler_params=_cparams(("arbitrary",)),
        name="moe_combine",
    )(pos, rt, h2, nf, ys)


def _slots_body(rt_ref, cnt_ref, pos_ref):
    rt = rt_ref[...]
    lane = lax.broadcasted_iota(jnp.int32, rt.shape, 1).astype(F32)
    col = lambda c: jnp.sum(jnp.where(lane == c, rt, 0.0), axis=1, keepdims=True)
    ri = lax.broadcasted_iota(jnp.int32, (LANES, LANES), 0)
    ci = lax.broadcasted_iota(jnp.int32, (LANES, LANES), 1)
    first = _dot_exact_rhs(cnt_ref[...], (ri < ci).astype(BF16))[0:1, :]
    slot = lambda e, r: col(r) + jnp.sum(jnp.where(lane == col(e) + EXPERT_LANE0, first, 0.0),
                                         axis=1, keepdims=True)
    rec = jnp.where(lane == 0, slot(RT_E1, RT_R1), jnp.where(lane == 1, slot(RT_E2, RT_R2), 0.0))
    sr = lax.broadcasted_iota(jnp.int32, (8, LANES), 0)
    sc = lax.broadcasted_iota(jnp.int32, (8, LANES), 1)
    sel = (sr == sc).astype(BF16)
    pos_ref[...] = sum(_dot_nt(sel, part) for part in _split3(rec)).astype(jnp.int32)


def _slots(rt, counts, tm):
    r = rt.shape[0]
    return pl.pallas_call(
        _slots_body,
        grid=(r // tm,),
        in_specs=[pl.BlockSpec((tm, LANES), lambda i: (i, 0)), pl.BlockSpec((8, LANES), lambda i: (0, 0))],
        out_specs=pl.BlockSpec((8, tm), lambda i: (0, i)),
        out_shape=jax.ShapeDtypeStruct((8, r), jnp.int32),
        compiler_params=_cparams(("parallel",)),
        name="moe_slots",
    )(rt, counts)


def _moe_plan(counts, rows, tme):
    cnt = counts.astype(jnp.int32)
    ends = jnp.cumsum(cnt)
    offs = jnp.concatenate([jnp.zeros((1,), jnp.int32), ends])
    first_tile = offs[:-1] // tme
    last_tile = (ends - 1) // tme
    n_items = jnp.where(cnt > 0, last_tile - first_tile + 1, 0)
    item_end = jnp.cumsum(n_items)
    nwork = item_end[-1]
    nw = 2 * rows // tme + N_EXPERTS - 1
    w = jnp.arange(nw, dtype=jnp.int32)
    wc = jnp.minimum(w, nwork - 1)
    exp_of = jnp.sum((wc[:, None] >= item_end[None, :]).astype(jnp.int32), axis=1)
    tile_of = (first_tile[exp_of] + (wc - (item_end[exp_of] - n_items[exp_of]))).astype(jnp.int32)
    return tile_of, exp_of, offs, nwork.reshape(1).astype(jnp.int32)


def _prep_w_in(w):
    o = np.cumsum([0, Q_RANK, KV_RANK, ROPE_DIM, D_INNER, CONV_DIM, SSM_HEADS, D_MODEL, D_MODEL])
    cq, ckv, kr, z, xbc, dt, ga, gs = (w[:, o[i]:o[i + 1]] for i in range(8))
    hr = ROPE_DIM // 2
    zeros = lambda n: jnp.zeros((w.shape[0], n), w.dtype)
    kra = jnp.concatenate([zeros(NOPE_DIM), kr, zeros(HEAD_SLOT - NOPE_DIM - ROPE_DIM)], axis=1)
    krb = jnp.concatenate([zeros(NOPE_DIM), -kr[:, hr:], kr[:, :hr], dt,
                           zeros(HEAD_SLOT - DT_LANE0 - SSM_HEADS)], axis=1)
    return jnp.concatenate([cq, ckv, kra, krb, z, xbc, ga, gs], axis=1).astype(BF16)


def _prep_w_uq(w):
    w = w.reshape(Q_RANK, MLA_HEADS, NOPE_DIM + ROPE_DIM)
    hr = ROPE_DIM // 2
    nope, r1, r2 = w[..., :NOPE_DIM], w[..., NOPE_DIM:NOPE_DIM + hr], w[..., NOPE_DIM + hr:]
    z = jnp.zeros((Q_RANK, MLA_HEADS, HEAD_SLOT - NOPE_DIM - ROPE_DIM), w.dtype)
    wa = jnp.concatenate([nope, r1, r2, z], axis=-1).reshape(Q_RANK, -1)
    wb = jnp.concatenate([jnp.zeros_like(nope), -r2, r1, z], axis=-1).reshape(Q_RANK, -1)
    return jnp.concatenate([wa, wb], axis=1).T.astype(BF16)


def _prep_w_ukv(w):
    w = w.reshape(KV_RANK, MLA_HEADS, NOPE_DIM + V_DIM)
    kn, v = w[..., :NOPE_DIM], w[..., NOPE_DIM:]
    wk = jnp.concatenate([kn, jnp.zeros((KV_RANK, MLA_HEADS, HEAD_SLOT - NOPE_DIM), w.dtype)], axis=-1)
    vslot = jnp.concatenate([v, jnp.zeros((KV_RANK, MLA_HEADS, V_SLOT - V_DIM), w.dtype)], axis=-1)
    return wk.reshape(KV_RANK, -1).astype(BF16), vslot.reshape(KV_RANK, -1).T.astype(BF16)


def _rope_tables(pos):
    inv = ROPE_THETA ** (-jnp.arange(0, ROPE_DIM, 2, dtype=F32) / ROPE_DIM)
    ang = pos.astype(F32)[:, None] * inv[None, :]
    cos, sin = jnp.cos(ang), jnp.sin(ang)
    n = pos.shape[0]
    pad = jnp.zeros((n, HEAD_SLOT - NOPE_DIM - ROPE_DIM), F32)
    scale = (NOPE_DIM + ROPE_DIM) ** -0.5 * float(np.log2(np.e))
    cq = jnp.concatenate([jnp.ones((n, NOPE_DIM), F32), cos, cos, pad], axis=1) * scale
    sq = jnp.concatenate([jnp.zeros((n, NOPE_DIM), F32), sin, sin, pad], axis=1) * scale
    ck = jnp.concatenate([jnp.zeros((n, NOPE_DIM), F32), cos, cos, pad], axis=1)
    sk = jnp.concatenate([jnp.zeros((n, NOPE_DIM), F32), sin, sin, pad], axis=1)
    return cq.T, sq.T, ck, sk


def _head_lanes(v):
    return jnp.zeros((1, LANES), F32).at[0, DT_LANE0:DT_LANE0 + SSM_HEADS].set(v.astype(F32))


def _pick(n, prefs):
    for t in prefs:
        if n % t == 0:
            return t
    raise ValueError(f"no tile for {n}")


def kernel(x, meta_tokens, norm_mix, w_in, mla_q_norm, mla_w_uq, mla_kv_norm, mla_w_ukv, ssm_conv_w, ssm_conv_b, ssm_dt_bias, ssm_a_log, ssm_d_skip, ssm_norm, w_branch_attn, w_branch_ssm, w_out, norm_ffn, moe_w_group, moe_b_group, moe_w_expert, moe_b_expert, moe_w_gate, moe_w_up, moe_w_down, norm_final):
    b, seq, _ = x.shape
    assert w_in.shape[0] == 1, "one layer"
    rows = b * seq
    x2d = x.reshape(rows, D_MODEL)
    tm = _pick(seq, (512, 256, 128))
    tq = _pick(seq, (512, 256, 128))
    tk = _pick(tq, (256, 128))
    chunk = 128
    ssd_blk = _pick(seq, (256, 128))

    w_in_r = _prep_w_in(w_in[0])
    wq = _prep_w_uq(mla_w_uq[0])
    wk, wvt = _prep_w_ukv(mla_w_ukv[0])
    g_mix = norm_mix[0].reshape(1, D_MODEL)
    qn = mla_q_norm[0].reshape(1, Q_RANK)
    kvn = mla_kv_norm[0].reshape(1, KV_RANK)
    cw = ssm_conv_w[0]
    cb = ssm_conv_b[0].reshape(1, CONV_DIM)
    dtb = _head_lanes(ssm_dt_bias[0])
    aneg = _head_lanes(-jnp.exp(ssm_a_log[0].astype(F32)))
    dsk = jnp.repeat(ssm_d_skip[0].astype(F32), SSM_HEAD_DIM).reshape(1, D_INNER)
    ng = ssm_norm[0].reshape(1, D_INNER)
    wa = w_branch_attn[0].astype(BF16)
    ws = w_branch_ssm[0].astype(BF16)
    wo = w_out[0].astype(BF16)
    nffn = norm_ffn[0].reshape(1, D_MODEL)
    wr = jnp.zeros((D_MODEL, LANES), F32)
    wr = wr.at[:, :N_EXPERT_GROUPS].set(moe_w_group[0]).at[:, EXPERT_LANE0:EXPERT_LANE0 + N_EXPERTS].set(moe_w_expert[0])
    wr = wr.astype(BF16)
    br = jnp.zeros((1, LANES), F32)
    br = br.at[0, :N_EXPERT_GROUPS].set(moe_b_group[0]).at[0, EXPERT_LANE0:EXPERT_LANE0 + N_EXPERTS].set(moe_b_expert[0])
    wg, wu, wd = moe_w_gate[0], moe_w_up[0], moe_w_down[0]
    nfin = norm_final.reshape(1, D_MODEL)

    meta_blk = jnp.concatenate([jnp.zeros((META_PAD, D_MODEL), F32), meta_tokens.astype(F32)], axis=0)
    m_cq, m_ckv, m_small, m_sz, m_xbc, _, _ = _inproj(meta_blk, g_mix, w_in_r, META_BLOCK)
    meta_pos = jnp.maximum(jnp.arange(META_BLOCK) - META_PAD, 0)
    _, kmeta, vmeta_t = _mla_prep(m_cq, m_ckv, m_small, qn, kvn, wq, wk, wvt, _rope_tables(meta_pos),
                                  META_BLOCK, 1, META_BLOCK, META_BLOCK)
    zero_state = jnp.zeros((LANES, D_INNER), F32)
    zero_halo = jnp.zeros((HALO, CONV_DIM), BF16)
    _, meta_state = _ssd(m_xbc[None], m_small[None], m_sz[None], zero_halo, zero_state,
                         cw, cb, dtb, aneg, dsk, ng, META_BLOCK, META_BLOCK, META_PAD)
    meta_halo = m_xbc[META_BLOCK - HALO:]

    cq, ckv, small, sz, xbc, ga, gs = _inproj(x2d, g_mix, w_in_r, tm)
    qt, k, vt = _mla_prep(cq, ckv, small, qn, kvn, wq, wk, wvt, _rope_tables(N_META + jnp.arange(seq)),
                          tm, seq // tm, tq, tk)
    o_attn = _flash(qt.reshape(b, seq // tq, -1, tq), k.reshape(b, seq, -1), vt.reshape(b, seq // tk, -1, tk),
                    kmeta, vmeta_t[0])
    o_ssm, _ = _ssd(xbc.reshape(b, seq, -1), small.reshape(b, seq, -1), sz.reshape(b, seq, -1), meta_halo,
                    meta_state[0], cw, cb, dtb, aneg, dsk, ng, ssd_blk, chunk, 0)
    h2, v, rt, counts = _merge(x2d, o_attn.reshape(rows, -1), o_ssm.reshape(rows, -1), ga, gs, wa, ws, wo,
                               nffn, wr, br, tm)
    tme = _pick(2 * rows, (256, 128))
    tile_of, exp_of, offs, nwork = _moe_plan(counts[0, EXPERT_LANE0:EXPERT_LANE0 + N_EXPERTS], rows, tme)
    pos = _slots(rt, counts, tm)
    xs = _dispatch(pos, v, tm)
    ys = _experts(tile_of, exp_of, offs, nwork, xs, wg, wu, wd, tme)
    out = _combine(pos, rt, h2, nfin, ys, tm)
    return out.reshape(b, seq, D_MODEL)
```

```python
import functools

import numpy as np
import jax
import jax.numpy as jnp
from jax import lax
from jax.experimental import pallas as pl
from jax.experimental.pallas import tpu as pltpu

F32 = jnp.float32
BF16 = jnp.bfloat16

D_MODEL = 1024
N_META = 16
EPS = 1e-6
NEG = -1e30
MLA_HEADS = 16
Q_RANK = 256
KV_RANK = 128
NOPE_DIM = 64
ROPE_DIM = 32
V_DIM = 64
ROPE_THETA = 10000.0
SSM_HEADS = 16
SSM_HEAD_DIM = 64
D_INNER = SSM_HEADS * SSM_HEAD_DIM
SSM_GROUPS = 2
D_STATE = 64
CONV_WIDTH = 4
CONV_DIM = D_INNER + 2 * SSM_GROUPS * D_STATE
N_EXPERT_GROUPS = 4
EXPERTS_PER_GROUP = 8
N_EXPERTS = N_EXPERT_GROUPS * EXPERTS_PER_GROUP
D_EXPERT = 256

LANES = 128
META_BLOCK = 128
META_PAD = META_BLOCK - N_META
HEAD_SLOT = 128
V_SLOT = 80
FLASH_HEADS = 4
EXPERT_LANE0 = 32
DT_LANE0 = 96
VMEM_LIMIT = 48 * 1024 * 1024

A_CQ, A_CKV, A_SMALL, A_Z, A_XBC, A_GA, A_GS = 256, 128, 256, D_INNER, CONV_DIM, D_MODEL, D_MODEL
A_OFFS = np.cumsum([0, A_CQ, A_CKV, A_SMALL, A_Z, A_XBC, A_GA, A_GS])


def _cparams(sem):
    return pltpu.CompilerParams(dimension_semantics=sem, vmem_limit_bytes=VMEM_LIMIT)


def _rms(x, g):
    return x * lax.rsqrt(jnp.mean(x * x, axis=-1, keepdims=True) + EPS) * g


def _sigmoid(x):
    return 1.0 / (1.0 + jnp.exp(-x))


def _dot(a, b):
    return jnp.dot(a, b, preferred_element_type=F32)


def _dot_nt(a, b):
    return lax.dot_general(a, b, (((1,), (1,)), ((), ())), preferred_element_type=F32)


def _dot_tn(a, b):
    return lax.dot_general(a, b, (((0,), (0,)), ((), ())), preferred_element_type=F32)


def _split3(x):
    hi = x.astype(BF16)
    r1 = x - hi.astype(F32)
    mid = r1.astype(BF16)
    lo = (r1 - mid.astype(F32)).astype(BF16)
    return hi, mid, lo


def _dot_exact_rhs(x, m):
    hi, mid, lo = _split3(x)
    return _dot(hi, m) + _dot(mid, m) + _dot(lo, m)


def _dot_exact_lhs(m, x):
    hi, mid, lo = _split3(x)
    return _dot(m, hi) + _dot(m, mid) + _dot(m, lo)


def _inproj_body(x_ref, g_ref, w_ref, cq_ref, ckv_ref, small_ref, sz_ref, xbc_ref, ga_ref, gs_ref):
    u = _rms(x_ref[...], g_ref[...]).astype(BF16)

    def mm(i):
        return _dot(u, w_ref[:, A_OFFS[i]:A_OFFS[i + 1]])

    cq_ref[...] = mm(0).astype(BF16)
    ckv_ref[...] = mm(1).astype(BF16)
    small_ref[...] = mm(2)
    z = mm(3)
    sz_ref[...] = (z * _sigmoid(z)).astype(BF16)
    xbc_ref[...] = mm(4).astype(BF16)
    ga_ref[...] = _sigmoid(mm(5)).astype(BF16)
    gs_ref[...] = _sigmoid(mm(6)).astype(BF16)


def _inproj(x2d, g, w, tm):
    r = x2d.shape[0]
    widths = (A_CQ, A_CKV, A_SMALL, A_Z, A_XBC, A_GA, A_GS)
    dtypes = (BF16, BF16, F32, BF16, BF16, BF16, BF16)
    return pl.pallas_call(
        _inproj_body,
        grid=(r // tm,),
        in_specs=[
            pl.BlockSpec((tm, D_MODEL), lambda i: (i, 0)),
            pl.BlockSpec((1, D_MODEL), lambda i: (0, 0)),
            pl.BlockSpec(w.shape, lambda i: (0, 0), pipeline_mode=pl.Buffered(1)),
        ],
        out_specs=[pl.BlockSpec((tm, n), lambda i: (i, 0)) for n in widths],
        out_shape=[jax.ShapeDtypeStruct((r, n), dt) for n, dt in zip(widths, dtypes)],
        compiler_params=_cparams(("parallel",)),
        name="inproj",
    )(x2d, g, w)


def _mla_prep_body(cq_ref, ckv_ref, small_ref, qn_ref, kvn_ref, wqt_ref, wk_ref, wvt_ref, vone_ref,
                   cqt_ref, sqt_ref, ckt_ref, skt_ref, qt_ref, k_ref, vt_ref):
    cn = _rms(cq_ref[...].astype(F32), qn_ref[...]).astype(BF16)
    kn = _rms(ckv_ref[...].astype(F32), kvn_ref[...]).astype(BF16)
    cqt = cqt_ref[...]
    sqt = sqt_ref[...]
    all_heads = MLA_HEADS * HEAD_SLOT
    part = all_heads // 2
    tqb = qt_ref.shape[-1]
    for lo in range(0, all_heads, part):
        qa = _dot_nt(wqt_ref[lo:lo + part, :], cn)
        qb = _dot_nt(wqt_ref[all_heads + lo:all_heads + lo + part, :], cn)
        for r0 in range(0, part, HEAD_SLOT):
            qh = (qa[r0:r0 + HEAD_SLOT] * cqt + qb[r0:r0 + HEAD_SLOT] * sqt).astype(BF16)
            for j in range(qt_ref.shape[0]):
                qt_ref[j, lo + r0:lo + r0 + HEAD_SLOT, :] = qh[:, j * tqb:(j + 1) * tqb]
    krp = small_ref[:, 0:LANES] * ckt_ref[...] + small_ref[:, LANES:2 * LANES] * skt_ref[...]
    krp2 = jnp.concatenate([krp, krp], axis=1)
    for hp in range(MLA_HEADS // 2):
        lo, hi = hp * 2 * HEAD_SLOT, (hp + 1) * 2 * HEAD_SLOT
        k_ref[:, lo:hi] = (_dot(kn, wk_ref[:, lo:hi]) + krp2).astype(BF16)
    vt = _dot_nt(wvt_ref[...], kn) + vone_ref[...]
    tkv = vt_ref.shape[-1]
    for j in range(vt_ref.shape[0]):
        vt_ref[j] = vt[:, j * tkv:(j + 1) * tkv].astype(BF16)


def _mla_prep(cq, ckv, small, qn, kvn, wqt, wk, wvt, tabs, tm, seq_blocks, tqb, tkv):
    r = cq.shape[0]
    nq = MLA_HEADS * HEAD_SLOT
    nv = MLA_HEADS * V_SLOT
    cqt, sqt, ckt, skt = tabs
    ones_row = (jnp.arange(nv) % V_SLOT == V_DIM).astype(F32)
    vone = jnp.broadcast_to(ones_row[:, None], (nv, tm))
    row = lambda n: pl.BlockSpec((tm, n), lambda i: (i, 0))
    tab = lambda n: pl.BlockSpec((tm, n), lambda i: (i % seq_blocks, 0))
    tab_t = pl.BlockSpec((HEAD_SLOT, tm), lambda i: (0, i % seq_blocks))
    full = lambda a: pl.BlockSpec(a.shape, lambda i: (0, 0))
    return pl.pallas_call(
        _mla_prep_body,
        grid=(r // tm,),
        in_specs=[row(Q_RANK), row(KV_RANK), row(A_SMALL), full(qn), full(kvn), full(wqt), full(wk), full(wvt),
                  full(vone), tab_t, tab_t, tab(HEAD_SLOT), tab(HEAD_SLOT)],
        out_specs=[pl.BlockSpec((tm // tqb, nq, tqb), lambda i: (i, 0, 0)), row(nq),
                   pl.BlockSpec((tm // tkv, nv, tkv), lambda i: (i, 0, 0))],
        out_shape=[jax.ShapeDtypeStruct((r // tqb, nq, tqb), BF16),
                   jax.ShapeDtypeStruct((r, nq), BF16),
                   jax.ShapeDtypeStruct((r // tkv, nv, tkv), BF16)],
        compiler_params=_cparams(("parallel",)),
        name="mla_prep",
    )(cq, ckv, small, qn, kvn, wqt, wk, wvt, vone, cqt, sqt, ckt, skt)


def _flash_body(qt_ref, k_ref, vt_ref, km_ref, vmt_ref, o_ref, *scratch, tq, tk, seq):
    nh = FLASH_HEADS
    s_scr = (scratch[0:nh], scratch[nh:2 * nh])
    p_scr = (scratch[2 * nh:3 * nh], scratch[3 * nh:4 * nh])
    acc_scr = scratch[4 * nh:5 * nh]
    key_i = lax.broadcasted_iota(jnp.int32, (tk, tq), 0)
    qry_i = lax.broadcasted_iota(jnp.int32, (tk, tq), 1)
    meta_row = lax.broadcasted_iota(jnp.int32, (META_BLOCK, 1), 0)
    heads = [slice(hh * HEAD_SLOT, (hh + 1) * HEAD_SLOT) for hh in range(nh)]
    vals = [slice(hh * V_SLOT, (hh + 1) * V_SLOT) for hh in range(nh)]
    ndiag = tq // tk
    assert ndiag % 2 == 0, "slot parity is static only when a query block spans an even number of key blocks"

    def probs(st, m):
        return jnp.exp2((st - m).astype(BF16))

    def qblock(qi, _):
        q0 = pl.multiple_of(qi * tq, tq)
        qs = [qt_ref[qi, hs, :] for hs in heads]
        nfull = qi * ndiag
        last = nfull + ndiag - 1

        def scores_into(slot, kj):
            k0 = pl.multiple_of(kj * tk, tk)
            for hh in range(nh):
                s_scr[slot][hh][...] = _dot(k_ref[pl.ds(k0, tk), heads[hh]], qs[hh])

        ms = []
        for hh in range(nh):
            st = jnp.where(meta_row >= META_PAD, _dot(km_ref[:, heads[hh]], qs[hh]), NEG)
            m = jnp.max(st, axis=0, keepdims=True)
            ms.append(m)
            acc_scr[hh][...] = _dot(vmt_ref[vals[hh], :], probs(st, m))
            p_scr[1][hh][...] = jnp.zeros((tk, tq), BF16)
        scores_into(0, 0)

        def kvstep(kj, slot, carry, masked):
            ms, alphas = carry
            kprev = jnp.maximum(kj - 1, 0)
            pvs = [_dot(vt_ref[kprev, vals[hh], :], p_scr[1 - slot][hh][...]) for hh in range(nh)]
            scores_into(1 - slot, jnp.minimum(kj + 1, last))
            out = ([], [])
            for hh in range(nh):
                st = s_scr[slot][hh][...]
                if masked:
                    st = jnp.where(kj * tk + key_i <= q0 + qry_i, st, NEG)
                m_new = jnp.maximum(ms[hh], jnp.max(st, axis=0, keepdims=True))
                p_scr[slot][hh][...] = probs(st, m_new)
                acc_scr[hh][...] = alphas[hh] * acc_scr[hh][...] + pvs[hh]
                out[0].append(m_new)
                out[1].append(jnp.exp2(ms[hh] - m_new))
            return tuple(tuple(x) for x in out)

        def two_steps(t, carry):
            carry = kvstep(2 * t, 0, carry, False)
            return kvstep(2 * t + 1, 1, carry, False)

        ones = jnp.ones((1, tq), F32)
        carry = lax.fori_loop(0, nfull // 2, two_steps, (tuple(ms), (ones,) * nh))
        for d in range(ndiag):
            carry = kvstep(nfull + d, d % 2, carry, True)
        _, alphas = carry
        outs = []
        for hh in range(nh):
            acc = alphas[hh] * acc_scr[hh][...] + _dot(vt_ref[last, vals[hh], :], p_scr[(ndiag - 1) % 2][hh][...])
            outs.append(acc[:V_DIM] / acc[V_DIM:V_DIM + 1])
        o_ref[pl.ds(q0, tq), :] = jnp.concatenate(outs, axis=0).T.astype(BF16)
        return 0

    lax.fori_loop(0, seq // tq, qblock, 0)


def _flash(qt, k, vt, kmeta, vmeta_t):
    b, seq, _ = k.shape
    tq = qt.shape[-1]
    tk = vt.shape[-1]
    nh = FLASH_HEADS
    return pl.pallas_call(
        functools.partial(_flash_body, tq=tq, tk=tk, seq=seq),
        grid=(b, MLA_HEADS // nh),
        in_specs=[
            pl.BlockSpec((None, seq // tq, nh * HEAD_SLOT, tq), lambda i, p: (i, 0, p, 0)),
            pl.BlockSpec((None, seq, nh * HEAD_SLOT), lambda i, p: (i, 0, p)),
            pl.BlockSpec((None, seq // tk, nh * V_SLOT, tk), lambda i, p: (i, 0, p, 0)),
            pl.BlockSpec((META_BLOCK, nh * HEAD_SLOT), lambda i, p: (0, p)),
            pl.BlockSpec((nh * V_SLOT, META_BLOCK), lambda i, p: (p, 0)),
        ],
        out_specs=pl.BlockSpec((None, seq, nh * V_DIM), lambda i, p: (i, 0, p)),
        out_shape=jax.ShapeDtypeStruct((b, seq, MLA_HEADS * V_DIM), BF16),
        scratch_shapes=([pltpu.VMEM((tk, tq), F32)] * (2 * nh) + [pltpu.VMEM((tk, tq), BF16)] * (2 * nh)
                        + [pltpu.VMEM((V_SLOT, tq), F32)] * nh),
        compiler_params=_cparams(("parallel", "parallel")),
        name="flash",
    )(qt, k, vt, kmeta, vmeta_t)


HALO = 16


def _expand2(x, expand):
    hi = x.astype(BF16)
    mid = (x - hi.astype(F32)).astype(BF16)
    return _dot(hi, expand) + _dot(mid, expand)


def _ssd_body(xbc_ref, halo_ref, mh_ref, small_ref, sz_ref, cw_ref, cb_ref, dtb_ref, aneg_ref, dsk_ref,
              ng_ref, init_ref, o_ref, fin_ref, st_ref, y_ref, *, q, n_pad):
    c = pl.program_id(1)
    nc = pl.num_programs(1)

    @pl.when(c == 0)
    def _():
        st_ref[...] = init_ref[...]

    state = st_ref[...]
    halo = jnp.where(c == 0, mh_ref[...], halo_ref[...])
    for j in range(xbc_ref.shape[0] // q):
        rows = slice(j * q, (j + 1) * q)
        x_b = xbc_ref[rows, :]
        state = _ssd_chunk(x_b, halo, small_ref[rows, LANES:2 * LANES], sz_ref[rows, :], state,
                           cw_ref, cb_ref, dtb_ref, aneg_ref, dsk_ref, ng_ref, o_ref.at[rows, :], y_ref.at[rows, :],
                           q=q, n_pad=n_pad - j * q)
        halo = x_b[q - HALO:, :]
    st_ref[...] = state

    @pl.when(c == nc - 1)
    def _():
        fin_ref[...] = state


def _ssd_chunk(x_b, halo, dt_raw, sz, prev, cw_ref, cb_ref, dtb_ref, aneg_ref, dsk_ref, ng_ref, o_ref, y_ref,
               *, q, n_pad):
    x_ext = jnp.concatenate([halo, x_b], axis=0)
    sh_r = lax.broadcasted_iota(jnp.int32, (q, HALO + q), 0)
    sh_c = lax.broadcasted_iota(jnp.int32, (q, HALO + q), 1)
    xc = cb_ref[...] + cw_ref[CONV_WIDTH - 1:CONV_WIDTH, :] * x_b.astype(F32)
    for d in range(1, CONV_WIDTH):
        shifted = _dot((sh_c == sh_r + (HALO - d)).astype(BF16), x_ext)
        xc = xc + cw_ref[CONV_WIDTH - 1 - d:CONV_WIDTH - d, :] * shifted
    xc = xc * _sigmoid(xc)

    lane = lax.broadcasted_iota(jnp.int32, (q, LANES), 1)
    dt = dt_raw + dtb_ref[...]
    dt = jnp.maximum(dt, 0.0) + jnp.log(1.0 + jnp.exp(-jnp.abs(dt)))
    dt = jnp.where((lane >= DT_LANE0) & (lane < DT_LANE0 + SSM_HEADS), dt, 0.0)
    if n_pad > 0:
        rowv = lax.broadcasted_iota(jnp.int32, (q, 1), 0) >= n_pad
        xc = jnp.where(rowv, xc, 0.0)
        dt = jnp.where(rowv, dt, 0.0)

    xs = xc[:, :D_INNER]
    bm = xc[:, D_INNER:D_INNER + LANES]
    cm = xc[:, D_INNER + LANES:]

    a = dt * aneg_ref[...]
    ri = lax.broadcasted_iota(jnp.int32, (q, q), 0)
    ci = lax.broadcasted_iota(jnp.int32, (q, q), 1)
    tril = ri >= ci
    a_cs = _dot_exact_lhs(tril.astype(BF16), a)
    a_cs_t = a_cs.T

    er = lax.broadcasted_iota(jnp.int32, (LANES, D_INNER), 0)
    ec = lax.broadcasted_iota(jnp.int32, (LANES, D_INNER), 1)
    expand = (er - DT_LANE0 == ec // SSM_HEAD_DIM).astype(BF16)
    dt_x = _expand2(dt, expand)
    grow_x = _expand2(jnp.exp(a_cs), expand)
    rest_x = _expand2(jnp.exp(a_cs[q - 1:q, :] - a_cs), expand)

    xdt = xs * dt_x
    xdt_b = xdt.astype(BF16)
    xd_b = (xdt * rest_x).astype(BF16)

    bm_b = bm.astype(BF16)
    cm_b = cm.astype(BF16)
    lane_q = lax.broadcasted_iota(jnp.int32, (q, LANES), 1)

    for g in range(SSM_GROUPS):
        in_g = (lane_q >= g * D_STATE) & (lane_q < (g + 1) * D_STATE)
        cb = _dot_nt(jnp.where(in_g, cm, 0.0).astype(BF16), bm_b)
        for pr in range(SSM_HEADS // SSM_GROUPS // 2):
            pair = g * (SSM_HEADS // SSM_GROUPS // 2) + pr
            rhs = xdt_b[:, pair * LANES:(pair + 1) * LANES]
            ys = []
            for hh in range(2):
                e = DT_LANE0 + 2 * pair + hh
                seg = a_cs[:, e:e + 1] - a_cs_t[e:e + 1, :]
                lmat = jnp.exp(jnp.where(tril, seg, NEG))
                ys.append(_dot((cb * lmat).astype(BF16), rhs))
            y_ref[:, pair * LANES:(pair + 1) * LANES] = jnp.where(lane_q < SSM_HEAD_DIM, ys[0], ys[1])

    sr = lax.broadcasted_iota(jnp.int32, (LANES, D_INNER), 0)
    sc = lax.broadcasted_iota(jnp.int32, (LANES, D_INNER), 1)
    same_group = (sr // D_STATE) == (sc // (D_INNER // SSM_GROUPS))
    y_off = _dot(cm_b, prev.astype(BF16)) * grow_x
    st_new = prev * grow_x[q - 1:q, :] + jnp.where(same_group, _dot_tn(bm_b, xd_b), 0.0)

    y = y_ref[...] + y_off + xs * dsk_ref[...]
    yz = y * sz.astype(F32)
    half = D_INNER // SSM_GROUPS
    outs = []
    for g in range(SSM_GROUPS):
        part = yz[:, g * half:(g + 1) * half]
        outs.append(part * lax.rsqrt(jnp.mean(part * part, axis=-1, keepdims=True) + EPS))
    o_ref[...] = (jnp.concatenate(outs, axis=1) * ng_ref[...]).astype(BF16)
    return st_new


def _ssd(xbc, small, sz, meta_halo, init_state, cw, cb, dtb, aneg, dsk, ng, blk, q, n_pad):
    b, seq, _ = xbc.shape
    nc = seq // blk
    hb = blk // HALO
    full = lambda a: pl.BlockSpec(a.shape, lambda i, c: (0,) * a.ndim)
    return pl.pallas_call(
        functools.partial(_ssd_body, q=q, n_pad=n_pad),
        grid=(b, nc),
        in_specs=[
            pl.BlockSpec((None, blk, CONV_DIM), lambda i, c: (i, c, 0)),
            pl.BlockSpec((None, HALO, CONV_DIM), lambda i, c: (i, jnp.maximum(c * hb - 1, 0), 0)),
            full(meta_halo),
            pl.BlockSpec((None, blk, A_SMALL), lambda i, c: (i, c, 0)),
            pl.BlockSpec((None, blk, D_INNER), lambda i, c: (i, c, 0)),
            full(cw), full(cb), full(dtb), full(aneg), full(dsk), full(ng), full(init_state),
        ],
        out_specs=[
            pl.BlockSpec((None, blk, D_INNER), lambda i, c: (i, c, 0)),
            pl.BlockSpec((None, LANES, D_INNER), lambda i, c: (i, 0, 0)),
        ],
        out_shape=[jax.ShapeDtypeStruct((b, seq, D_INNER), BF16),
                   jax.ShapeDtypeStruct((b, LANES, D_INNER), F32)],
        scratch_shapes=[pltpu.VMEM((LANES, D_INNER), F32),
                        pltpu.VMEM((blk, D_INNER), F32)],
        compiler_params=_cparams(("parallel", "arbitrary")),
        name="ssd",
    )(xbc, xbc, meta_halo, small, sz, cw, cb, dtb, aneg, dsk, ng, init_state)


def _route(logits):
    lane = lax.broadcasted_iota(jnp.int32, logits.shape, 1).astype(F32)
    gl = jnp.where(lane < N_EXPERT_GROUPS, logits, NEG)
    gmax = jnp.max(gl, axis=1, keepdims=True)
    gidx = jnp.min(jnp.where(gl == gmax, lane, float(LANES)), axis=1, keepdims=True)
    p_g = 1.0 / jnp.sum(jnp.exp(gl - gmax), axis=1, keepdims=True)
    lo = EXPERT_LANE0 + EXPERTS_PER_GROUP * gidx
    el = jnp.where((lane >= lo) & (lane < lo + EXPERTS_PER_GROUP), logits, NEG)
    t1 = jnp.max(el, axis=1, keepdims=True)
    i1 = jnp.min(jnp.where(el == t1, lane, float(LANES)), axis=1, keepdims=True)
    el2 = jnp.where(lane == i1, NEG, el)
    t2 = jnp.max(el2, axis=1, keepdims=True)
    i2 = jnp.min(jnp.where(el2 == t2, lane, float(LANES)), axis=1, keepdims=True)
    e21 = jnp.exp(t2 - t1)
    w1 = p_g / (1.0 + e21)
    w2 = w1 * e21
    return lane, i1, i2, w1, w2


RT_E1, RT_E2, RT_R1, RT_R2, RT_W1, RT_W2 = range(6)


def _merge_body(x_ref, oa_ref, os_ref, ga_ref, gs_ref, wa_ref, ws_ref, wo_ref, nf_ref, wr_ref, br_ref,
                h2_ref, v_ref, rt_ref, cnt_out_ref, cnt_ref):
    @pl.when(pl.program_id(0) == 0)
    def _():
        cnt_ref[...] = jnp.zeros_like(cnt_ref)

    merged = (ga_ref[...].astype(F32) * _dot(oa_ref[...], wa_ref[...])
              + gs_ref[...].astype(F32) * _dot(os_ref[...], ws_ref[...]))
    h2 = x_ref[...] + _dot(merged.astype(BF16), wo_ref[...])
    h2_ref[...] = h2
    v = _rms(h2, nf_ref[...])
    _store_token_slabs(v_ref, v)
    lane, i1, i2, w1, w2 = _route(_dot(v.astype(BF16), wr_ref[...]) + br_ref[...])

    tm = lane.shape[0]
    chosen = jnp.where((lane == i1) | (lane == i2), 1.0, 0.0)
    ri = lax.broadcasted_iota(jnp.int32, (tm, tm), 0)
    ci = lax.broadcasted_iota(jnp.int32, (tm, tm), 1)
    earlier = _dot((ri > ci).astype(BF16), chosen.astype(BF16)) + cnt_ref[...]
    r1 = jnp.sum(jnp.where(lane == i1, earlier, 0.0), axis=1, keepdims=True)
    r2 = jnp.sum(jnp.where(lane == i2, earlier, 0.0), axis=1, keepdims=True)
    cnt_ref[...] += jnp.sum(chosen, axis=0, keepdims=True)
    cnt_out_ref[...] = jnp.broadcast_to(cnt_ref[...], cnt_out_ref.shape)

    rec = jnp.zeros_like(lane)
    for col, val in ((RT_E1, i1 - EXPERT_LANE0), (RT_E2, i2 - EXPERT_LANE0), (RT_R1, r1), (RT_R2, r2),
                     (RT_W1, w1), (RT_W2, w2)):
        rec = jnp.where(lane == col, val, rec)
    rt_ref[...] = rec


def _merge(x2d, oa, osm, ga, gs, wa, ws, wo, nf, wr, br, tm):
    r = x2d.shape[0]
    row = lambda n: pl.BlockSpec((tm, n), lambda i: (i, 0))
    full = lambda a: pl.BlockSpec(a.shape, lambda i: (0, 0))
    return pl.pallas_call(
        _merge_body,
        grid=(r // tm,),
        in_specs=[row(D_MODEL)] * 5 + [full(wa), full(ws), full(wo), full(nf), full(wr), full(br)],
        out_specs=[row(D_MODEL), pl.BlockSpec((tm * SLAB, LANES), lambda i: (i, 0)), row(LANES),
                   pl.BlockSpec((8, LANES), lambda i: (0, 0))],
        out_shape=[jax.ShapeDtypeStruct((r, D_MODEL), F32),
                   jax.ShapeDtypeStruct((r * SLAB, LANES), F32),
                   jax.ShapeDtypeStruct((r, LANES), F32),
                   jax.ShapeDtypeStruct((8, LANES), F32)],
        scratch_shapes=[pltpu.VMEM((1, LANES), F32)],
        compiler_params=_cparams(("arbitrary",)),
        name="merge",
    )(x2d, oa, osm, ga, gs, wa, ws, wo, nf, wr, br)


SLAB = D_MODEL // LANES


def _store_token_slabs(ref, x):
    tm = x.shape[0]
    for s in range(SLAB):
        ref[pl.ds(s, tm, stride=SLAB), :] = x[:, s * LANES:(s + 1) * LANES]


def _load_token_slabs(ref):
    tm = ref.shape[0] // SLAB
    return jnp.concatenate([ref[pl.ds(s, tm, stride=SLAB), :] for s in range(SLAB)], axis=1)


def _row_copy(src_ref, src_row, dst_ref, dst_row, sem):
    src = src_ref.at[pl.ds(pl.multiple_of(src_row * SLAB, SLAB), SLAB)]
    dst = dst_ref.at[pl.ds(pl.multiple_of(dst_row * SLAB, SLAB), SLAB)]
    return pltpu.make_async_copy(src, dst, sem)


def _dispatch_body(pos_ref, v_ref, xs_ref, sem):
    tm = v_ref.shape[0] // SLAB

    def issue(t, _):
        _row_copy(v_ref, t, xs_ref, pos_ref[0, t], sem).start(priority=0)
        _row_copy(v_ref, t, xs_ref, pos_ref[1, t], sem).start(priority=1)
        return 0

    lax.fori_loop(0, tm, issue, 0, unroll=8)
    for _ in range(2):
        pltpu.make_async_copy(v_ref, xs_ref.at[pl.ds(0, tm * SLAB)], sem).wait()


def _slot_spec(tm):
    return pl.BlockSpec((8, tm), lambda i: (0, i), memory_space=pltpu.SMEM)


def _dispatch(pos, v, tm):
    r = v.shape[0] // SLAB
    return pl.pallas_call(
        _dispatch_body,
        grid=(r // tm,),
        in_specs=[_slot_spec(tm), pl.BlockSpec((tm * SLAB, LANES), lambda i: (i, 0))],
        out_specs=pl.BlockSpec(memory_space=pl.ANY),
        out_shape=jax.ShapeDtypeStruct((2 * r * SLAB, LANES), F32),
        scratch_shapes=[pltpu.SemaphoreType.DMA],
        compiler_params=_cparams(("arbitrary",)),
        name="moe_dispatch",
    )(pos, v)


def _experts_body(tile_ref, exp_ref, off_ref, nwork_ref, xs_ref, wg_ref, wu_ref, wd_ref, ys_ref,
                  wg_s, wu_s, wd_s):
    w = pl.program_id(0)
    tme = xs_ref.shape[0] // SLAB

    @pl.when(w < nwork_ref[0])
    def _():
        e = exp_ref[w]
        tile = tile_ref[w]
        prev = jnp.maximum(w - 1, 0)
        new_expert = (w == 0) | (exp_ref[prev] != e)
        new_tile = (w == 0) | (tile_ref[prev] != tile)

        @pl.when(new_expert)
        def _():
            wg_s[...] = wg_ref[...].astype(BF16)
            wu_s[...] = wu_ref[...].astype(BF16)
            wd_s[...] = wd_ref[...].astype(BF16)

        x = _load_token_slabs(xs_ref).astype(BF16)
        g = _dot(x, wg_s[...])
        u = _dot(x, wu_s[...])
        y = _dot((g * _sigmoid(g) * u).astype(BF16), wd_s[...])
        rows = tile * tme + lax.broadcasted_iota(jnp.int32, (tme, 1), 0)
        mine = (rows >= off_ref[e]) & (rows < off_ref[e + 1])

        @pl.when(new_tile)
        def _():
            _store_token_slabs(ys_ref, jnp.where(mine, y, 0.0))

        @pl.when(jnp.logical_not(new_tile))
        def _():
            _store_token_slabs(ys_ref, jnp.where(mine, y, _load_token_slabs(ys_ref)))


def _experts(tile_of, exp_of, offs, nwork, xs, wg, wu, wd, tme):
    nw = tile_of.shape[0]
    grid_spec = pltpu.PrefetchScalarGridSpec(
        num_scalar_prefetch=4,
        grid=(nw,),
        in_specs=[
            pl.BlockSpec((tme * SLAB, LANES), lambda w, t, e, o, n: (t[w], 0)),
            pl.BlockSpec((None, D_MODEL, D_EXPERT), lambda w, t, e, o, n: (e[w], 0, 0)),
            pl.BlockSpec((None, D_MODEL, D_EXPERT), lambda w, t, e, o, n: (e[w], 0, 0)),
            pl.BlockSpec((None, D_EXPERT, D_MODEL), lambda w, t, e, o, n: (e[w], 0, 0)),
        ],
        out_specs=pl.BlockSpec((tme * SLAB, LANES), lambda w, t, e, o, n: (t[w], 0)),
        scratch_shapes=[pltpu.VMEM((D_MODEL, D_EXPERT), BF16), pltpu.VMEM((D_MODEL, D_EXPERT), BF16),
                        pltpu.VMEM((D_EXPERT, D_MODEL), BF16)],
    )
    return pl.pallas_call(
        _experts_body,
        grid_spec=grid_spec,
        out_shape=jax.ShapeDtypeStruct(xs.shape, F32),
        compiler_params=_cparams(("arbitrary",)),
        name="moe_experts",
    )(tile_of, exp_of, offs, nwork, xs, wg, wu, wd)


def _combine_body(pos_ref, pos_next_ref, rt_ref, h2_ref, nf_ref, ys_ref, out_ref, y1_ref, y2_ref, sem):
    i = pl.program_id(0)
    n = pl.num_programs(0)
    tm = h2_ref.shape[0]
    slot = i % 2

    def issue_tile(p_ref, s):
        def issue(t, _):
            _row_copy(ys_ref, p_ref[0, t], y1_ref.at[s], t, sem.at[s]).start(priority=0)
            _row_copy(ys_ref, p_ref[1, t], y2_ref.at[s], t, sem.at[s]).start(priority=1)
            return 0

        lax.fori_loop(0, tm, issue, 0, unroll=8)

    @pl.when(i == 0)
    def _():
        issue_tile(pos_ref, 0)

    @pl.when(i + 1 < n)
    def _():
        issue_tile(pos_next_ref, 1 - slot)

    for buf in (y1_ref, y2_ref):
        pltpu.make_async_copy(ys_ref.at[pl.ds(0, tm * SLAB)], buf.at[slot], sem.at[slot]).wait()
    lane = lax.broadcasted_iota(jnp.int32, rt_ref.shape, 1)
    rt = rt_ref[...]
    w1 = jnp.sum(jnp.where(lane == RT_W1, rt, 0.0), axis=1, keepdims=True)
    w2 = jnp.sum(jnp.where(lane == RT_W2, rt, 0.0), axis=1, keepdims=True)
    moe = w1 * _load_token_slabs(y1_ref.at[slot]) + w2 * _load_token_slabs(y2_ref.at[slot])
    out_ref[...] = _rms(h2_ref[...] + moe, nf_ref[...])


def _combine(pos, rt, h2, nf, ys, tm):
    r = h2.shape[0]
    last = r // tm - 1
    next_slots = pl.BlockSpec((8, tm), lambda i: (0, jnp.minimum(i + 1, last)), memory_space=pltpu.SMEM)
    return pl.pallas_call(
        _combine_body,
        grid=(r // tm,),
        in_specs=[_slot_spec(tm), next_slots, pl.BlockSpec((tm, LANES), lambda i: (i, 0)),
                  pl.BlockSpec((tm, D_MODEL), lambda i: (i, 0)), pl.BlockSpec((1, D_MODEL), lambda i: (0, 0)),
                  pl.BlockSpec(memory_space=pl.ANY)],
        out_specs=pl.BlockSpec((tm, D_MODEL), lambda i: (i, 0)),
        out_shape=jax.ShapeDtypeStruct((r, D_MODEL), F32),
        scratch_shapes=[pltpu.VMEM((2, tm * SLAB, LANES), F32), pltpu.VMEM((2, tm * SLAB, LANES), F32),
                        pltpu.SemaphoreType.DMA((2,))],
        compiler_params=_cparams(("arbitrary",)),
        name="moe_combine",
    )(pos, pos, rt, h2, nf, ys)


def _slots_body(rt_ref, cnt_ref, pos_ref):
    rt = rt_ref[...]
    lane = lax.broadcasted_iota(jnp.int32, rt.shape, 1).astype(F32)
    col = lambda c: jnp.sum(jnp.where(lane == c, rt, 0.0), axis=1, keepdims=True)
    ri = lax.broadcasted_iota(jnp.int32, (LANES, LANES), 0)
    ci = lax.broadcasted_iota(jnp.int32, (LANES, LANES), 1)
    first = _dot_exact_rhs(cnt_ref[...], (ri < ci).astype(BF16))[0:1, :]
    slot = lambda e, r: col(r) + jnp.sum(jnp.where(lane == col(e) + EXPERT_LANE0, first, 0.0),
                                         axis=1, keepdims=True)
    rec = jnp.where(lane == 0, slot(RT_E1, RT_R1), jnp.where(lane == 1, slot(RT_E2, RT_R2), 0.0))
    sr = lax.broadcasted_iota(jnp.int32, (8, LANES), 0)
    sc = lax.broadcasted_iota(jnp.int32, (8, LANES), 1)
    sel = (sr == sc).astype(BF16)
    pos_ref[...] = sum(_dot_nt(sel, part) for part in _split3(rec)).astype(jnp.int32)


def _slots(rt, counts, tm):
    r = rt.shape[0]
    return pl.pallas_call(
        _slots_body,
        grid=(r // tm,),
        in_specs=[pl.BlockSpec((tm, LANES), lambda i: (i, 0)), pl.BlockSpec((8, LANES), lambda i: (0, 0))],
        out_specs=pl.BlockSpec((8, tm), lambda i: (0, i)),
        out_shape=jax.ShapeDtypeStruct((8, r), jnp.int32),
        compiler_params=_cparams(("parallel",)),
        name="moe_slots",
    )(rt, counts)


def _moe_plan(counts, rows, tme):
    cnt = counts.astype(jnp.int32)
    ends = jnp.cumsum(cnt)
    offs = jnp.concatenate([jnp.zeros((1,), jnp.int32), ends])
    first_tile = offs[:-1] // tme
    last_tile = (ends - 1) // tme
    n_items = jnp.where(cnt > 0, last_tile - first_tile + 1, 0)
    item_end = jnp.cumsum(n_items)
    nwork = item_end[-1]
    nw = 2 * rows // tme + N_EXPERTS - 1
    w = jnp.arange(nw, dtype=jnp.int32)
    wc = jnp.minimum(w, nwork - 1)
    exp_of = jnp.sum((wc[:, None] >= item_end[None, :]).astype(jnp.int32), axis=1)
    onehot = (exp_of[:, None] == jnp.arange(N_EXPERTS, dtype=jnp.int32)[None, :]).astype(jnp.int32)
    item_start = jnp.sum(onehot * (item_end - n_items)[None, :], axis=1)
    tile_of = (jnp.sum(onehot * first_tile[None, :], axis=1) + (wc - item_start)).astype(jnp.int32)
    return tile_of, exp_of, offs, nwork.reshape(1).astype(jnp.int32)


def _prep_w_in(w):
    o = np.cumsum([0, Q_RANK, KV_RANK, ROPE_DIM, D_INNER, CONV_DIM, SSM_HEADS, D_MODEL, D_MODEL])
    cq, ckv, kr, z, xbc, dt, ga, gs = (w[:, o[i]:o[i + 1]] for i in range(8))
    hr = ROPE_DIM // 2
    zeros = lambda n: jnp.zeros((w.shape[0], n), w.dtype)
    kra = jnp.concatenate([zeros(NOPE_DIM), kr, zeros(HEAD_SLOT - NOPE_DIM - ROPE_DIM)], axis=1)
    krb = jnp.concatenate([zeros(NOPE_DIM), -kr[:, hr:], kr[:, :hr], dt,
                           zeros(HEAD_SLOT - DT_LANE0 - SSM_HEADS)], axis=1)
    return jnp.concatenate([cq, ckv, kra, krb, z, xbc, ga, gs], axis=1).astype(BF16)


def _prep_w_uq(w):
    w = w.reshape(Q_RANK, MLA_HEADS, NOPE_DIM + ROPE_DIM)
    hr = ROPE_DIM // 2
    nope, r1, r2 = w[..., :NOPE_DIM], w[..., NOPE_DIM:NOPE_DIM + hr], w[..., NOPE_DIM + hr:]
    z = jnp.zeros((Q_RANK, MLA_HEADS, HEAD_SLOT - NOPE_DIM - ROPE_DIM), w.dtype)
    wa = jnp.concatenate([nope, r1, r2, z], axis=-1).reshape(Q_RANK, -1)
    wb = jnp.concatenate([jnp.zeros_like(nope), -r2, r1, z], axis=-1).reshape(Q_RANK, -1)
    return jnp.concatenate([wa, wb], axis=1).T.astype(BF16)


def _prep_w_ukv(w):
    w = w.reshape(KV_RANK, MLA_HEADS, NOPE_DIM + V_DIM)
    kn, v = w[..., :NOPE_DIM], w[..., NOPE_DIM:]
    wk = jnp.concatenate([kn, jnp.zeros((KV_RANK, MLA_HEADS, HEAD_SLOT - NOPE_DIM), w.dtype)], axis=-1)
    vslot = jnp.concatenate([v, jnp.zeros((KV_RANK, MLA_HEADS, V_SLOT - V_DIM), w.dtype)], axis=-1)
    return wk.reshape(KV_RANK, -1).astype(BF16), vslot.reshape(KV_RANK, -1).T.astype(BF16)


def _rope_tables(pos):
    inv = ROPE_THETA ** (-jnp.arange(0, ROPE_DIM, 2, dtype=F32) / ROPE_DIM)
    ang = pos.astype(F32)[:, None] * inv[None, :]
    cos, sin = jnp.cos(ang), jnp.sin(ang)
    n = pos.shape[0]
    pad = jnp.zeros((n, HEAD_SLOT - NOPE_DIM - ROPE_DIM), F32)
    scale = (NOPE_DIM + ROPE_DIM) ** -0.5 * float(np.log2(np.e))
    cq = jnp.concatenate([jnp.ones((n, NOPE_DIM), F32), cos, cos, pad], axis=1) * scale
    sq = jnp.concatenate([jnp.zeros((n, NOPE_DIM), F32), sin, sin, pad], axis=1) * scale
    ck = jnp.concatenate([jnp.zeros((n, NOPE_DIM), F32), cos, cos, pad], axis=1)
    sk = jnp.concatenate([jnp.zeros((n, NOPE_DIM), F32), sin, sin, pad], axis=1)
    return cq.T, sq.T, ck, sk


def _head_lanes(v):
    return jnp.zeros((1, LANES), F32).at[0, DT_LANE0:DT_LANE0 + SSM_HEADS].set(v.astype(F32))


def _pick(n, prefs):
    for t in prefs:
        if n % t == 0:
            return t
    raise ValueError(f"no tile for {n}")


def kernel(x, meta_tokens, norm_mix, w_in, mla_q_norm, mla_w_uq, mla_kv_norm, mla_w_ukv, ssm_conv_w, ssm_conv_b, ssm_dt_bias, ssm_a_log, ssm_d_skip, ssm_norm, w_branch_attn, w_branch_ssm, w_out, norm_ffn, moe_w_group, moe_b_group, moe_w_expert, moe_b_expert, moe_w_gate, moe_w_up, moe_w_down, norm_final):
    b, seq, _ = x.shape
    assert w_in.shape[0] == 1, "one layer"
    rows = b * seq
    x2d = x.reshape(rows, D_MODEL)
    tm = _pick(seq, (512, 256, 128))
    tq = _pick(seq, (512, 256, 128))
    tk = _pick(tq, (256, 128))
    chunk = 128
    ssd_blk = _pick(seq, (512, 256, 128))

    w_in_r = _prep_w_in(w_in[0])
    wq = _prep_w_uq(mla_w_uq[0])
    wk, wvt = _prep_w_ukv(mla_w_ukv[0])
    g_mix = norm_mix[0].reshape(1, D_MODEL)
    qn = mla_q_norm[0].reshape(1, Q_RANK)
    kvn = mla_kv_norm[0].reshape(1, KV_RANK)
    cw = ssm_conv_w[0]
    cb = ssm_conv_b[0].reshape(1, CONV_DIM)
    dtb = _head_lanes(ssm_dt_bias[0])
    aneg = _head_lanes(-jnp.exp(ssm_a_log[0].astype(F32)))
    dsk = jnp.repeat(ssm_d_skip[0].astype(F32), SSM_HEAD_DIM).reshape(1, D_INNER)
    ng = ssm_norm[0].reshape(1, D_INNER)
    wa = w_branch_attn[0].astype(BF16)
    ws = w_branch_ssm[0].astype(BF16)
    wo = w_out[0].astype(BF16)
    nffn = norm_ffn[0].reshape(1, D_MODEL)
    wr = jnp.zeros((D_MODEL, LANES), F32)
    wr = wr.at[:, :N_EXPERT_GROUPS].set(moe_w_group[0]).at[:, EXPERT_LANE0:EXPERT_LANE0 + N_EXPERTS].set(moe_w_expert[0])
    wr = wr.astype(BF16)
    br = jnp.zeros((1, LANES), F32)
    br = br.at[0, :N_EXPERT_GROUPS].set(moe_b_group[0]).at[0, EXPERT_LANE0:EXPERT_LANE0 + N_EXPERTS].set(moe_b_expert[0])
    wg, wu, wd = moe_w_gate[0], moe_w_up[0], moe_w_down[0]
    nfin = norm_final.reshape(1, D_MODEL)

    meta_blk = jnp.concatenate([jnp.zeros((META_PAD, D_MODEL), F32), meta_tokens.astype(F32)], axis=0)
    m_cq, m_ckv, m_small, m_sz, m_xbc, _, _ = _inproj(meta_blk, g_mix, w_in_r, META_BLOCK)
    meta_pos = jnp.maximum(jnp.arange(META_BLOCK) - META_PAD, 0)
    _, kmeta, vmeta_t = _mla_prep(m_cq, m_ckv, m_small, qn, kvn, wq, wk, wvt, _rope_tables(meta_pos),
                                  META_BLOCK, 1, META_BLOCK, META_BLOCK)
    zero_state = jnp.zeros((LANES, D_INNER), F32)
    zero_halo = jnp.zeros((HALO, CONV_DIM), BF16)
    _, meta_state = _ssd(m_xbc[None], m_small[None], m_sz[None], zero_halo, zero_state,
                         cw, cb, dtb, aneg, dsk, ng, META_BLOCK, META_BLOCK, META_PAD)
    meta_halo = m_xbc[META_BLOCK - HALO:]

    cq, ckv, small, sz, xbc, ga, gs = _inproj(x2d, g_mix, w_in_r, tm)
    qt, k, vt = _mla_prep(cq, ckv, small, qn, kvn, wq, wk, wvt, _rope_tables(N_META + jnp.arange(seq)),
                          tm, seq // tm, tq, tk)
    o_attn = _flash(qt.reshape(b, seq // tq, -1, tq), k.reshape(b, seq, -1), vt.reshape(b, seq // tk, -1, tk),
                    kmeta, vmeta_t[0])
    o_ssm, _ = _ssd(xbc.reshape(b, seq, -1), small.reshape(b, seq, -1), sz.reshape(b, seq, -1), meta_halo,
                    meta_state[0], cw, cb, dtb, aneg, dsk, ng, ssd_blk, chunk, 0)
    h2, v, rt, counts = _merge(x2d, o_attn.reshape(rows, -1), o_ssm.reshape(rows, -1), ga, gs, wa, ws, wo,
                               nffn, wr, br, tm)
    tme = _pick(2 * rows, (512, 256, 128))
    tile_of, exp_of, offs, nwork = _moe_plan(counts[0, EXPERT_LANE0:EXPERT_LANE0 + N_EXPERTS], rows, tme)
    pos = _slots(rt, counts, tm)
    xs = _dispatch(pos, v, tm)
    ys = _experts(tile_of, exp_of, offs, nwork, xs, wg, wu, wd, tme)
    out = _combine(pos, rt, h2, nfin, ys, tm)
    return out.reshape(b, seq, D_MODEL)
```

```python
import functools

import numpy as np
import jax
import jax.numpy as jnp
from jax import lax
from jax.experimental import pallas as pl
from jax.experimental.pallas import tpu as pltpu

F32 = jnp.float32
BF16 = jnp.bfloat16

D_MODEL = 1024
N_META = 16
EPS = 1e-6
NEG = -1e30
MLA_HEADS = 16
Q_RANK = 256
KV_RANK = 128
NOPE_DIM = 64
ROPE_DIM = 32
V_DIM = 64
ROPE_THETA = 10000.0
SSM_HEADS = 16
SSM_HEAD_DIM = 64
D_INNER = SSM_HEADS * SSM_HEAD_DIM
SSM_GROUPS = 2
D_STATE = 64
CONV_WIDTH = 4
CONV_DIM = D_INNER + 2 * SSM_GROUPS * D_STATE
N_EXPERT_GROUPS = 4
EXPERTS_PER_GROUP = 8
N_EXPERTS = N_EXPERT_GROUPS * EXPERTS_PER_GROUP
D_EXPERT = 256

LANES = 128
META_BLOCK = 128
META_PAD = META_BLOCK - N_META
HEAD_SLOT = 128
V_SLOT = 80
FLASH_HEADS = 4
EXPERT_LANE0 = 32
DT_LANE0 = 96
VMEM_LIMIT = 48 * 1024 * 1024

A_CQ, A_CKV, A_SMALL, A_Z, A_XBC, A_GA, A_GS = 256, 128, 256, D_INNER, CONV_DIM, D_MODEL, D_MODEL
A_OFFS = np.cumsum([0, A_CQ, A_CKV, A_SMALL, A_Z, A_XBC, A_GA, A_GS])


def _cparams(sem):
    return pltpu.CompilerParams(dimension_semantics=sem, vmem_limit_bytes=VMEM_LIMIT)


def _rms(x, g):
    return x * lax.rsqrt(jnp.mean(x * x, axis=-1, keepdims=True) + EPS) * g


def _sigmoid(x):
    return 1.0 / (1.0 + jnp.exp(-x))


def _dot(a, b):
    return jnp.dot(a, b, preferred_element_type=F32)


def _dot_nt(a, b):
    return lax.dot_general(a, b, (((1,), (1,)), ((), ())), preferred_element_type=F32)


def _dot_tn(a, b):
    return lax.dot_general(a, b, (((0,), (0,)), ((), ())), preferred_element_type=F32)


def _split3(x):
    hi = x.astype(BF16)
    r1 = x - hi.astype(F32)
    mid = r1.astype(BF16)
    lo = (r1 - mid.astype(F32)).astype(BF16)
    return hi, mid, lo


def _dot_exact_rhs(x, m):
    hi, mid, lo = _split3(x)
    return _dot(hi, m) + _dot(mid, m) + _dot(lo, m)


def _dot_exact_lhs(m, x):
    hi, mid, lo = _split3(x)
    return _dot(m, hi) + _dot(m, mid) + _dot(m, lo)


def _inproj_body(x_ref, g_ref, w_ref, cq_ref, ckv_ref, small_ref, sz_ref, xbc_ref, ga_ref, gs_ref):
    u = _rms(x_ref[...], g_ref[...]).astype(BF16)

    def mm(i):
        return _dot(u, w_ref[:, A_OFFS[i]:A_OFFS[i + 1]])

    cq_ref[...] = mm(0).astype(BF16)
    ckv_ref[...] = mm(1).astype(BF16)
    small_ref[...] = mm(2)
    z = mm(3)
    sz_ref[...] = (z * _sigmoid(z)).astype(BF16)
    xbc_ref[...] = mm(4).astype(BF16)
    ga_ref[...] = _sigmoid(mm(5)).astype(BF16)
    gs_ref[...] = _sigmoid(mm(6)).astype(BF16)


def _inproj(x2d, g, w, tm):
    r = x2d.shape[0]
    widths = (A_CQ, A_CKV, A_SMALL, A_Z, A_XBC, A_GA, A_GS)
    dtypes = (BF16, BF16, F32, BF16, BF16, BF16, BF16)
    return pl.pallas_call(
        _inproj_body,
        grid=(r // tm,),
        in_specs=[
            pl.BlockSpec((tm, D_MODEL), lambda i: (i, 0)),
            pl.BlockSpec((1, D_MODEL), lambda i: (0, 0)),
            pl.BlockSpec(w.shape, lambda i: (0, 0), pipeline_mode=pl.Buffered(1)),
        ],
        out_specs=[pl.BlockSpec((tm, n), lambda i: (i, 0)) for n in widths],
        out_shape=[jax.ShapeDtypeStruct((r, n), dt) for n, dt in zip(widths, dtypes)],
        compiler_params=_cparams(("parallel",)),
        name="inproj",
    )(x2d, g, w)


def _mla_prep_body(cq_ref, ckv_ref, small_ref, qn_ref, kvn_ref, wqt_ref, wk_ref, wvt_ref, vone_ref,
                   cqt_ref, sqt_ref, ckt_ref, skt_ref, qt_ref, k_ref, vt_ref):
    cn = _rms(cq_ref[...].astype(F32), qn_ref[...]).astype(BF16)
    kn = _rms(ckv_ref[...].astype(F32), kvn_ref[...]).astype(BF16)
    cqt = cqt_ref[...]
    sqt = sqt_ref[...]
    all_heads = MLA_HEADS * HEAD_SLOT
    part = all_heads // 2
    tqb = qt_ref.shape[-1]
    r1, r2, r3 = NOPE_DIM, NOPE_DIM + ROPE_DIM // 2, NOPE_DIM + ROPE_DIM
    for lo in range(0, all_heads, part):
        qa = _dot_nt(wqt_ref[lo:lo + part, :], cn)
        for r0 in range(0, part, HEAD_SLOT):
            x = qa[r0:r0 + HEAD_SLOT]
            rot = jnp.concatenate([x[:r1], -x[r2:r3], x[r1:r2], x[r3:]], axis=0)
            qh = (x * cqt + rot * sqt).astype(BF16)
            for j in range(qt_ref.shape[0]):
                qt_ref[j, lo + r0:lo + r0 + HEAD_SLOT, :] = qh[:, j * tqb:(j + 1) * tqb]
    krp = small_ref[:, 0:LANES] * ckt_ref[...] + small_ref[:, LANES:2 * LANES] * skt_ref[...]
    krp2 = jnp.concatenate([krp, krp], axis=1)
    for hp in range(MLA_HEADS // 2):
        lo, hi = hp * 2 * HEAD_SLOT, (hp + 1) * 2 * HEAD_SLOT
        k_ref[:, lo:hi] = (_dot(kn, wk_ref[:, lo:hi]) + krp2).astype(BF16)
    vt = _dot_nt(wvt_ref[...], kn) + vone_ref[...]
    tkv = vt_ref.shape[-1]
    for j in range(vt_ref.shape[0]):
        vt_ref[j] = vt[:, j * tkv:(j + 1) * tkv].astype(BF16)


def _mla_prep(cq, ckv, small, qn, kvn, wqt, wk, wvt, tabs, tm, seq_blocks, tqb, tkv):
    r = cq.shape[0]
    nq = MLA_HEADS * HEAD_SLOT
    nv = MLA_HEADS * V_SLOT
    cqt, sqt, ckt, skt = tabs
    ones_row = (jnp.arange(nv) % V_SLOT == V_DIM).astype(F32)
    vone = jnp.broadcast_to(ones_row[:, None], (nv, tm))
    row = lambda n: pl.BlockSpec((tm, n), lambda i: (i, 0))
    tab = lambda n: pl.BlockSpec((tm, n), lambda i: (i % seq_blocks, 0))
    tab_t = pl.BlockSpec((HEAD_SLOT, tm), lambda i: (0, i % seq_blocks))
    full = lambda a: pl.BlockSpec(a.shape, lambda i: (0, 0))
    return pl.pallas_call(
        _mla_prep_body,
        grid=(r // tm,),
        in_specs=[row(Q_RANK), row(KV_RANK), row(A_SMALL), full(qn), full(kvn), full(wqt), full(wk), full(wvt),
                  full(vone), tab_t, tab_t, tab(HEAD_SLOT), tab(HEAD_SLOT)],
        out_specs=[pl.BlockSpec((tm // tqb, nq, tqb), lambda i: (i, 0, 0)), row(nq),
                   pl.BlockSpec((tm // tkv, nv, tkv), lambda i: (i, 0, 0))],
        out_shape=[jax.ShapeDtypeStruct((r // tqb, nq, tqb), BF16),
                   jax.ShapeDtypeStruct((r, nq), BF16),
                   jax.ShapeDtypeStruct((r // tkv, nv, tkv), BF16)],
        compiler_params=_cparams(("parallel",)),
        name="mla_prep",
    )(cq, ckv, small, qn, kvn, wqt, wk, wvt, vone, cqt, sqt, ckt, skt)


def _flash_body(qt_ref, k_ref, vt_ref, km_ref, vmt_ref, o_ref, *scratch, tq, tk, seq):
    nh = FLASH_HEADS
    s_scr = (scratch[0:nh], scratch[nh:2 * nh])
    p_scr = (scratch[2 * nh:3 * nh], scratch[3 * nh:4 * nh])
    acc_scr = scratch[4 * nh:5 * nh]
    key_i = lax.broadcasted_iota(jnp.int32, (tk, tq), 0)
    qry_i = lax.broadcasted_iota(jnp.int32, (tk, tq), 1)
    meta_row = lax.broadcasted_iota(jnp.int32, (META_BLOCK, 1), 0)
    heads = [slice(hh * HEAD_SLOT, (hh + 1) * HEAD_SLOT) for hh in range(nh)]
    vals = [slice(hh * V_SLOT, (hh + 1) * V_SLOT) for hh in range(nh)]
    ndiag = tq // tk
    assert ndiag % 2 == 0, "slot parity is static only when a query block spans an even number of key blocks"

    def probs(st, m):
        return jnp.exp2((st - m).astype(BF16))

    def qblock(qi, _):
        q0 = pl.multiple_of(qi * tq, tq)
        qs = [qt_ref[qi, hs, :] for hs in heads]
        nfull = qi * ndiag
        last = nfull + ndiag - 1

        def scores_into(slot, kj):
            k0 = pl.multiple_of(kj * tk, tk)
            for hh in range(nh):
                s_scr[slot][hh][...] = _dot(k_ref[pl.ds(k0, tk), heads[hh]], qs[hh])

        ms = []
        for hh in range(nh):
            st = jnp.where(meta_row >= META_PAD, _dot(km_ref[:, heads[hh]], qs[hh]), NEG)
            m = jnp.max(st, axis=0, keepdims=True)
            ms.append(m)
            p_scr[1][hh][0:META_BLOCK, :] = probs(st, m)
            p_scr[1][hh][META_BLOCK:, :] = jnp.zeros((tk - META_BLOCK, tq), BF16)
            acc_scr[hh][...] = jnp.zeros((V_SLOT, tq), F32)
        scores_into(0, 0)

        def kvstep(kj, slot, carry, masked, look_ahead=True, maybe_first=False):
            ms, alphas = carry
            kprev = jnp.maximum(kj - 1, 0)
            pvs = []
            for hh in range(nh):
                vt_prev = vt_ref[kprev, vals[hh], :]
                if maybe_first:
                    vt_prev = jnp.where(kj == 0, vmt_ref[vals[hh], :], vt_prev)
                pvs.append(_dot(vt_prev, p_scr[1 - slot][hh][...]))
            if look_ahead:
                scores_into(1 - slot, kj + 1)
            out = ([], [])
            for hh in range(nh):
                st = s_scr[slot][hh][...]
                if masked:
                    st = jnp.where(kj * tk + key_i <= q0 + qry_i, st, NEG)
                m_new = jnp.maximum(ms[hh], jnp.max(st, axis=0, keepdims=True))
                p_scr[slot][hh][...] = probs(st, m_new)
                acc_scr[hh][...] = alphas[hh] * acc_scr[hh][...] + pvs[hh]
                out[0].append(m_new)
                out[1].append(jnp.exp2(ms[hh] - m_new))
            return tuple(tuple(x) for x in out)

        def two_steps(t, carry):
            carry = kvstep(2 * t, 0, carry, False, maybe_first=True)
            return kvstep(2 * t + 1, 1, carry, False)

        ones = jnp.ones((1, tq), F32)
        carry = lax.fori_loop(0, nfull // 2, two_steps, (tuple(ms), (ones,) * nh))
        for d in range(ndiag):
            carry = kvstep(nfull + d, d % 2, carry, True, look_ahead=d + 1 < ndiag, maybe_first=(d == 0))
        _, alphas = carry
        outs = []
        for hh in range(nh):
            acc = alphas[hh] * acc_scr[hh][...] + _dot(vt_ref[last, vals[hh], :], p_scr[(ndiag - 1) % 2][hh][...])
            outs.append(acc[:V_DIM] / acc[V_DIM:V_DIM + 1])
        o_ref[pl.ds(q0, tq), :] = jnp.concatenate(outs, axis=0).T.astype(BF16)
        return 0

    lax.fori_loop(0, seq // tq, qblock, 0)


def _flash(qt, k, vt, kmeta, vmeta_t):
    b, seq, _ = k.shape
    tq = qt.shape[-1]
    tk = vt.shape[-1]
    nh = FLASH_HEADS
    return pl.pallas_call(
        functools.partial(_flash_body, tq=tq, tk=tk, seq=seq),
        grid=(b, MLA_HEADS // nh),
        in_specs=[
            pl.BlockSpec((None, seq // tq, nh * HEAD_SLOT, tq), lambda i, p: (i, 0, p, 0)),
            pl.BlockSpec((None, seq, nh * HEAD_SLOT), lambda i, p: (i, 0, p)),
            pl.BlockSpec((None, seq // tk, nh * V_SLOT, tk), lambda i, p: (i, 0, p, 0)),
            pl.BlockSpec((META_BLOCK, nh * HEAD_SLOT), lambda i, p: (0, p)),
            pl.BlockSpec((nh * V_SLOT, tk), lambda i, p: (p, 0)),
        ],
        out_specs=pl.BlockSpec((None, seq, nh * V_DIM), lambda i, p: (i, 0, p)),
        out_shape=jax.ShapeDtypeStruct((b, seq, MLA_HEADS * V_DIM), BF16),
        scratch_shapes=([pltpu.VMEM((tk, tq), F32)] * (2 * nh) + [pltpu.VMEM((tk, tq), BF16)] * (2 * nh)
                        + [pltpu.VMEM((V_SLOT, tq), F32)] * nh),
        compiler_params=_cparams(("parallel", "parallel")),
        name="flash",
    )(qt, k, vt, kmeta, vmeta_t)


HALO = 16


def _expand2(x, expand):
    hi = x.astype(BF16)
    mid = (x - hi.astype(F32)).astype(BF16)
    return _dot(hi, expand) + _dot(mid, expand)


def _ssd_body(xbc_ref, halo_ref, mh_ref, small_ref, sz_ref, cw_ref, cb_ref, dtb_ref, aneg_ref, dsk_ref,
              ng_ref, init_ref, o_ref, fin_ref, st_ref, y_ref, *, q, n_pad):
    c = pl.program_id(1)
    nc = pl.num_programs(1)

    @pl.when(c == 0)
    def _():
        st_ref[...] = init_ref[...]

    state = st_ref[...]
    halo = jnp.where(c == 0, mh_ref[...], halo_ref[...])
    for j in range(xbc_ref.shape[0] // q):
        rows = slice(j * q, (j + 1) * q)
        x_b = xbc_ref[rows, :]
        state = _ssd_chunk(x_b, halo, small_ref[rows, LANES:2 * LANES], sz_ref[rows, :], state,
                           cw_ref, cb_ref, dtb_ref, aneg_ref, dsk_ref, ng_ref, o_ref.at[rows, :], y_ref.at[rows, :],
                           q=q, n_pad=n_pad - j * q)
        halo = x_b[q - HALO:, :]
    st_ref[...] = state

    @pl.when(c == nc - 1)
    def _():
        fin_ref[...] = state


def _ssd_chunk(x_b, halo, dt_raw, sz, prev, cw_ref, cb_ref, dtb_ref, aneg_ref, dsk_ref, ng_ref, o_ref, y_ref,
               *, q, n_pad):
    x_ext = jnp.concatenate([halo, x_b], axis=0)
    sh_r = lax.broadcasted_iota(jnp.int32, (q, HALO + q), 0)
    sh_c = lax.broadcasted_iota(jnp.int32, (q, HALO + q), 1)
    xc = cb_ref[...] + cw_ref[CONV_WIDTH - 1:CONV_WIDTH, :] * x_b.astype(F32)
    for d in range(1, CONV_WIDTH):
        shifted = _dot((sh_c == sh_r + (HALO - d)).astype(BF16), x_ext)
        xc = xc + cw_ref[CONV_WIDTH - 1 - d:CONV_WIDTH - d, :] * shifted
    xc = xc * _sigmoid(xc)

    lane = lax.broadcasted_iota(jnp.int32, (q, LANES), 1)
    dt = dt_raw + dtb_ref[...]
    dt = jnp.maximum(dt, 0.0) + jnp.log(1.0 + jnp.exp(-jnp.abs(dt)))
    dt = jnp.where((lane >= DT_LANE0) & (lane < DT_LANE0 + SSM_HEADS), dt, 0.0)
    if n_pad > 0:
        rowv = lax.broadcasted_iota(jnp.int32, (q, 1), 0) >= n_pad
        xc = jnp.where(rowv, xc, 0.0)
        dt = jnp.where(rowv, dt, 0.0)

    xs = xc[:, :D_INNER]
    bm = xc[:, D_INNER:D_INNER + LANES]
    cm = xc[:, D_INNER + LANES:]

    a = dt * aneg_ref[...]
    ri = lax.broadcasted_iota(jnp.int32, (q, q), 0)
    ci = lax.broadcasted_iota(jnp.int32, (q, q), 1)
    tril = ri >= ci
    a_cs = _dot_exact_lhs(tril.astype(BF16), a)
    a_cs_t = a_cs.T

    er = lax.broadcasted_iota(jnp.int32, (LANES, D_INNER), 0)
    ec = lax.broadcasted_iota(jnp.int32, (LANES, D_INNER), 1)
    expand = (er - DT_LANE0 == ec // SSM_HEAD_DIM).astype(BF16)
    dt_x = _expand2(dt, expand)
    grow_x = _expand2(jnp.exp(a_cs), expand)
    rest_x = _expand2(jnp.exp(a_cs[q - 1:q, :] - a_cs), expand)

    xdt = xs * dt_x
    xdt_b = xdt.astype(BF16)
    xd_b = (xdt * rest_x).astype(BF16)

    bm_b = bm.astype(BF16)
    cm_b = cm.astype(BF16)
    lane_q = lax.broadcasted_iota(jnp.int32, (q, LANES), 1)

    for g in range(SSM_GROUPS):
        in_g = (lane_q >= g * D_STATE) & (lane_q < (g + 1) * D_STATE)
        cb = _dot_nt(jnp.where(in_g, cm, 0.0).astype(BF16), bm_b)
        for pr in range(SSM_HEADS // SSM_GROUPS // 2):
            pair = g * (SSM_HEADS // SSM_GROUPS // 2) + pr
            rhs = xdt_b[:, pair * LANES:(pair + 1) * LANES]
            ys = []
            for hh in range(2):
                e = DT_LANE0 + 2 * pair + hh
                seg = a_cs[:, e:e + 1] - a_cs_t[e:e + 1, :]
                lmat = jnp.exp(jnp.where(tril, seg, NEG))
                ys.append(_dot((cb * lmat).astype(BF16), rhs))
            y_ref[:, pair * LANES:(pair + 1) * LANES] = jnp.where(lane_q < SSM_HEAD_DIM, ys[0], ys[1])

    sr = lax.broadcasted_iota(jnp.int32, (LANES, D_INNER), 0)
    sc = lax.broadcasted_iota(jnp.int32, (LANES, D_INNER), 1)
    same_group = (sr // D_STATE) == (sc // (D_INNER // SSM_GROUPS))
    y_off = _dot(cm_b, prev.astype(BF16)) * grow_x
    st_new = prev * grow_x[q - 1:q, :] + jnp.where(same_group, _dot_tn(bm_b, xd_b), 0.0)

    y = y_ref[...] + y_off + xs * dsk_ref[...]
    yz = y * sz.astype(F32)
    half = D_INNER // SSM_GROUPS
    outs = []
    for g in range(SSM_GROUPS):
        part = yz[:, g * half:(g + 1) * half]
        outs.append(part * lax.rsqrt(jnp.mean(part * part, axis=-1, keepdims=True) + EPS))
    o_ref[...] = (jnp.concatenate(outs, axis=1) * ng_ref[...]).astype(BF16)
    return st_new


def _ssd(xbc, small, sz, meta_halo, init_state, cw, cb, dtb, aneg, dsk, ng, blk, q, n_pad):
    b, seq, _ = xbc.shape
    nc = seq // blk
    hb = blk // HALO
    full = lambda a: pl.BlockSpec(a.shape, lambda i, c: (0,) * a.ndim)
    return pl.pallas_call(
        functools.partial(_ssd_body, q=q, n_pad=n_pad),
        grid=(b, nc),
        in_specs=[
            pl.BlockSpec((None, blk, CONV_DIM), lambda i, c: (i, c, 0)),
            pl.BlockSpec((None, HALO, CONV_DIM), lambda i, c: (i, jnp.maximum(c * hb - 1, 0), 0)),
            full(meta_halo),
            pl.BlockSpec((None, blk, A_SMALL), lambda i, c: (i, c, 0)),
            pl.BlockSpec((None, blk, D_INNER), lambda i, c: (i, c, 0)),
            full(cw), full(cb), full(dtb), full(aneg), full(dsk), full(ng), full(init_state),
        ],
        out_specs=[
            pl.BlockSpec((None, blk, D_INNER), lambda i, c: (i, c, 0)),
            pl.BlockSpec((None, LANES, D_INNER), lambda i, c: (i, 0, 0)),
        ],
        out_shape=[jax.ShapeDtypeStruct((b, seq, D_INNER), BF16),
                   jax.ShapeDtypeStruct((b, LANES, D_INNER), F32)],
        scratch_shapes=[pltpu.VMEM((LANES, D_INNER), F32),
                        pltpu.VMEM((blk, D_INNER), F32)],
        compiler_params=_cparams(("parallel", "arbitrary")),
        name="ssd",
    )(xbc, xbc, meta_halo, small, sz, cw, cb, dtb, aneg, dsk, ng, init_state)


def _route(logits):
    lane = lax.broadcasted_iota(jnp.int32, logits.shape, 1).astype(F32)
    gl = jnp.where(lane < N_EXPERT_GROUPS, logits, NEG)
    gmax = jnp.max(gl, axis=1, keepdims=True)
    gidx = jnp.min(jnp.where(gl == gmax, lane, float(LANES)), axis=1, keepdims=True)
    p_g = 1.0 / jnp.sum(jnp.exp(gl - gmax), axis=1, keepdims=True)
    lo = EXPERT_LANE0 + EXPERTS_PER_GROUP * gidx
    el = jnp.where((lane >= lo) & (lane < lo + EXPERTS_PER_GROUP), logits, NEG)
    t1 = jnp.max(el, axis=1, keepdims=True)
    i1 = jnp.min(jnp.where(el == t1, lane, float(LANES)), axis=1, keepdims=True)
    el2 = jnp.where(lane == i1, NEG, el)
    t2 = jnp.max(el2, axis=1, keepdims=True)
    i2 = jnp.min(jnp.where(el2 == t2, lane, float(LANES)), axis=1, keepdims=True)
    e21 = jnp.exp(t2 - t1)
    w1 = p_g / (1.0 + e21)
    w2 = w1 * e21
    return lane, i1, i2, w1, w2


RT_E1, RT_E2, RT_R1, RT_R2, RT_W1, RT_W2 = range(6)


def _merge_body(x_ref, oa_ref, os_ref, ga_ref, gs_ref, wa_ref, ws_ref, wo_ref, nf_ref, wr_ref, br_ref,
                h2_ref, v_ref, rt_ref, cnt_out_ref, cnt_ref):
    @pl.when(pl.program_id(0) == 0)
    def _():
        cnt_ref[...] = jnp.zeros_like(cnt_ref)

    merged = (ga_ref[...].astype(F32) * _dot(oa_ref[...], wa_ref[...])
              + gs_ref[...].astype(F32) * _dot(os_ref[...], ws_ref[...]))
    h2 = x_ref[...] + _dot(merged.astype(BF16), wo_ref[...])
    h2_ref[...] = h2
    v = _rms(h2, nf_ref[...])
    _store_token_slabs(v_ref, v)
    lane, i1, i2, w1, w2 = _route(_dot(v.astype(BF16), wr_ref[...]) + br_ref[...])

    tm = lane.shape[0]
    chosen = jnp.where((lane == i1) | (lane == i2), 1.0, 0.0)
    ri = lax.broadcasted_iota(jnp.int32, (tm, tm), 0)
    ci = lax.broadcasted_iota(jnp.int32, (tm, tm), 1)
    earlier = _dot((ri > ci).astype(BF16), chosen.astype(BF16)) + cnt_ref[...]
    r1 = jnp.sum(jnp.where(lane == i1, earlier, 0.0), axis=1, keepdims=True)
    r2 = jnp.sum(jnp.where(lane == i2, earlier, 0.0), axis=1, keepdims=True)
    cnt_ref[...] += jnp.sum(chosen, axis=0, keepdims=True)
    cnt_out_ref[...] = jnp.broadcast_to(cnt_ref[...], cnt_out_ref.shape)

    rec = jnp.zeros_like(lane)
    for col, val in ((RT_E1, i1 - EXPERT_LANE0), (RT_E2, i2 - EXPERT_LANE0), (RT_R1, r1), (RT_R2, r2),
                     (RT_W1, w1), (RT_W2, w2)):
        rec = jnp.where(lane == col, val, rec)
    rt_ref[...] = rec


def _merge(x2d, oa, osm, ga, gs, wa, ws, wo, nf, wr, br, tm):
    r = x2d.shape[0]
    row = lambda n: pl.BlockSpec((tm, n), lambda i: (i, 0))
    full = lambda a: pl.BlockSpec(a.shape, lambda i: (0, 0))
    return pl.pallas_call(
        _merge_body,
        grid=(r // tm,),
        in_specs=[row(D_MODEL)] * 5 + [full(wa), full(ws), full(wo), full(nf), full(wr), full(br)],
        out_specs=[row(D_MODEL), pl.BlockSpec((tm * SLAB, LANES), lambda i: (i, 0)), row(LANES),
                   pl.BlockSpec((8, LANES), lambda i: (0, 0))],
        out_shape=[jax.ShapeDtypeStruct((r, D_MODEL), F32),
                   jax.ShapeDtypeStruct((r * SLAB, LANES), F32),
                   jax.ShapeDtypeStruct((r, LANES), F32),
                   jax.ShapeDtypeStruct((8, LANES), F32)],
        scratch_shapes=[pltpu.VMEM((1, LANES), F32)],
        compiler_params=_cparams(("arbitrary",)),
        name="merge",
    )(x2d, oa, osm, ga, gs, wa, ws, wo, nf, wr, br)


SLAB = D_MODEL // LANES


def _store_token_slabs(ref, x):
    tm = x.shape[0]
    for s in range(SLAB):
        ref[pl.ds(s, tm, stride=SLAB), :] = x[:, s * LANES:(s + 1) * LANES]


def _load_token_slabs(ref):
    tm = ref.shape[0] // SLAB
    return jnp.concatenate([ref[pl.ds(s, tm, stride=SLAB), :] for s in range(SLAB)], axis=1)


def _row_copy(src_ref, src_row, dst_ref, dst_row, sem):
    src = src_ref.at[pl.ds(pl.multiple_of(src_row * SLAB, SLAB), SLAB)]
    dst = dst_ref.at[pl.ds(pl.multiple_of(dst_row * SLAB, SLAB), SLAB)]
    return pltpu.make_async_copy(src, dst, sem)


def _dispatch_body(pos_ref, v_ref, xs_ref, sem):
    tm = v_ref.shape[0] // SLAB

    def issue(t, _):
        _row_copy(v_ref, t, xs_ref, pos_ref[0, t], sem).start(priority=0)
        _row_copy(v_ref, t, xs_ref, pos_ref[1, t], sem).start(priority=1)
        return 0

    lax.fori_loop(0, tm, issue, 0, unroll=8)
    for _ in range(2):
        pltpu.make_async_copy(v_ref, xs_ref.at[pl.ds(0, tm * SLAB)], sem).wait()


def _slot_spec(tm):
    return pl.BlockSpec((8, tm), lambda i: (0, i), memory_space=pltpu.SMEM)


def _dispatch(pos, v, tm):
    r = v.shape[0] // SLAB
    return pl.pallas_call(
        _dispatch_body,
        grid=(r // tm,),
        in_specs=[_slot_spec(tm), pl.BlockSpec((tm * SLAB, LANES), lambda i: (i, 0))],
        out_specs=pl.BlockSpec(memory_space=pl.ANY),
        out_shape=jax.ShapeDtypeStruct((2 * r * SLAB, LANES), F32),
        scratch_shapes=[pltpu.SemaphoreType.DMA],
        compiler_params=_cparams(("arbitrary",)),
        name="moe_dispatch",
    )(pos, v)


def _experts_body(tile_ref, exp_ref, off_ref, nwork_ref, xs_ref, wg_ref, wu_ref, wd_ref, ys_ref,
                  wg_s, wu_s, wd_s):
    w = pl.program_id(0)
    tme = xs_ref.shape[0] // SLAB

    @pl.when(w < nwork_ref[0])
    def _():
        e = exp_ref[w]
        tile = tile_ref[w]
        prev = jnp.maximum(w - 1, 0)
        new_expert = (w == 0) | (exp_ref[prev] != e)
        new_tile = (w == 0) | (tile_ref[prev] != tile)

        @pl.when(new_expert)
        def _():
            wg_s[...] = wg_ref[...].astype(BF16)
            wu_s[...] = wu_ref[...].astype(BF16)
            wd_s[...] = wd_ref[...].astype(BF16)

        x = _load_token_slabs(xs_ref).astype(BF16)
        g = _dot(x, wg_s[...])
        u = _dot(x, wu_s[...])
        y = _dot((g * _sigmoid(g) * u).astype(BF16), wd_s[...])
        rows = tile * tme + lax.broadcasted_iota(jnp.int32, (tme, 1), 0)
        mine = (rows >= off_ref[e]) & (rows < off_ref[e + 1])

        @pl.when(new_tile)
        def _():
            _store_token_slabs(ys_ref, jnp.where(mine, y, 0.0))

        @pl.when(jnp.logical_not(new_tile))
        def _():
            _store_token_slabs(ys_ref, jnp.where(mine, y, _load_token_slabs(ys_ref)))


def _experts(tile_of, exp_of, offs, nwork, xs, wg, wu, wd, tme):
    nw = tile_of.shape[0]
    grid_spec = pltpu.PrefetchScalarGridSpec(
        num_scalar_prefetch=4,
        grid=(nw,),
        in_specs=[
            pl.BlockSpec((tme * SLAB, LANES), lambda w, t, e, o, n: (t[w], 0)),
            pl.BlockSpec((None, D_MODEL, D_EXPERT), lambda w, t, e, o, n: (e[w], 0, 0)),
            pl.BlockSpec((None, D_MODEL, D_EXPERT), lambda w, t, e, o, n: (e[w], 0, 0)),
            pl.BlockSpec((None, D_EXPERT, D_MODEL), lambda w, t, e, o, n: (e[w], 0, 0)),
        ],
        out_specs=pl.BlockSpec((tme * SLAB, LANES), lambda w, t, e, o, n: (t[w], 0)),
        scratch_shapes=[pltpu.VMEM((D_MODEL, D_EXPERT), BF16), pltpu.VMEM((D_MODEL, D_EXPERT), BF16),
                        pltpu.VMEM((D_EXPERT, D_MODEL), BF16)],
    )
    return pl.pallas_call(
        _experts_body,
        grid_spec=grid_spec,
        out_shape=jax.ShapeDtypeStruct(xs.shape, F32),
        compiler_params=_cparams(("arbitrary",)),
        name="moe_experts",
    )(tile_of, exp_of, offs, nwork, xs, wg, wu, wd)


def _combine_body(pos_ref, pos_next_ref, rt_ref, h2_ref, nf_ref, ys_ref, out_ref, y1_ref, y2_ref, sem):
    i = pl.program_id(0)
    n = pl.num_programs(0)
    tm = h2_ref.shape[0]
    slot = i % 2

    def issue_tile(p_ref, s):
        def issue(t, _):
            _row_copy(ys_ref, p_ref[0, t], y1_ref.at[s], t, sem.at[s]).start(priority=0)
            _row_copy(ys_ref, p_ref[1, t], y2_ref.at[s], t, sem.at[s]).start(priority=1)
            return 0

        lax.fori_loop(0, tm, issue, 0, unroll=8)

    @pl.when(i == 0)
    def _():
        issue_tile(pos_ref, 0)

    @pl.when(i + 1 < n)
    def _():
        issue_tile(pos_next_ref, 1 - slot)

    for buf in (y1_ref, y2_ref):
        pltpu.make_async_copy(ys_ref.at[pl.ds(0, tm * SLAB)], buf.at[slot], sem.at[slot]).wait()
    lane = lax.broadcasted_iota(jnp.int32, rt_ref.shape, 1)
    rt = rt_ref[...]
    w1 = jnp.sum(jnp.where(lane == RT_W1, rt, 0.0), axis=1, keepdims=True)
    w2 = jnp.sum(jnp.where(lane == RT_W2, rt, 0.0), axis=1, keepdims=True)
    moe = w1 * _load_token_slabs(y1_ref.at[slot]) + w2 * _load_token_slabs(y2_ref.at[slot])
    out_ref[...] = _rms(h2_ref[...] + moe, nf_ref[...])


def _combine(pos, rt, h2, nf, ys, tm):
    r = h2.shape[0]
    last = r // tm - 1
    next_slots = pl.BlockSpec((8, tm), lambda i: (0, jnp.minimum(i + 1, last)), memory_space=pltpu.SMEM)
    return pl.pallas_call(
        _combine_body,
        grid=(r // tm,),
        in_specs=[_slot_spec(tm), next_slots, pl.BlockSpec((tm, LANES), lambda i: (i, 0)),
                  pl.BlockSpec((tm, D_MODEL), lambda i: (i, 0)), pl.BlockSpec((1, D_MODEL), lambda i: (0, 0)),
                  pl.BlockSpec(memory_space=pl.ANY)],
        out_specs=pl.BlockSpec((tm, D_MODEL), lambda i: (i, 0)),
        out_shape=jax.ShapeDtypeStruct((r, D_MODEL), F32),
        scratch_shapes=[pltpu.VMEM((2, tm * SLAB, LANES), F32), pltpu.VMEM((2, tm * SLAB, LANES), F32),
                        pltpu.SemaphoreType.DMA((2,))],
        compiler_params=_cparams(("arbitrary",)),
        name="moe_combine",
    )(pos, pos, rt, h2, nf, ys)


def _slots_body(rt_ref, cnt_ref, pos_ref):
    rt = rt_ref[...]
    lane = lax.broadcasted_iota(jnp.int32, rt.shape, 1).astype(F32)
    col = lambda c: jnp.sum(jnp.where(lane == c, rt, 0.0), axis=1, keepdims=True)
    ri = lax.broadcasted_iota(jnp.int32, (LANES, LANES), 0)
    ci = lax.broadcasted_iota(jnp.int32, (LANES, LANES), 1)
    first = _dot_exact_rhs(cnt_ref[...], (ri < ci).astype(BF16))[0:1, :]
    slot = lambda e, r: col(r) + jnp.sum(jnp.where(lane == col(e) + EXPERT_LANE0, first, 0.0),
                                         axis=1, keepdims=True)
    rec = jnp.where(lane == 0, slot(RT_E1, RT_R1), jnp.where(lane == 1, slot(RT_E2, RT_R2), 0.0))
    sr = lax.broadcasted_iota(jnp.int32, (8, LANES), 0)
    sc = lax.broadcasted_iota(jnp.int32, (8, LANES), 1)
    sel = (sr == sc).astype(BF16)
    pos_ref[...] = sum(_dot_nt(sel, part) for part in _split3(rec)).astype(jnp.int32)


def _slots(rt, counts, tm):
    r = rt.shape[0]
    return pl.pallas_call(
        _slots_body,
        grid=(r // tm,),
        in_specs=[pl.BlockSpec((tm, LANES), lambda i: (i, 0)), pl.BlockSpec((8, LANES), lambda i: (0, 0))],
        out_specs=pl.BlockSpec((8, tm), lambda i: (0, i)),
        out_shape=jax.ShapeDtypeStruct((8, r), jnp.int32),
        compiler_params=_cparams(("parallel",)),
        name="moe_slots",
    )(rt, counts)


def _moe_plan(counts, rows, tme):
    cnt = counts.astype(jnp.int32)
    ends = jnp.cumsum(cnt)
    offs = jnp.concatenate([jnp.zeros((1,), jnp.int32), ends])
    first_tile = offs[:-1] // tme
    last_tile = (ends - 1) // tme
    n_items = jnp.where(cnt > 0, last_tile - first_tile + 1, 0)
    item_end = jnp.cumsum(n_items)
    nwork = item_end[-1]
    nw = 2 * rows // tme + N_EXPERTS - 1
    w = jnp.arange(nw, dtype=jnp.int32)
    wc = jnp.minimum(w, nwork - 1)
    exp_of = jnp.sum((wc[:, None] >= item_end[None, :]).astype(jnp.int32), axis=1)
    onehot = (exp_of[:, None] == jnp.arange(N_EXPERTS, dtype=jnp.int32)[None, :]).astype(jnp.int32)
    item_start = jnp.sum(onehot * (item_end - n_items)[None, :], axis=1)
    tile_of = (jnp.sum(onehot * first_tile[None, :], axis=1) + (wc - item_start)).astype(jnp.int32)
    return tile_of, exp_of, offs, nwork.reshape(1).astype(jnp.int32)


def _prep_w_in(w):
    o = np.cumsum([0, Q_RANK, KV_RANK, ROPE_DIM, D_INNER, CONV_DIM, SSM_HEADS, D_MODEL, D_MODEL])
    cq, ckv, kr, z, xbc, dt, ga, gs = (w[:, o[i]:o[i + 1]] for i in range(8))
    hr = ROPE_DIM // 2
    zeros = lambda n: jnp.zeros((w.shape[0], n), w.dtype)
    kra = jnp.concatenate([zeros(NOPE_DIM), kr, zeros(HEAD_SLOT - NOPE_DIM - ROPE_DIM)], axis=1)
    krb = jnp.concatenate([zeros(NOPE_DIM), -kr[:, hr:], kr[:, :hr], dt,
                           zeros(HEAD_SLOT - DT_LANE0 - SSM_HEADS)], axis=1)
    return jnp.concatenate([cq, ckv, kra, krb, z, xbc, ga, gs], axis=1).astype(BF16)


def _prep_w_uq(w):
    w = w.reshape(Q_RANK, MLA_HEADS, NOPE_DIM + ROPE_DIM)
    hr = ROPE_DIM // 2
    nope, r1, r2 = w[..., :NOPE_DIM], w[..., NOPE_DIM:NOPE_DIM + hr], w[..., NOPE_DIM + hr:]
    z = jnp.zeros((Q_RANK, MLA_HEADS, HEAD_SLOT - NOPE_DIM - ROPE_DIM), w.dtype)
    wa = jnp.concatenate([nope, r1, r2, z], axis=-1).reshape(Q_RANK, -1)
    return wa.T.astype(BF16)


def _prep_w_ukv(w):
    w = w.reshape(KV_RANK, MLA_HEADS, NOPE_DIM + V_DIM)
    kn, v = w[..., :NOPE_DIM], w[..., NOPE_DIM:]
    wk = jnp.concatenate([kn, jnp.zeros((KV_RANK, MLA_HEADS, HEAD_SLOT - NOPE_DIM), w.dtype)], axis=-1)
    vslot = jnp.concatenate([v, jnp.zeros((KV_RANK, MLA_HEADS, V_SLOT - V_DIM), w.dtype)], axis=-1)
    return wk.reshape(KV_RANK, -1).astype(BF16), vslot.reshape(KV_RANK, -1).T.astype(BF16)


def _rope_tables(pos):
    inv = ROPE_THETA ** (-jnp.arange(0, ROPE_DIM, 2, dtype=F32) / ROPE_DIM)
    ang = pos.astype(F32)[:, None] * inv[None, :]
    cos, sin = jnp.cos(ang), jnp.sin(ang)
    n = pos.shape[0]
    pad = jnp.zeros((n, HEAD_SLOT - NOPE_DIM - ROPE_DIM), F32)
    scale = (NOPE_DIM + ROPE_DIM) ** -0.5 * float(np.log2(np.e))
    cq = jnp.concatenate([jnp.ones((n, NOPE_DIM), F32), cos, cos, pad], axis=1) * scale
    sq = jnp.concatenate([jnp.zeros((n, NOPE_DIM), F32), sin, sin, pad], axis=1) * scale
    ck = jnp.concatenate([jnp.zeros((n, NOPE_DIM), F32), cos, cos, pad], axis=1)
    sk = jnp.concatenate([jnp.zeros((n, NOPE_DIM), F32), sin, sin, pad], axis=1)
    return cq.T, sq.T, ck, sk


def _head_lanes(v):
    return jnp.zeros((1, LANES), F32).at[0, DT_LANE0:DT_LANE0 + SSM_HEADS].set(v.astype(F32))


def _pick(n, prefs):
    for t in prefs:
        if n % t == 0:
            return t
    raise ValueError(f"no tile for {n}")


def kernel(x, meta_tokens, norm_mix, w_in, mla_q_norm, mla_w_uq, mla_kv_norm, mla_w_ukv, ssm_conv_w, ssm_conv_b, ssm_dt_bias, ssm_a_log, ssm_d_skip, ssm_norm, w_branch_attn, w_branch_ssm, w_out, norm_ffn, moe_w_group, moe_b_group, moe_w_expert, moe_b_expert, moe_w_gate, moe_w_up, moe_w_down, norm_final):
    b, seq, _ = x.shape
    assert w_in.shape[0] == 1, "one layer"
    rows = b * seq
    x2d = x.reshape(rows, D_MODEL)
    tm = _pick(seq, (512, 256, 128))
    tq = _pick(seq, (512, 256, 128))
    tk = _pick(tq, (256, 128))
    chunk = 128
    ssd_blk = _pick(seq, (512, 256, 128))

    w_in_r = _prep_w_in(w_in[0])
    wq = _prep_w_uq(mla_w_uq[0])
    wk, wvt = _prep_w_ukv(mla_w_ukv[0])
    g_mix = norm_mix[0].reshape(1, D_MODEL)
    qn = mla_q_norm[0].reshape(1, Q_RANK)
    kvn = mla_kv_norm[0].reshape(1, KV_RANK)
    cw = ssm_conv_w[0]
    cb = ssm_conv_b[0].reshape(1, CONV_DIM)
    dtb = _head_lanes(ssm_dt_bias[0])
    aneg = _head_lanes(-jnp.exp(ssm_a_log[0].astype(F32)))
    dsk = jnp.repeat(ssm_d_skip[0].astype(F32), SSM_HEAD_DIM).reshape(1, D_INNER)
    ng = ssm_norm[0].reshape(1, D_INNER)
    wa = w_branch_attn[0].astype(BF16)
    ws = w_branch_ssm[0].astype(BF16)
    wo = w_out[0].astype(BF16)
    nffn = norm_ffn[0].reshape(1, D_MODEL)
    wr = jnp.zeros((D_MODEL, LANES), F32)
    wr = wr.at[:, :N_EXPERT_GROUPS].set(moe_w_group[0]).at[:, EXPERT_LANE0:EXPERT_LANE0 + N_EXPERTS].set(moe_w_expert[0])
    wr = wr.astype(BF16)
    br = jnp.zeros((1, LANES), F32)
    br = br.at[0, :N_EXPERT_GROUPS].set(moe_b_group[0]).at[0, EXPERT_LANE0:EXPERT_LANE0 + N_EXPERTS].set(moe_b_expert[0])
    wg, wu, wd = moe_w_gate[0], moe_w_up[0], moe_w_down[0]
    nfin = norm_final.reshape(1, D_MODEL)

    meta_blk = jnp.concatenate([jnp.zeros((META_PAD, D_MODEL), F32), meta_tokens.astype(F32)], axis=0)
    m_cq, m_ckv, m_small, m_sz, m_xbc, _, _ = _inproj(meta_blk, g_mix, w_in_r, META_BLOCK)
    meta_pos = jnp.maximum(jnp.arange(META_BLOCK) - META_PAD, 0)
    _, kmeta, vmeta_t = _mla_prep(m_cq, m_ckv, m_small, qn, kvn, wq, wk, wvt, _rope_tables(meta_pos),
                                  META_BLOCK, 1, META_BLOCK, META_BLOCK)
    zero_state = jnp.zeros((LANES, D_INNER), F32)
    zero_halo = jnp.zeros((HALO, CONV_DIM), BF16)
    _, meta_state = _ssd(m_xbc[None], m_small[None], m_sz[None], zero_halo, zero_state,
                         cw, cb, dtb, aneg, dsk, ng, META_BLOCK, META_BLOCK, META_PAD)
    meta_halo = m_xbc[META_BLOCK - HALO:]

    cq, ckv, small, sz, xbc, ga, gs = _inproj(x2d, g_mix, w_in_r, tm)
    qt, k, vt = _mla_prep(cq, ckv, small, qn, kvn, wq, wk, wvt, _rope_tables(N_META + jnp.arange(seq)),
                          tm, seq // tm, tq, tk)
    o_attn = _flash(qt.reshape(b, seq // tq, -1, tq), k.reshape(b, seq, -1), vt.reshape(b, seq // tk, -1, tk),
                    kmeta, jnp.pad(vmeta_t[0], ((0, 0), (0, tk - META_BLOCK))))
    o_ssm, _ = _ssd(xbc.reshape(b, seq, -1), small.reshape(b, seq, -1), sz.reshape(b, seq, -1), meta_halo,
                    meta_state[0], cw, cb, dtb, aneg, dsk, ng, ssd_blk, chunk, 0)
    h2, v, rt, counts = _merge(x2d, o_attn.reshape(rows, -1), o_ssm.reshape(rows, -1), ga, gs, wa, ws, wo,
                               nffn, wr, br, tm)
    tme = _pick(2 * rows, (512, 256, 128))
    tile_of, exp_of, offs, nwork = _moe_plan(counts[0, EXPERT_LANE0:EXPERT_LANE0 + N_EXPERTS], rows, tme)
    pos = _slots(rt, counts, tm)
    xs = _dispatch(pos, v, tm)
    ys = _experts(tile_of, exp_of, offs, nwork, xs, wg, wu, wd, tme)
    out = _combine(pos, rt, h2, nfin, ys, tm)
    return out.reshape(b, seq, D_MODEL)
```

```python
import functools

import numpy as np
import jax
import jax.numpy as jnp
from jax import lax
from jax.experimental import pallas as pl
from jax.experimental.pallas import tpu as pltpu

F32 = jnp.float32
BF16 = jnp.bfloat16

D_MODEL = 1024
N_META = 16
EPS = 1e-6
NEG = -1e30
MLA_HEADS = 16
Q_RANK = 256
KV_RANK = 128
NOPE_DIM = 64
ROPE_DIM = 32
V_DIM = 64
ROPE_THETA = 10000.0
SSM_HEADS = 16
SSM_HEAD_DIM = 64
D_INNER = SSM_HEADS * SSM_HEAD_DIM
SSM_GROUPS = 2
D_STATE = 64
CONV_WIDTH = 4
CONV_DIM = D_INNER + 2 * SSM_GROUPS * D_STATE
N_EXPERT_GROUPS = 4
EXPERTS_PER_GROUP = 8
N_EXPERTS = N_EXPERT_GROUPS * EXPERTS_PER_GROUP
D_EXPERT = 256

LANES = 128
META_BLOCK = 128
META_PAD = META_BLOCK - N_META
HEAD_SLOT = 128
V_SLOT = 80
FLASH_HEADS = 4
EXPERT_LANE0 = 32
DT_LANE0 = 96
VMEM_LIMIT = 48 * 1024 * 1024

A_CQ, A_CKV, A_SMALL, A_Z, A_XBC, A_GA, A_GS = 256, 128, 256, D_INNER, CONV_DIM, D_MODEL, D_MODEL
A_OFFS = np.cumsum([0, A_CQ, A_CKV, A_SMALL, A_Z, A_XBC, A_GA, A_GS])


def _cparams(sem):
    return pltpu.CompilerParams(dimension_semantics=sem, vmem_limit_bytes=VMEM_LIMIT)


def _rms(x, g):
    return x * lax.rsqrt(jnp.mean(x * x, axis=-1, keepdims=True) + EPS) * g


def _sigmoid(x):
    return 1.0 / (1.0 + jnp.exp(-x))


def _dot(a, b):
    return jnp.dot(a, b, preferred_element_type=F32)


def _dot_nt(a, b):
    return lax.dot_general(a, b, (((1,), (1,)), ((), ())), preferred_element_type=F32)


def _dot_tn(a, b):
    return lax.dot_general(a, b, (((0,), (0,)), ((), ())), preferred_element_type=F32)


def _split3(x):
    hi = x.astype(BF16)
    r1 = x - hi.astype(F32)
    mid = r1.astype(BF16)
    lo = (r1 - mid.astype(F32)).astype(BF16)
    return hi, mid, lo


def _dot_exact_rhs(x, m):
    hi, mid, lo = _split3(x)
    return _dot(hi, m) + _dot(mid, m) + _dot(lo, m)


def _dot_exact_lhs(m, x):
    hi, mid, lo = _split3(x)
    return _dot(m, hi) + _dot(m, mid) + _dot(m, lo)


def _inproj_body(x_ref, g_ref, w_ref, cq_ref, ckv_ref, small_ref, sz_ref, xbc_ref, ga_ref, gs_ref):
    u = _rms(x_ref[...], g_ref[...]).astype(BF16)

    def mm(i):
        return _dot(u, w_ref[:, A_OFFS[i]:A_OFFS[i + 1]])

    cq_ref[...] = mm(0).astype(BF16)
    ckv_ref[...] = mm(1).astype(BF16)
    small_ref[...] = mm(2)
    z = mm(3)
    sz_ref[...] = (z * _sigmoid(z)).astype(BF16)
    xbc_ref[...] = mm(4).astype(BF16)
    ga_ref[...] = _sigmoid(mm(5)).astype(BF16)
    gs_ref[...] = _sigmoid(mm(6)).astype(BF16)


def _inproj(x2d, g, w, tm):
    r = x2d.shape[0]
    widths = (A_CQ, A_CKV, A_SMALL, A_Z, A_XBC, A_GA, A_GS)
    dtypes = (BF16, BF16, F32, BF16, BF16, BF16, BF16)
    return pl.pallas_call(
        _inproj_body,
        grid=(r // tm,),
        in_specs=[
            pl.BlockSpec((tm, D_MODEL), lambda i: (i, 0)),
            pl.BlockSpec((1, D_MODEL), lambda i: (0, 0)),
            pl.BlockSpec(w.shape, lambda i: (0, 0), pipeline_mode=pl.Buffered(1)),
        ],
        out_specs=[pl.BlockSpec((tm, n), lambda i: (i, 0)) for n in widths],
        out_shape=[jax.ShapeDtypeStruct((r, n), dt) for n, dt in zip(widths, dtypes)],
        compiler_params=_cparams(("parallel",)),
        name="inproj",
    )(x2d, g, w)


def _mla_prep_body(cq_ref, ckv_ref, small_ref, qn_ref, kvn_ref, wqt_ref, wk_ref, wvt_ref, vone_ref,
                   cqt_ref, sqt_ref, ckt_ref, skt_ref, qt_ref, k_ref, vt_ref):
    cn = _rms(cq_ref[...].astype(F32), qn_ref[...]).astype(BF16)
    kn = _rms(ckv_ref[...].astype(F32), kvn_ref[...]).astype(BF16)
    cqt = cqt_ref[...]
    sqt = sqt_ref[...]
    all_heads = MLA_HEADS * HEAD_SLOT
    part = all_heads // 2
    tqb = qt_ref.shape[-1]
    r1, r2, r3 = NOPE_DIM, NOPE_DIM + ROPE_DIM // 2, NOPE_DIM + ROPE_DIM
    for lo in range(0, all_heads, part):
        qa = _dot_nt(wqt_ref[lo:lo + part, :], cn)
        for r0 in range(0, part, HEAD_SLOT):
            x = qa[r0:r0 + HEAD_SLOT]
            rot = jnp.concatenate([x[:r1], -x[r2:r3], x[r1:r2], x[r3:]], axis=0)
            qh = (x * cqt + rot * sqt).astype(BF16)
            for j in range(qt_ref.shape[0]):
                qt_ref[j, lo + r0:lo + r0 + HEAD_SLOT, :] = qh[:, j * tqb:(j + 1) * tqb]
    krp = small_ref[:, 0:LANES] * ckt_ref[...] + small_ref[:, LANES:2 * LANES] * skt_ref[...]
    krp2 = jnp.concatenate([krp, krp], axis=1)
    for hp in range(MLA_HEADS // 2):
        lo, hi = hp * 2 * HEAD_SLOT, (hp + 1) * 2 * HEAD_SLOT
        k_ref[:, lo:hi] = (_dot(kn, wk_ref[:, lo:hi]) + krp2).astype(BF16)
    vt = _dot_nt(wvt_ref[...], kn) + vone_ref[...]
    tkv = vt_ref.shape[-1]
    for j in range(vt_ref.shape[0]):
        vt_ref[j] = vt[:, j * tkv:(j + 1) * tkv].astype(BF16)


def _mla_prep(cq, ckv, small, qn, kvn, wqt, wk, wvt, tabs, tm, seq_blocks, tqb, tkv):
    r = cq.shape[0]
    nq = MLA_HEADS * HEAD_SLOT
    nv = MLA_HEADS * V_SLOT
    cqt, sqt, ckt, skt = tabs
    ones_row = (jnp.arange(nv) % V_SLOT == V_DIM).astype(F32)
    vone = jnp.broadcast_to(ones_row[:, None], (nv, tm))
    row = lambda n: pl.BlockSpec((tm, n), lambda i: (i, 0))
    tab = lambda n: pl.BlockSpec((tm, n), lambda i: (i % seq_blocks, 0))
    tab_t = pl.BlockSpec((HEAD_SLOT, tm), lambda i: (0, i % seq_blocks))
    full = lambda a: pl.BlockSpec(a.shape, lambda i: (0, 0))
    return pl.pallas_call(
        _mla_prep_body,
        grid=(r // tm,),
        in_specs=[row(Q_RANK), row(KV_RANK), row(A_SMALL), full(qn), full(kvn), full(wqt), full(wk), full(wvt),
                  full(vone), tab_t, tab_t, tab(HEAD_SLOT), tab(HEAD_SLOT)],
        out_specs=[pl.BlockSpec((tm // tqb, nq, tqb), lambda i: (i, 0, 0)), row(nq),
                   pl.BlockSpec((tm // tkv, nv, tkv), lambda i: (i, 0, 0))],
        out_shape=[jax.ShapeDtypeStruct((r // tqb, nq, tqb), BF16),
                   jax.ShapeDtypeStruct((r, nq), BF16),
                   jax.ShapeDtypeStruct((r // tkv, nv, tkv), BF16)],
        compiler_params=_cparams(("parallel",)),
        name="mla_prep",
    )(cq, ckv, small, qn, kvn, wqt, wk, wvt, vone, cqt, sqt, ckt, skt)


def _flash_body(qt_ref, k_ref, vt_ref, km_ref, vmt_ref, o_ref, *scratch, tq, tk, seq):
    nh = FLASH_HEADS
    s_scr = (scratch[0:nh], scratch[nh:2 * nh])
    p_scr = (scratch[2 * nh:3 * nh], scratch[3 * nh:4 * nh])
    acc_scr = scratch[4 * nh:5 * nh]
    key_i = lax.broadcasted_iota(jnp.int32, (tk, tq), 0)
    qry_i = lax.broadcasted_iota(jnp.int32, (tk, tq), 1)
    meta_row = lax.broadcasted_iota(jnp.int32, (META_BLOCK, 1), 0)
    heads = [slice(hh * HEAD_SLOT, (hh + 1) * HEAD_SLOT) for hh in range(nh)]
    vals = [slice(hh * V_SLOT, (hh + 1) * V_SLOT) for hh in range(nh)]
    ndiag = tq // tk
    assert ndiag % 2 == 0, "slot parity is static only when a query block spans an even number of key blocks"

    def probs(st, m):
        return jnp.exp2((st - m).astype(BF16))

    def qblock(qi, _):
        q0 = pl.multiple_of(qi * tq, tq)
        qs = [qt_ref[qi, hs, :] for hs in heads]
        nfull = qi * ndiag
        last = nfull + ndiag - 1

        def scores_into(slot, kj):
            k0 = pl.multiple_of(kj * tk, tk)
            for hh in range(nh):
                s_scr[slot][hh][...] = _dot(k_ref[pl.ds(k0, tk), heads[hh]], qs[hh])

        ms = []
        for hh in range(nh):
            st = jnp.where(meta_row >= META_PAD, _dot(km_ref[:, heads[hh]], qs[hh]), NEG)
            m = jnp.max(st, axis=0, keepdims=True)
            ms.append(m)
            p_scr[1][hh][0:META_BLOCK, :] = probs(st, m)
            p_scr[1][hh][META_BLOCK:, :] = jnp.zeros((tk - META_BLOCK, tq), BF16)
            acc_scr[hh][...] = jnp.zeros((V_SLOT, tq), F32)
        scores_into(0, 0)

        def kvstep(kj, slot, carry, masked, look_ahead=True, maybe_first=False):
            ms, alphas = carry
            kprev = jnp.maximum(kj - 1, 0)
            pvs = []
            for hh in range(nh):
                vt_prev = vt_ref[kprev, vals[hh], :]
                if maybe_first:
                    vt_prev = jnp.where(kj == 0, vmt_ref[vals[hh], :], vt_prev)
                pvs.append(_dot(vt_prev, p_scr[1 - slot][hh][...]))
            if look_ahead:
                scores_into(1 - slot, kj + 1)
            out = ([], [])
            for hh in range(nh):
                st = s_scr[slot][hh][...]
                if masked:
                    st = jnp.where(kj * tk + key_i <= q0 + qry_i, st, NEG)
                m_new = jnp.maximum(ms[hh], jnp.max(st, axis=0, keepdims=True))
                p_scr[slot][hh][...] = probs(st, m_new)
                acc_scr[hh][...] = alphas[hh] * acc_scr[hh][...] + pvs[hh]
                out[0].append(m_new)
                out[1].append(jnp.exp2(ms[hh] - m_new))
            return tuple(tuple(x) for x in out)

        def run_steps(k_start, carry, n):
            for j in range(n):
                carry = kvstep(k_start + j, j % 2, carry, False, maybe_first=(j == 0))
            return carry

        ones = jnp.ones((1, tq), F32)
        carry = lax.fori_loop(0, nfull // 4, lambda t, c: run_steps(4 * t, c, 4), (tuple(ms), (ones,) * nh))
        carry = lax.cond(nfull % 4 == 2, lambda c: run_steps(nfull - 2, c, 2), lambda c: c, carry)
        for d in range(ndiag):
            carry = kvstep(nfull + d, d % 2, carry, True, look_ahead=d + 1 < ndiag, maybe_first=(d == 0))
        _, alphas = carry
        outs = []
        for hh in range(nh):
            acc = alphas[hh] * acc_scr[hh][...] + _dot(vt_ref[last, vals[hh], :], p_scr[(ndiag - 1) % 2][hh][...])
            outs.append(acc[:V_DIM] / acc[V_DIM:V_DIM + 1])
        o_ref[pl.ds(q0, tq), :] = jnp.concatenate(outs, axis=0).T.astype(BF16)
        return 0

    lax.fori_loop(0, seq // tq, qblock, 0)


def _flash(qt, k, vt, kmeta, vmeta_t):
    b, seq, _ = k.shape
    tq = qt.shape[-1]
    tk = vt.shape[-1]
    nh = FLASH_HEADS
    return pl.pallas_call(
        functools.partial(_flash_body, tq=tq, tk=tk, seq=seq),
        grid=(b, MLA_HEADS // nh),
        in_specs=[
            pl.BlockSpec((None, seq // tq, nh * HEAD_SLOT, tq), lambda i, p: (i, 0, p, 0)),
            pl.BlockSpec((None, seq, nh * HEAD_SLOT), lambda i, p: (i, 0, p)),
            pl.BlockSpec((None, seq // tk, nh * V_SLOT, tk), lambda i, p: (i, 0, p, 0)),
            pl.BlockSpec((META_BLOCK, nh * HEAD_SLOT), lambda i, p: (0, p)),
            pl.BlockSpec((nh * V_SLOT, tk), lambda i, p: (p, 0)),
        ],
        out_specs=pl.BlockSpec((None, seq, nh * V_DIM), lambda i, p: (i, 0, p)),
        out_shape=jax.ShapeDtypeStruct((b, seq, MLA_HEADS * V_DIM), BF16),
        scratch_shapes=([pltpu.VMEM((tk, tq), F32)] * (2 * nh) + [pltpu.VMEM((tk, tq), BF16)] * (2 * nh)
                        + [pltpu.VMEM((V_SLOT, tq), F32)] * nh),
        compiler_params=_cparams(("parallel", "parallel")),
        name="flash",
    )(qt, k, vt, kmeta, vmeta_t)


HALO = 16


def _expand2(xs, expand):
    q = xs[0].shape[0]
    parts = []
    for x in xs:
        hi = x.astype(BF16)
        parts += [hi, (x - hi.astype(F32)).astype(BF16)]
    out = _dot(jnp.concatenate(parts, axis=0), expand)
    return [out[2 * i * q:(2 * i + 1) * q] + out[(2 * i + 1) * q:(2 * i + 2) * q] for i in range(len(xs))]


def _ssd_body(xbc_ref, halo_ref, mh_ref, small_ref, sz_ref, cw_ref, cb_ref, dtb_ref, aneg_ref, dsk_ref,
              ng_ref, init_ref, o_ref, fin_ref, st_ref, y_ref, *, q, n_pad):
    c = pl.program_id(1)
    nc = pl.num_programs(1)

    @pl.when(c == 0)
    def _():
        st_ref[...] = init_ref[...]

    state = st_ref[...]
    halo = jnp.where(c == 0, mh_ref[...], halo_ref[...])
    for j in range(xbc_ref.shape[0] // q):
        rows = slice(j * q, (j + 1) * q)
        x_b = xbc_ref[rows, :]
        state = _ssd_chunk(x_b, halo, small_ref[rows, LANES:2 * LANES], sz_ref[rows, :], state,
                           cw_ref, cb_ref, dtb_ref, aneg_ref, dsk_ref, ng_ref, o_ref.at[rows, :], y_ref.at[rows, :],
                           q=q, n_pad=n_pad - j * q)
        halo = x_b[q - HALO:, :]
    st_ref[...] = state

    @pl.when(c == nc - 1)
    def _():
        fin_ref[...] = state


def _ssd_chunk(x_b, halo, dt_raw, sz, prev, cw_ref, cb_ref, dtb_ref, aneg_ref, dsk_ref, ng_ref, o_ref, y_ref,
               *, q, n_pad):
    x_ext = jnp.concatenate([halo, x_b], axis=0)
    sh_r = lax.broadcasted_iota(jnp.int32, (q, HALO + q), 0)
    sh_c = lax.broadcasted_iota(jnp.int32, (q, HALO + q), 1)
    xc = cb_ref[...] + cw_ref[CONV_WIDTH - 1:CONV_WIDTH, :] * x_b.astype(F32)
    shifts = jnp.concatenate([(sh_c == sh_r + (HALO - d)).astype(BF16) for d in range(1, CONV_WIDTH)], axis=0)
    shifted = _dot(shifts, x_ext)
    for d in range(1, CONV_WIDTH):
        xc = xc + cw_ref[CONV_WIDTH - 1 - d:CONV_WIDTH - d, :] * shifted[(d - 1) * q:d * q]
    xc = xc * _sigmoid(xc)

    lane = lax.broadcasted_iota(jnp.int32, (q, LANES), 1)
    dt = dt_raw + dtb_ref[...]
    dt = jnp.maximum(dt, 0.0) + jnp.log(1.0 + jnp.exp(-jnp.abs(dt)))
    dt = jnp.where((lane >= DT_LANE0) & (lane < DT_LANE0 + SSM_HEADS), dt, 0.0)
    if n_pad > 0:
        rowv = lax.broadcasted_iota(jnp.int32, (q, 1), 0) >= n_pad
        xc = jnp.where(rowv, xc, 0.0)
        dt = jnp.where(rowv, dt, 0.0)

    xs = xc[:, :D_INNER]
    bm = xc[:, D_INNER:D_INNER + LANES]
    cm = xc[:, D_INNER + LANES:]

    a = dt * aneg_ref[...]
    ri = lax.broadcasted_iota(jnp.int32, (q, q), 0)
    ci = lax.broadcasted_iota(jnp.int32, (q, q), 1)
    tril = ri >= ci
    a_cs = _dot_exact_lhs(tril.astype(BF16), a)
    a_cs_t = a_cs.T

    er = lax.broadcasted_iota(jnp.int32, (LANES, D_INNER), 0)
    ec = lax.broadcasted_iota(jnp.int32, (LANES, D_INNER), 1)
    expand = (er - DT_LANE0 == ec // SSM_HEAD_DIM).astype(BF16)
    dt_x, grow_x, rest_x = _expand2([dt, jnp.exp(a_cs), jnp.exp(a_cs[q - 1:q, :] - a_cs)], expand)

    xdt = xs * dt_x
    xdt_b = xdt.astype(BF16)
    xd_b = (xdt * rest_x).astype(BF16)

    bm_b = bm.astype(BF16)
    cm_b = cm.astype(BF16)
    lane_q = lax.broadcasted_iota(jnp.int32, (q, LANES), 1)

    for g in range(SSM_GROUPS):
        in_g = (lane_q >= g * D_STATE) & (lane_q < (g + 1) * D_STATE)
        cb = _dot_nt(jnp.where(in_g, cm, 0.0).astype(BF16), bm_b)
        for pr in range(SSM_HEADS // SSM_GROUPS // 2):
            pair = g * (SSM_HEADS // SSM_GROUPS // 2) + pr
            rhs = xdt_b[:, pair * LANES:(pair + 1) * LANES]
            ys = []
            for hh in range(2):
                e = DT_LANE0 + 2 * pair + hh
                seg = a_cs[:, e:e + 1] - a_cs_t[e:e + 1, :]
                lmat = jnp.exp(jnp.where(tril, seg, NEG))
                ys.append(_dot((cb * lmat).astype(BF16), rhs))
            y_ref[:, pair * LANES:(pair + 1) * LANES] = jnp.where(lane_q < SSM_HEAD_DIM, ys[0], ys[1])

    sr = lax.broadcasted_iota(jnp.int32, (LANES, D_INNER), 0)
    sc = lax.broadcasted_iota(jnp.int32, (LANES, D_INNER), 1)
    same_group = (sr // D_STATE) == (sc // (D_INNER // SSM_GROUPS))
    y_off = _dot(cm_b, prev.astype(BF16)) * grow_x
    st_new = prev * grow_x[q - 1:q, :] + jnp.where(same_group, _dot_tn(bm_b, xd_b), 0.0)

    y = y_ref[...] + y_off + xs * dsk_ref[...]
    yz = y * sz.astype(F32)
    half = D_INNER // SSM_GROUPS
    outs = []
    for g in range(SSM_GROUPS):
        part = yz[:, g * half:(g + 1) * half]
        outs.append(part * lax.rsqrt(jnp.mean(part * part, axis=-1, keepdims=True) + EPS))
    o_ref[...] = (jnp.concatenate(outs, axis=1) * ng_ref[...]).astype(BF16)
    return st_new


def _ssd(xbc, small, sz, meta_halo, init_state, cw, cb, dtb, aneg, dsk, ng, blk, q, n_pad):
    b, seq, _ = xbc.shape
    nc = seq // blk
    hb = blk // HALO
    full = lambda a: pl.BlockSpec(a.shape, lambda i, c: (0,) * a.ndim)
    return pl.pallas_call(
        functools.partial(_ssd_body, q=q, n_pad=n_pad),
        grid=(b, nc),
        in_specs=[
            pl.BlockSpec((None, blk, CONV_DIM), lambda i, c: (i, c, 0)),
            pl.BlockSpec((None, HALO, CONV_DIM), lambda i, c: (i, jnp.maximum(c * hb - 1, 0), 0)),
            full(meta_halo),
            pl.BlockSpec((None, blk, A_SMALL), lambda i, c: (i, c, 0)),
            pl.BlockSpec((None, blk, D_INNER), lambda i, c: (i, c, 0)),
            full(cw), full(cb), full(dtb), full(aneg), full(dsk), full(ng), full(init_state),
        ],
        out_specs=[
            pl.BlockSpec((None, blk, D_INNER), lambda i, c: (i, c, 0)),
            pl.BlockSpec((None, LANES, D_INNER), lambda i, c: (i, 0, 0)),
        ],
        out_shape=[jax.ShapeDtypeStruct((b, seq, D_INNER), BF16),
                   jax.ShapeDtypeStruct((b, LANES, D_INNER), F32)],
        scratch_shapes=[pltpu.VMEM((LANES, D_INNER), F32),
                        pltpu.VMEM((blk, D_INNER), F32)],
        compiler_params=_cparams(("parallel", "arbitrary")),
        name="ssd",
    )(xbc, xbc, meta_halo, small, sz, cw, cb, dtb, aneg, dsk, ng, init_state)


def _route(logits):
    lane = lax.broadcasted_iota(jnp.int32, logits.shape, 1).astype(F32)
    gl = jnp.where(lane < N_EXPERT_GROUPS, logits, NEG)
    gmax = jnp.max(gl, axis=1, keepdims=True)
    gidx = jnp.min(jnp.where(gl == gmax, lane, float(LANES)), axis=1, keepdims=True)
    p_g = 1.0 / jnp.sum(jnp.exp(gl - gmax), axis=1, keepdims=True)
    lo = EXPERT_LANE0 + EXPERTS_PER_GROUP * gidx
    el = jnp.where((lane >= lo) & (lane < lo + EXPERTS_PER_GROUP), logits, NEG)
    t1 = jnp.max(el, axis=1, keepdims=True)
    i1 = jnp.min(jnp.where(el == t1, lane, float(LANES)), axis=1, keepdims=True)
    el2 = jnp.where(lane == i1, NEG, el)
    t2 = jnp.max(el2, axis=1, keepdims=True)
    i2 = jnp.min(jnp.where(el2 == t2, lane, float(LANES)), axis=1, keepdims=True)
    e21 = jnp.exp(t2 - t1)
    w1 = p_g / (1.0 + e21)
    w2 = w1 * e21
    return lane, i1, i2, w1, w2


RT_E1, RT_E2, RT_R1, RT_R2, RT_W1, RT_W2 = range(6)


def _merge_body(x_ref, oa_ref, os_ref, ga_ref, gs_ref, wa_ref, ws_ref, wo_ref, nf_ref, wr_ref, br_ref,
                h2_ref, v_ref, rt_ref, cnt_out_ref, cnt_ref):
    @pl.when(pl.program_id(0) == 0)
    def _():
        cnt_ref[...] = jnp.zeros_like(cnt_ref)

    merged = (ga_ref[...].astype(F32) * _dot(oa_ref[...], wa_ref[...])
              + gs_ref[...].astype(F32) * _dot(os_ref[...], ws_ref[...]))
    h2 = x_ref[...] + _dot(merged.astype(BF16), wo_ref[...])
    h2_ref[...] = h2
    v = _rms(h2, nf_ref[...])
    _store_token_slabs(v_ref, v)
    lane, i1, i2, w1, w2 = _route(_dot(v.astype(BF16), wr_ref[...]) + br_ref[...])

    tm = lane.shape[0]
    chosen = jnp.where((lane == i1) | (lane == i2), 1.0, 0.0)
    ri = lax.broadcasted_iota(jnp.int32, (tm, tm), 0)
    ci = lax.broadcasted_iota(jnp.int32, (tm, tm), 1)
    earlier = _dot((ri > ci).astype(BF16), chosen.astype(BF16)) + cnt_ref[...]
    r1 = jnp.sum(jnp.where(lane == i1, earlier, 0.0), axis=1, keepdims=True)
    r2 = jnp.sum(jnp.where(lane == i2, earlier, 0.0), axis=1, keepdims=True)
    cnt_ref[...] += jnp.sum(chosen, axis=0, keepdims=True)
    cnt_out_ref[...] = jnp.broadcast_to(cnt_ref[...], cnt_out_ref.shape)

    rec = jnp.zeros_like(lane)
    for col, val in ((RT_E1, i1 - EXPERT_LANE0), (RT_E2, i2 - EXPERT_LANE0), (RT_R1, r1), (RT_R2, r2),
                     (RT_W1, w1), (RT_W2, w2)):
        rec = jnp.where(lane == col, val, rec)
    rt_ref[...] = rec


def _merge(x2d, oa, osm, ga, gs, wa, ws, wo, nf, wr, br, tm):
    r = x2d.shape[0]
    row = lambda n: pl.BlockSpec((tm, n), lambda i: (i, 0))
    full = lambda a: pl.BlockSpec(a.shape, lambda i: (0, 0))
    return pl.pallas_call(
        _merge_body,
        grid=(r // tm,),
        in_specs=[row(D_MODEL)] * 5 + [full(wa), full(ws), full(wo), full(nf), full(wr), full(br)],
        out_specs=[row(D_MODEL), pl.BlockSpec((tm * SLAB, LANES), lambda i: (i, 0)), row(LANES),
                   pl.BlockSpec((8, LANES), lambda i: (0, 0))],
        out_shape=[jax.ShapeDtypeStruct((r, D_MODEL), F32),
                   jax.ShapeDtypeStruct((r * SLAB, LANES), F32),
                   jax.ShapeDtypeStruct((r, LANES), F32),
                   jax.ShapeDtypeStruct((8, LANES), F32)],
        scratch_shapes=[pltpu.VMEM((1, LANES), F32)],
        compiler_params=_cparams(("arbitrary",)),
        name="merge",
    )(x2d, oa, osm, ga, gs, wa, ws, wo, nf, wr, br)


SLAB = D_MODEL // LANES


def _store_token_slabs(ref, x):
    tm = x.shape[0]
    for s in range(SLAB):
        ref[pl.ds(s, tm, stride=SLAB), :] = x[:, s * LANES:(s + 1) * LANES]


def _load_token_slabs(ref):
    tm = ref.shape[0] // SLAB
    return jnp.concatenate([ref[pl.ds(s, tm, stride=SLAB), :] for s in range(SLAB)], axis=1)


def _row_copy(src_ref, src_row, dst_ref, dst_row, sem):
    src = src_ref.at[pl.ds(pl.multiple_of(src_row * SLAB, SLAB), SLAB)]
    dst = dst_ref.at[pl.ds(pl.multiple_of(dst_row * SLAB, SLAB), SLAB)]
    return pltpu.make_async_copy(src, dst, sem)


def _dispatch_body(pos_ref, v_ref, xs_ref, sem):
    tm = v_ref.shape[0] // SLAB

    def issue(t, _):
        _row_copy(v_ref, t, xs_ref, pos_ref[0, t], sem).start(priority=0)
        _row_copy(v_ref, t, xs_ref, pos_ref[1, t], sem).start(priority=1)
        return 0

    lax.fori_loop(0, tm, issue, 0, unroll=8)
    for _ in range(2):
        pltpu.make_async_copy(v_ref, xs_ref.at[pl.ds(0, tm * SLAB)], sem).wait()


def _slot_spec(tm):
    return pl.BlockSpec((8, tm), lambda i: (0, i), memory_space=pltpu.SMEM)


def _dispatch(pos, v, tm):
    r = v.shape[0] // SLAB
    return pl.pallas_call(
        _dispatch_body,
        grid=(r // tm,),
        in_specs=[_slot_spec(tm), pl.BlockSpec((tm * SLAB, LANES), lambda i: (i, 0))],
        out_specs=pl.BlockSpec(memory_space=pl.ANY),
        out_shape=jax.ShapeDtypeStruct((2 * r * SLAB, LANES), F32),
        scratch_shapes=[pltpu.SemaphoreType.DMA],
        compiler_params=_cparams(("arbitrary",)),
        name="moe_dispatch",
    )(pos, v)


def _experts_body(tile_ref, exp_ref, off_ref, nwork_ref, xs_ref, wg_ref, wu_ref, wd_ref, ys_ref,
                  wg_s, wu_s, wd_s):
    w = pl.program_id(0)
    tme = xs_ref.shape[0] // SLAB

    @pl.when(w < nwork_ref[0])
    def _():
        e = exp_ref[w]
        tile = tile_ref[w]
        prev = jnp.maximum(w - 1, 0)
        new_expert = (w == 0) | (exp_ref[prev] != e)
        new_tile = (w == 0) | (tile_ref[prev] != tile)

        @pl.when(new_expert)
        def _():
            wg_s[...] = wg_ref[...].astype(BF16)
            wu_s[...] = wu_ref[...].astype(BF16)
            wd_s[...] = wd_ref[...].astype(BF16)

        x = _load_token_slabs(xs_ref).astype(BF16)
        g = _dot(x, wg_s[...])
        u = _dot(x, wu_s[...])
        y = _dot((g * _sigmoid(g) * u).astype(BF16), wd_s[...])
        rows = tile * tme + lax.broadcasted_iota(jnp.int32, (tme, 1), 0)
        mine = (rows >= off_ref[e]) & (rows < off_ref[e + 1])

        @pl.when(new_tile)
        def _():
            _store_token_slabs(ys_ref, jnp.where(mine, y, 0.0))

        @pl.when(jnp.logical_not(new_tile))
        def _():
            _store_token_slabs(ys_ref, jnp.where(mine, y, _load_token_slabs(ys_ref)))


def _experts(tile_of, exp_of, offs, nwork, xs, wg, wu, wd, tme):
    nw = tile_of.shape[0]
    grid_spec = pltpu.PrefetchScalarGridSpec(
        num_scalar_prefetch=4,
        grid=(nw,),
        in_specs=[
            pl.BlockSpec((tme * SLAB, LANES), lambda w, t, e, o, n: (t[w], 0)),
            pl.BlockSpec((None, D_MODEL, D_EXPERT), lambda w, t, e, o, n: (e[w], 0, 0)),
            pl.BlockSpec((None, D_MODEL, D_EXPERT), lambda w, t, e, o, n: (e[w], 0, 0)),
            pl.BlockSpec((None, D_EXPERT, D_MODEL), lambda w, t, e, o, n: (e[w], 0, 0)),
        ],
        out_specs=pl.BlockSpec((tme * SLAB, LANES), lambda w, t, e, o, n: (t[w], 0)),
        scratch_shapes=[pltpu.VMEM((D_MODEL, D_EXPERT), BF16), pltpu.VMEM((D_MODEL, D_EXPERT), BF16),
                        pltpu.VMEM((D_EXPERT, D_MODEL), BF16)],
    )
    return pl.pallas_call(
        _experts_body,
        grid_spec=grid_spec,
        out_shape=jax.ShapeDtypeStruct(xs.shape, F32),
        compiler_params=_cparams(("arbitrary",)),
        name="moe_experts",
    )(tile_of, exp_of, offs, nwork, xs, wg, wu, wd)


def _combine_body(pos_ref, pos_next_ref, rt_ref, h2_ref, nf_ref, ys_ref, out_ref, y1_ref, y2_ref, sem):
    i = pl.program_id(0)
    n = pl.num_programs(0)
    tm = h2_ref.shape[0]
    slot = i % 2

    def issue_tile(p_ref, s):
        def issue(t, _):
            _row_copy(ys_ref, p_ref[0, t], y1_ref.at[s], t, sem.at[s]).start(priority=0)
            _row_copy(ys_ref, p_ref[1, t], y2_ref.at[s], t, sem.at[s]).start(priority=1)
            return 0

        lax.fori_loop(0, tm, issue, 0, unroll=8)

    @pl.when(i == 0)
    def _():
        issue_tile(pos_ref, 0)

    @pl.when(i + 1 < n)
    def _():
        issue_tile(pos_next_ref, 1 - slot)

    for buf in (y1_ref, y2_ref):
        pltpu.make_async_copy(ys_ref.at[pl.ds(0, tm * SLAB)], buf.at[slot], sem.at[slot]).wait()
    lane = lax.broadcasted_iota(jnp.int32, rt_ref.shape, 1)
    rt = rt_ref[...]
    w1 = jnp.sum(jnp.where(lane == RT_W1, rt, 0.0), axis=1, keepdims=True)
    w2 = jnp.sum(jnp.where(lane == RT_W2, rt, 0.0), axis=1, keepdims=True)
    moe = w1 * _load_token_slabs(y1_ref.at[slot]) + w2 * _load_token_slabs(y2_ref.at[slot])
    out_ref[...] = _rms(h2_ref[...] + moe, nf_ref[...])


def _combine(pos, rt, h2, nf, ys, tm):
    r = h2.shape[0]
    last = r // tm - 1
    next_slots = pl.BlockSpec((8, tm), lambda i: (0, jnp.minimum(i + 1, last)), memory_space=pltpu.SMEM)
    return pl.pallas_call(
        _combine_body,
        grid=(r // tm,),
        in_specs=[_slot_spec(tm), next_slots, pl.BlockSpec((tm, LANES), lambda i: (i, 0)),
                  pl.BlockSpec((tm, D_MODEL), lambda i: (i, 0)), pl.BlockSpec((1, D_MODEL), lambda i: (0, 0)),
                  pl.BlockSpec(memory_space=pl.ANY)],
        out_specs=pl.BlockSpec((tm, D_MODEL), lambda i: (i, 0)),
        out_shape=jax.ShapeDtypeStruct((r, D_MODEL), F32),
        scratch_shapes=[pltpu.VMEM((2, tm * SLAB, LANES), F32), pltpu.VMEM((2, tm * SLAB, LANES), F32),
                        pltpu.SemaphoreType.DMA((2,))],
        compiler_params=_cparams(("arbitrary",)),
        name="moe_combine",
    )(pos, pos, rt, h2, nf, ys)


def _slots_body(rt_ref, cnt_ref, pos_ref):
    rt = rt_ref[...]
    lane = lax.broadcasted_iota(jnp.int32, rt.shape, 1).astype(F32)
    col = lambda c: jnp.sum(jnp.where(lane == c, rt, 0.0), axis=1, keepdims=True)
    ri = lax.broadcasted_iota(jnp.int32, (LANES, LANES), 0)
    ci = lax.broadcasted_iota(jnp.int32, (LANES, LANES), 1)
    first = _dot_exact_rhs(cnt_ref[...], (ri < ci).astype(BF16))[0:1, :]
    slot = lambda e, r: col(r) + jnp.sum(jnp.where(lane == col(e) + EXPERT_LANE0, first, 0.0),
                                         axis=1, keepdims=True)
    rec = jnp.where(lane == 0, slot(RT_E1, RT_R1), jnp.where(lane == 1, slot(RT_E2, RT_R2), 0.0))
    sr = lax.broadcasted_iota(jnp.int32, (8, LANES), 0)
    sc = lax.broadcasted_iota(jnp.int32, (8, LANES), 1)
    sel = (sr == sc).astype(BF16)
    pos_ref[...] = sum(_dot_nt(sel, part) for part in _split3(rec)).astype(jnp.int32)


def _slots(rt, counts, tm):
    r = rt.shape[0]
    return pl.pallas_call(
        _slots_body,
        grid=(r // tm,),
        in_specs=[pl.BlockSpec((tm, LANES), lambda i: (i, 0)), pl.BlockSpec((8, LANES), lambda i: (0, 0))],
        out_specs=pl.BlockSpec((8, tm), lambda i: (0, i)),
        out_shape=jax.ShapeDtypeStruct((8, r), jnp.int32),
        compiler_params=_cparams(("parallel",)),
        name="moe_slots",
    )(rt, counts)


def _moe_plan(counts, rows, tme):
    cnt = counts.astype(jnp.int32)
    ends = jnp.cumsum(cnt)
    offs = jnp.concatenate([jnp.zeros((1,), jnp.int32), ends])
    first_tile = offs[:-1] // tme
    last_tile = (ends - 1) // tme
    n_items = jnp.where(cnt > 0, last_tile - first_tile + 1, 0)
    item_end = jnp.cumsum(n_items)
    nwork = item_end[-1]
    nw = 2 * rows // tme + N_EXPERTS - 1
    w = jnp.arange(nw, dtype=jnp.int32)
    wc = jnp.minimum(w, nwork - 1)
    exp_of = jnp.sum((wc[:, None] >= item_end[None, :]).astype(jnp.int32), axis=1)
    onehot = (exp_of[:, None] == jnp.arange(N_EXPERTS, dtype=jnp.int32)[None, :]).astype(jnp.int32)
    item_start = jnp.sum(onehot * (item_end - n_items)[None, :], axis=1)
    tile_of = (jnp.sum(onehot * first_tile[None, :], axis=1) + (wc - item_start)).astype(jnp.int32)
    return tile_of, exp_of, offs, nwork.reshape(1).astype(jnp.int32)


def _prep_w_in(w):
    o = np.cumsum([0, Q_RANK, KV_RANK, ROPE_DIM, D_INNER, CONV_DIM, SSM_HEADS, D_MODEL, D_MODEL])
    cq, ckv, kr, z, xbc, dt, ga, gs = (w[:, o[i]:o[i + 1]] for i in range(8))
    hr = ROPE_DIM // 2
    zeros = lambda n: jnp.zeros((w.shape[0], n), w.dtype)
    kra = jnp.concatenate([zeros(NOPE_DIM), kr, zeros(HEAD_SLOT - NOPE_DIM - ROPE_DIM)], axis=1)
    krb = jnp.concatenate([zeros(NOPE_DIM), -kr[:, hr:], kr[:, :hr], dt,
                           zeros(HEAD_SLOT - DT_LANE0 - SSM_HEADS)], axis=1)
    return jnp.concatenate([cq, ckv, kra, krb, z, xbc, ga, gs], axis=1).astype(BF16)


def _prep_w_uq(w):
    w = w.reshape(Q_RANK, MLA_HEADS, NOPE_DIM + ROPE_DIM)
    hr = ROPE_DIM // 2
    nope, r1, r2 = w[..., :NOPE_DIM], w[..., NOPE_DIM:NOPE_DIM + hr], w[..., NOPE_DIM + hr:]
    z = jnp.zeros((Q_RANK, MLA_HEADS, HEAD_SLOT - NOPE_DIM - ROPE_DIM), w.dtype)
    wa = jnp.concatenate([nope, r1, r2, z], axis=-1).reshape(Q_RANK, -1)
    return wa.T.astype(BF16)


def _prep_w_ukv(w):
    w = w.reshape(KV_RANK, MLA_HEADS, NOPE_DIM + V_DIM)
    kn, v = w[..., :NOPE_DIM], w[..., NOPE_DIM:]
    wk = jnp.concatenate([kn, jnp.zeros((KV_RANK, MLA_HEADS, HEAD_SLOT - NOPE_DIM), w.dtype)], axis=-1)
    vslot = jnp.concatenate([v, jnp.zeros((KV_RANK, MLA_HEADS, V_SLOT - V_DIM), w.dtype)], axis=-1)
    return wk.reshape(KV_RANK, -1).astype(BF16), vslot.reshape(KV_RANK, -1).T.astype(BF16)


def _rope_tables(pos):
    inv = ROPE_THETA ** (-jnp.arange(0, ROPE_DIM, 2, dtype=F32) / ROPE_DIM)
    ang = pos.astype(F32)[:, None] * inv[None, :]
    cos, sin = jnp.cos(ang), jnp.sin(ang)
    n = pos.shape[0]
    pad = jnp.zeros((n, HEAD_SLOT - NOPE_DIM - ROPE_DIM), F32)
    scale = (NOPE_DIM + ROPE_DIM) ** -0.5 * float(np.log2(np.e))
    cq = jnp.concatenate([jnp.ones((n, NOPE_DIM), F32), cos, cos, pad], axis=1) * scale
    sq = jnp.concatenate([jnp.zeros((n, NOPE_DIM), F32), sin, sin, pad], axis=1) * scale
    ck = jnp.concatenate([jnp.zeros((n, NOPE_DIM), F32), cos, cos, pad], axis=1)
    sk = jnp.concatenate([jnp.zeros((n, NOPE_DIM), F32), sin, sin, pad], axis=1)
    return cq.T, sq.T, ck, sk


def _head_lanes(v):
    return jnp.zeros((1, LANES), F32).at[0, DT_LANE0:DT_LANE0 + SSM_HEADS].set(v.astype(F32))


def _pick(n, prefs):
    for t in prefs:
        if n % t == 0:
            return t
    raise ValueError(f"no tile for {n}")


def kernel(x, meta_tokens, norm_mix, w_in, mla_q_norm, mla_w_uq, mla_kv_norm, mla_w_ukv, ssm_conv_w, ssm_conv_b, ssm_dt_bias, ssm_a_log, ssm_d_skip, ssm_norm, w_branch_attn, w_branch_ssm, w_out, norm_ffn, moe_w_group, moe_b_group, moe_w_expert, moe_b_expert, moe_w_gate, moe_w_up, moe_w_down, norm_final):
    b, seq, _ = x.shape
    assert w_in.shape[0] == 1, "one layer"
    rows = b * seq
    x2d = x.reshape(rows, D_MODEL)
    tm = _pick(seq, (512, 256, 128))
    tq = _pick(seq, (512, 256, 128))
    tk = _pick(tq, (256, 128))
    chunk = 128
    ssd_blk = _pick(seq, (512, 256, 128))

    w_in_r = _prep_w_in(w_in[0])
    wq = _prep_w_uq(mla_w_uq[0])
    wk, wvt = _prep_w_ukv(mla_w_ukv[0])
    g_mix = norm_mix[0].reshape(1, D_MODEL)
    qn = mla_q_norm[0].reshape(1, Q_RANK)
    kvn = mla_kv_norm[0].reshape(1, KV_RANK)
    cw = ssm_conv_w[0]
    cb = ssm_conv_b[0].reshape(1, CONV_DIM)
    dtb = _head_lanes(ssm_dt_bias[0])
    aneg = _head_lanes(-jnp.exp(ssm_a_log[0].astype(F32)))
    dsk = jnp.repeat(ssm_d_skip[0].astype(F32), SSM_HEAD_DIM).reshape(1, D_INNER)
    ng = ssm_norm[0].reshape(1, D_INNER)
    wa = w_branch_attn[0].astype(BF16)
    ws = w_branch_ssm[0].astype(BF16)
    wo = w_out[0].astype(BF16)
    nffn = norm_ffn[0].reshape(1, D_MODEL)
    wr = jnp.zeros((D_MODEL, LANES), F32)
    wr = wr.at[:, :N_EXPERT_GROUPS].set(moe_w_group[0]).at[:, EXPERT_LANE0:EXPERT_LANE0 + N_EXPERTS].set(moe_w_expert[0])
    wr = wr.astype(BF16)
    br = jnp.zeros((1, LANES), F32)
    br = br.at[0, :N_EXPERT_GROUPS].set(moe_b_group[0]).at[0, EXPERT_LANE0:EXPERT_LANE0 + N_EXPERTS].set(moe_b_expert[0])
    wg, wu, wd = moe_w_gate[0], moe_w_up[0], moe_w_down[0]
    nfin = norm_final.reshape(1, D_MODEL)

    meta_blk = jnp.concatenate([jnp.zeros((META_PAD, D_MODEL), F32), meta_tokens.astype(F32)], axis=0)
    m_cq, m_ckv, m_small, m_sz, m_xbc, _, _ = _inproj(meta_blk, g_mix, w_in_r, META_BLOCK)
    meta_pos = jnp.maximum(jnp.arange(META_BLOCK) - META_PAD, 0)
    _, kmeta, vmeta_t = _mla_prep(m_cq, m_ckv, m_small, qn, kvn, wq, wk, wvt, _rope_tables(meta_pos),
                                  META_BLOCK, 1, META_BLOCK, META_BLOCK)
    zero_state = jnp.zeros((LANES, D_INNER), F32)
    zero_halo = jnp.zeros((HALO, CONV_DIM), BF16)
    _, meta_state = _ssd(m_xbc[None], m_small[None], m_sz[None], zero_halo, zero_state,
                         cw, cb, dtb, aneg, dsk, ng, META_BLOCK, META_BLOCK, META_PAD)
    meta_halo = m_xbc[META_BLOCK - HALO:]

    cq, ckv, small, sz, xbc, ga, gs = _inproj(x2d, g_mix, w_in_r, tm)
    qt, k, vt = _mla_prep(cq, ckv, small, qn, kvn, wq, wk, wvt, _rope_tables(N_META + jnp.arange(seq)),
                          tm, seq // tm, tq, tk)
    o_attn = _flash(qt.reshape(b, seq // tq, -1, tq), k.reshape(b, seq, -1), vt.reshape(b, seq // tk, -1, tk),
                    kmeta, jnp.pad(vmeta_t[0], ((0, 0), (0, tk - META_BLOCK))))
    o_ssm, _ = _ssd(xbc.reshape(b, seq, -1), small.reshape(b, seq, -1), sz.reshape(b, seq, -1), meta_halo,
                    meta_state[0], cw, cb, dtb, aneg, dsk, ng, ssd_blk, chunk, 0)
    h2, v, rt, counts = _merge(x2d, o_attn.reshape(rows, -1), o_ssm.reshape(rows, -1), ga, gs, wa, ws, wo,
                               nffn, wr, br, tm)
    tme = _pick(2 * rows, (512, 256, 128))
    tile_of, exp_of, offs, nwork = _moe_plan(counts[0, EXPERT_LANE0:EXPERT_LANE0 + N_EXPERTS], rows, tme)
    pos = _slots(rt, counts, tm)
    xs = _dispatch(pos, v, tm)
    ys = _experts(tile_of, exp_of, offs, nwork, xs, wg, wu, wd, tme)
    out = _combine(pos, rt, h2, nfin, ys, tm)
    return out.reshape(b, seq, D_MODEL)
```

```python
import functools

import numpy as np
import jax
import jax.numpy as jnp
from jax import lax
from jax.experimental import pallas as pl
from jax.experimental.pallas import tpu as pltpu

F32 = jnp.float32
BF16 = jnp.bfloat16

D_MODEL = 1024
N_META = 16
EPS = 1e-6
NEG = -1e30
MLA_HEADS = 16
Q_RANK = 256
KV_RANK = 128
NOPE_DIM = 64
ROPE_DIM = 32
V_DIM = 64
ROPE_THETA = 10000.0
SSM_HEADS = 16
SSM_HEAD_DIM = 64
D_INNER = SSM_HEADS * SSM_HEAD_DIM
SSM_GROUPS = 2
D_STATE = 64
CONV_WIDTH = 4
CONV_DIM = D_INNER + 2 * SSM_GROUPS * D_STATE
N_EXPERT_GROUPS = 4
EXPERTS_PER_GROUP = 8
N_EXPERTS = N_EXPERT_GROUPS * EXPERTS_PER_GROUP
D_EXPERT = 256

LANES = 128
META_BLOCK = 128
META_PAD = META_BLOCK - N_META
HEAD_SLOT = 128
V_SLOT = 80
FLASH_HEADS = 4
EXPERT_LANE0 = 32
DT_LANE0 = 96
VMEM_LIMIT = 48 * 1024 * 1024

A_CQ, A_CKV, A_SMALL, A_Z, A_XBC, A_GA, A_GS = 256, 128, 256, D_INNER, CONV_DIM, D_MODEL, D_MODEL


def _cparams(sem):
    return pltpu.CompilerParams(dimension_semantics=sem, vmem_limit_bytes=VMEM_LIMIT)


def _rms(x, g):
    return x * lax.rsqrt(jnp.mean(x * x, axis=-1, keepdims=True) + EPS) * g


def _sigmoid(x):
    return 1.0 / (1.0 + jnp.exp(-x))


def _dot(a, b):
    return jnp.dot(a, b, preferred_element_type=F32)


def _dot_nt(a, b):
    return lax.dot_general(a, b, (((1,), (1,)), ((), ())), preferred_element_type=F32)


def _dot_tn(a, b):
    return lax.dot_general(a, b, (((0,), (0,)), ((), ())), preferred_element_type=F32)


def _split3(x):
    hi = x.astype(BF16)
    r1 = x - hi.astype(F32)
    mid = r1.astype(BF16)
    lo = (r1 - mid.astype(F32)).astype(BF16)
    return hi, mid, lo


def _dot_exact_rhs(x, m):
    hi, mid, lo = _split3(x)
    return _dot(hi, m) + _dot(mid, m) + _dot(lo, m)


def _dot_exact_lhs(m, x):
    hi, mid, lo = _split3(x)
    return _dot(m, hi) + _dot(m, mid) + _dot(m, lo)


def _inproj_body(x_ref, g_ref, *refs):
    w_refs = refs[:7]
    cq_ref, ckv_ref, small_ref, sz_ref, xbc_ref, ga_ref, gs_ref = refs[7:]
    u = _rms(x_ref[...], g_ref[...]).astype(BF16)

    def mm(i):
        return _dot(u, w_refs[i][...])

    cq_ref[...] = mm(0).astype(BF16)
    ckv_ref[...] = mm(1).astype(BF16)
    small_ref[...] = mm(2)
    z = mm(3)
    sz_ref[...] = (z * _sigmoid(z)).astype(BF16)
    xbc_ref[...] = mm(4).astype(BF16)
    ga_ref[...] = _sigmoid(mm(5)).astype(BF16)
    gs_ref[...] = _sigmoid(mm(6)).astype(BF16)


def _inproj(x2d, g, ws, tm):
    r = x2d.shape[0]
    widths = (A_CQ, A_CKV, A_SMALL, A_Z, A_XBC, A_GA, A_GS)
    dtypes = (BF16, BF16, F32, BF16, BF16, BF16, BF16)
    assert tuple(w.shape[1] for w in ws) == widths
    return pl.pallas_call(
        _inproj_body,
        grid=(r // tm,),
        in_specs=[pl.BlockSpec((tm, D_MODEL), lambda i: (i, 0)), pl.BlockSpec((1, D_MODEL), lambda i: (0, 0))]
        + [pl.BlockSpec(w.shape, lambda i: (0, 0), pipeline_mode=pl.Buffered(1)) for w in ws],
        out_specs=[pl.BlockSpec((tm, n), lambda i: (i, 0)) for n in widths],
        out_shape=[jax.ShapeDtypeStruct((r, n), dt) for n, dt in zip(widths, dtypes)],
        compiler_params=_cparams(("parallel",)),
        name="inproj",
    )(x2d, g, *ws)


def _mla_prep_body(cq_ref, ckv_ref, small_ref, qn_ref, kvn_ref, wqt_ref, wk_ref, wvt_ref, vone_ref,
                   cqt_ref, sqt_ref, ckt_ref, skt_ref, qt_ref, k_ref, vt_ref):
    cn = _rms(cq_ref[...].astype(F32), qn_ref[...]).astype(BF16)
    kn = _rms(ckv_ref[...].astype(F32), kvn_ref[...]).astype(BF16)
    cqt = cqt_ref[...]
    sqt = sqt_ref[...]
    all_heads = MLA_HEADS * HEAD_SLOT
    part = all_heads // 2
    tqb = qt_ref.shape[-1]
    r1, r2, r3 = NOPE_DIM, NOPE_DIM + ROPE_DIM // 2, NOPE_DIM + ROPE_DIM
    for lo in range(0, all_heads, part):
        qa = _dot_nt(wqt_ref[lo:lo + part, :], cn)
        for r0 in range(0, part, HEAD_SLOT):
            x = qa[r0:r0 + HEAD_SLOT]
            rot = jnp.concatenate([x[:r1], -x[r2:r3], x[r1:r2], x[r3:]], axis=0)
            qh = (x * cqt + rot * sqt).astype(BF16)
            for j in range(qt_ref.shape[0]):
                qt_ref[j, lo + r0:lo + r0 + HEAD_SLOT, :] = qh[:, j * tqb:(j + 1) * tqb]
    krp = small_ref[:, 0:LANES] * ckt_ref[...] + small_ref[:, LANES:2 * LANES] * skt_ref[...]
    krp2 = jnp.concatenate([krp, krp], axis=1)
    for hp in range(MLA_HEADS // 2):
        lo, hi = hp * 2 * HEAD_SLOT, (hp + 1) * 2 * HEAD_SLOT
        k_ref[:, lo:hi] = (_dot(kn, wk_ref[:, lo:hi]) + krp2).astype(BF16)
    vt = _dot_nt(wvt_ref[...], kn) + vone_ref[...]
    tkv = vt_ref.shape[-1]
    for j in range(vt_ref.shape[0]):
        vt_ref[j] = vt[:, j * tkv:(j + 1) * tkv].astype(BF16)


def _mla_prep(cq, ckv, small, qn, kvn, wqt, wk, wvt, tabs, tm, seq_blocks, tqb, tkv):
    r = cq.shape[0]
    nq = MLA_HEADS * HEAD_SLOT
    nv = MLA_HEADS * V_SLOT
    cqt, sqt, ckt, skt = tabs
    ones_row = (jnp.arange(nv) % V_SLOT == V_DIM).astype(F32)
    vone = jnp.broadcast_to(ones_row[:, None], (nv, tm))
    row = lambda n: pl.BlockSpec((tm, n), lambda i: (i, 0))
    tab = lambda n: pl.BlockSpec((tm, n), lambda i: (i % seq_blocks, 0))
    tab_t = pl.BlockSpec((HEAD_SLOT, tm), lambda i: (0, i % seq_blocks))
    full = lambda a: pl.BlockSpec(a.shape, lambda i: (0, 0))
    return pl.pallas_call(
        _mla_prep_body,
        grid=(r // tm,),
        in_specs=[row(Q_RANK), row(KV_RANK), row(A_SMALL), full(qn), full(kvn), full(wqt), full(wk), full(wvt),
                  full(vone), tab_t, tab_t, tab(HEAD_SLOT), tab(HEAD_SLOT)],
        out_specs=[pl.BlockSpec((tm // tqb, nq, tqb), lambda i: (i, 0, 0)), row(nq),
                   pl.BlockSpec((tm // tkv, nv, tkv), lambda i: (i, 0, 0))],
        out_shape=[jax.ShapeDtypeStruct((r // tqb, nq, tqb), BF16),
                   jax.ShapeDtypeStruct((r, nq), BF16),
                   jax.ShapeDtypeStruct((r // tkv, nv, tkv), BF16)],
        compiler_params=_cparams(("parallel",)),
        name="mla_prep",
    )(cq, ckv, small, qn, kvn, wqt, wk, wvt, vone, cqt, sqt, ckt, skt)


def _flash_body(qt_ref, k_ref, vt_ref, km_ref, vmt_ref, o_ref, *scratch, tq, tk, seq):
    nh = FLASH_HEADS
    s_scr = (scratch[0:nh], scratch[nh:2 * nh])
    p_scr = (scratch[2 * nh:3 * nh], scratch[3 * nh:4 * nh])
    acc_scr = scratch[4 * nh:5 * nh]
    key_i = lax.broadcasted_iota(jnp.int32, (tk, tq), 0)
    qry_i = lax.broadcasted_iota(jnp.int32, (tk, tq), 1)
    meta_row = lax.broadcasted_iota(jnp.int32, (META_BLOCK, 1), 0)
    heads = [slice(hh * HEAD_SLOT, (hh + 1) * HEAD_SLOT) for hh in range(nh)]
    vals = [slice(hh * V_SLOT, (hh + 1) * V_SLOT) for hh in range(nh)]
    ndiag = tq // tk
    assert ndiag % 2 == 0, "slot parity is static only when a query block spans an even number of key blocks"

    def probs(st, m):
        return jnp.exp2((st - m).astype(BF16))

    def qblock(qi, _):
        q0 = pl.multiple_of(qi * tq, tq)
        qs = [qt_ref[qi, hs, :] for hs in heads]
        nfull = qi * ndiag
        last = nfull + ndiag - 1

        def scores_into(slot, kj):
            k0 = pl.multiple_of(kj * tk, tk)
            for hh in range(nh):
                s_scr[slot][hh][...] = _dot(k_ref[pl.ds(k0, tk), heads[hh]], qs[hh])

        ms = []
        for hh in range(nh):
            st = jnp.where(meta_row >= META_PAD, _dot(km_ref[:, heads[hh]], qs[hh]), NEG)
            m = jnp.max(st, axis=0, keepdims=True)
            ms.append(m)
            p_scr[1][hh][0:META_BLOCK, :] = probs(st, m)
            p_scr[1][hh][META_BLOCK:, :] = jnp.zeros((tk - META_BLOCK, tq), BF16)
            acc_scr[hh][...] = jnp.zeros((V_SLOT, tq), F32)
        scores_into(0, 0)

        def kvstep(kj, slot, carry, masked, look_ahead=True, maybe_first=False):
            ms, alphas = carry
            kprev = jnp.maximum(kj - 1, 0)
            pvs = []
            for hh in range(nh):
                vt_prev = vt_ref[kprev, vals[hh], :]
                if maybe_first:
                    vt_prev = jnp.where(kj == 0, vmt_ref[vals[hh], :], vt_prev)
                pvs.append(_dot(vt_prev, p_scr[1 - slot][hh][...]))
            if look_ahead:
                scores_into(1 - slot, kj + 1)
            out = ([], [])
            for hh in range(nh):
                st = s_scr[slot][hh][...]
                if masked:
                    st = jnp.where(kj * tk + key_i <= q0 + qry_i, st, NEG)
                m_new = jnp.maximum(ms[hh], jnp.max(st, axis=0, keepdims=True))
                p_scr[slot][hh][...] = probs(st, m_new)
                acc_scr[hh][...] = alphas[hh] * acc_scr[hh][...] + pvs[hh]
                out[0].append(m_new)
                out[1].append(jnp.exp2(ms[hh] - m_new))
            return tuple(tuple(x) for x in out)

        def run_steps(k_start, carry, n):
            for j in range(n):
                carry = kvstep(k_start + j, j % 2, carry, False, maybe_first=(j == 0))
            return carry

        ones = jnp.ones((1, tq), F32)
        carry = lax.fori_loop(0, nfull // 4, lambda t, c: run_steps(4 * t, c, 4), (tuple(ms), (ones,) * nh))
        carry = lax.cond(nfull % 4 == 2, lambda c: run_steps(nfull - 2, c, 2), lambda c: c, carry)
        for d in range(ndiag):
            carry = kvstep(nfull + d, d % 2, carry, True, look_ahead=d + 1 < ndiag, maybe_first=(d == 0))
        _, alphas = carry
        outs = []
        for hh in range(nh):
            acc = alphas[hh] * acc_scr[hh][...] + _dot(vt_ref[last, vals[hh], :], p_scr[(ndiag - 1) % 2][hh][...])
            outs.append(acc[:V_DIM] / acc[V_DIM:V_DIM + 1])
        o_ref[pl.ds(q0, tq), :] = jnp.concatenate(outs, axis=0).T.astype(BF16)
        return 0

    lax.fori_loop(0, seq // tq, qblock, 0)


def _flash(qt, k, vt, kmeta, vmeta_t):
    b, seq, _ = k.shape
    tq = qt.shape[-1]
    tk = vt.shape[-1]
    nh = FLASH_HEADS
    return pl.pallas_call(
        functools.partial(_flash_body, tq=tq, tk=tk, seq=seq),
        grid=(b, MLA_HEADS // nh),
        in_specs=[
            pl.BlockSpec((None, seq // tq, nh * HEAD_SLOT, tq), lambda i, p: (i, 0, p, 0)),
            pl.BlockSpec((None, seq, nh * HEAD_SLOT), lambda i, p: (i, 0, p)),
            pl.BlockSpec((None, seq // tk, nh * V_SLOT, tk), lambda i, p: (i, 0, p, 0)),
            pl.BlockSpec((META_BLOCK, nh * HEAD_SLOT), lambda i, p: (0, p)),
            pl.BlockSpec((nh * V_SLOT, tk), lambda i, p: (p, 0)),
        ],
        out_specs=pl.BlockSpec((None, seq, nh * V_DIM), lambda i, p: (i, 0, p)),
        out_shape=jax.ShapeDtypeStruct((b, seq, MLA_HEADS * V_DIM), BF16),
        scratch_shapes=([pltpu.VMEM((tk, tq), F32)] * (2 * nh) + [pltpu.VMEM((tk, tq), BF16)] * (2 * nh)
                        + [pltpu.VMEM((V_SLOT, tq), F32)] * nh),
        compiler_params=_cparams(("parallel", "parallel")),
        name="flash",
    )(qt, k, vt, kmeta, vmeta_t)


HALO = 16


def _expand2(xs, expand):
    q = xs[0].shape[0]
    parts = []
    for x in xs:
        hi = x.astype(BF16)
        parts += [hi, (x - hi.astype(F32)).astype(BF16)]
    out = _dot(jnp.concatenate(parts, axis=0), expand)
    return [out[2 * i * q:(2 * i + 1) * q] + out[(2 * i + 1) * q:(2 * i + 2) * q] for i in range(len(xs))]


def _ssd_body(xbc_ref, halo_ref, mh_ref, small_ref, sz_ref, cw_ref, cb_ref, dtb_ref, aneg_ref, dsk_ref,
              ng_ref, init_ref, o_ref, fin_ref, st_ref, y_ref, *, q, n_pad):
    c = pl.program_id(1)
    nc = pl.num_programs(1)

    @pl.when(c == 0)
    def _():
        st_ref[...] = init_ref[...]

    state = st_ref[...]
    halo = jnp.where(c == 0, mh_ref[...], halo_ref[...])
    for j in range(xbc_ref.shape[0] // q):
        rows = slice(j * q, (j + 1) * q)
        x_b = xbc_ref[rows, :]
        state = _ssd_chunk(x_b, halo, small_ref[rows, LANES:2 * LANES], sz_ref[rows, :], state,
                           cw_ref, cb_ref, dtb_ref, aneg_ref, dsk_ref, ng_ref, o_ref.at[rows, :], y_ref.at[rows, :],
                           q=q, n_pad=n_pad - j * q)
        halo = x_b[q - HALO:, :]
    st_ref[...] = state

    @pl.when(c == nc - 1)
    def _():
        fin_ref[...] = state


def _ssd_chunk(x_b, halo, dt_raw, sz, prev, cw_ref, cb_ref, dtb_ref, aneg_ref, dsk_ref, ng_ref, o_ref, y_ref,
               *, q, n_pad):
    x_ext = jnp.concatenate([halo, x_b], axis=0)
    sh_r = lax.broadcasted_iota(jnp.int32, (q, HALO + q), 0)
    sh_c = lax.broadcasted_iota(jnp.int32, (q, HALO + q), 1)
    xc = cb_ref[...] + cw_ref[CONV_WIDTH - 1:CONV_WIDTH, :] * x_b.astype(F32)
    shifts = jnp.concatenate([(sh_c == sh_r + (HALO - d)).astype(BF16) for d in range(1, CONV_WIDTH)], axis=0)
    shifted = _dot(shifts, x_ext)
    for d in range(1, CONV_WIDTH):
        xc = xc + cw_ref[CONV_WIDTH - 1 - d:CONV_WIDTH - d, :] * shifted[(d - 1) * q:d * q]
    xc = xc * _sigmoid(xc)

    lane = lax.broadcasted_iota(jnp.int32, (q, LANES), 1)
    dt = dt_raw + dtb_ref[...]
    dt = jnp.maximum(dt, 0.0) + jnp.log(1.0 + jnp.exp(-jnp.abs(dt)))
    dt = jnp.where((lane >= DT_LANE0) & (lane < DT_LANE0 + SSM_HEADS), dt, 0.0)
    if n_pad > 0:
        rowv = lax.broadcasted_iota(jnp.int32, (q, 1), 0) >= n_pad
        xc = jnp.where(rowv, xc, 0.0)
        dt = jnp.where(rowv, dt, 0.0)

    xs = xc[:, :D_INNER]
    bm = xc[:, D_INNER:D_INNER + LANES]
    cm = xc[:, D_INNER + LANES:]

    a = dt * aneg_ref[...]
    ri = lax.broadcasted_iota(jnp.int32, (q, q), 0)
    ci = lax.broadcasted_iota(jnp.int32, (q, q), 1)
    tril = ri >= ci
    a_cs = _dot_exact_lhs(tril.astype(BF16), a)
    a_cs_t = a_cs.T

    er = lax.broadcasted_iota(jnp.int32, (LANES, D_INNER), 0)
    ec = lax.broadcasted_iota(jnp.int32, (LANES, D_INNER), 1)
    expand = (er - DT_LANE0 == ec // SSM_HEAD_DIM).astype(BF16)
    dt_x, grow_x, rest_x = _expand2([dt, jnp.exp(a_cs), jnp.exp(a_cs[q - 1:q, :] - a_cs)], expand)

    xdt = xs * dt_x
    xdt_b = xdt.astype(BF16)
    xd_b = (xdt * rest_x).astype(BF16)

    bm_b = bm.astype(BF16)
    cm_b = cm.astype(BF16)
    lane_q = lax.broadcasted_iota(jnp.int32, (q, LANES), 1)

    for g in range(SSM_GROUPS):
        in_g = (lane_q >= g * D_STATE) & (lane_q < (g + 1) * D_STATE)
        cb = _dot_nt(jnp.where(in_g, cm, 0.0).astype(BF16), bm_b)
        for pr in range(SSM_HEADS // SSM_GROUPS // 2):
            pair = g * (SSM_HEADS // SSM_GROUPS // 2) + pr
            rhs = xdt_b[:, pair * LANES:(pair + 1) * LANES]
            ys = []
            for hh in range(2):
                e = DT_LANE0 + 2 * pair + hh
                seg = a_cs[:, e:e + 1] - a_cs_t[e:e + 1, :]
                lmat = jnp.exp(jnp.where(tril, seg, NEG))
                ys.append(_dot((cb * lmat).astype(BF16), rhs))
            y_ref[:, pair * LANES:(pair + 1) * LANES] = jnp.where(lane_q < SSM_HEAD_DIM, ys[0], ys[1])

    sr = lax.broadcasted_iota(jnp.int32, (LANES, D_INNER), 0)
    sc = lax.broadcasted_iota(jnp.int32, (LANES, D_INNER), 1)
    same_group = (sr // D_STATE) == (sc // (D_INNER // SSM_GROUPS))
    y_off = _dot(cm_b, prev.astype(BF16)) * grow_x
    st_new = prev * grow_x[q - 1:q, :] + jnp.where(same_group, _dot_tn(bm_b, xd_b), 0.0)

    y = y_ref[...] + y_off + xs * dsk_ref[...]
    yz = y * sz.astype(F32)
    half = D_INNER // SSM_GROUPS
    outs = []
    for g in range(SSM_GROUPS):
        part = yz[:, g * half:(g + 1) * half]
        outs.append(part * lax.rsqrt(jnp.mean(part * part, axis=-1, keepdims=True) + EPS))
    o_ref[...] = (jnp.concatenate(outs, axis=1) * ng_ref[...]).astype(BF16)
    return st_new


def _ssd(xbc, small, sz, meta_halo, init_state, cw, cb, dtb, aneg, dsk, ng, blk, q, n_pad):
    b, seq, _ = xbc.shape
    nc = seq // blk
    hb = blk // HALO
    full = lambda a: pl.BlockSpec(a.shape, lambda i, c: (0,) * a.ndim)
    return pl.pallas_call(
        functools.partial(_ssd_body, q=q, n_pad=n_pad),
        grid=(b, nc),
        in_specs=[
            pl.BlockSpec((None, blk, CONV_DIM), lambda i, c: (i, c, 0)),
            pl.BlockSpec((None, HALO, CONV_DIM), lambda i, c: (i, jnp.maximum(c * hb - 1, 0), 0)),
            full(meta_halo),
            pl.BlockSpec((None, blk, A_SMALL), lambda i, c: (i, c, 0)),
            pl.BlockSpec((None, blk, D_INNER), lambda i, c: (i, c, 0)),
            full(cw), full(cb), full(dtb), full(aneg), full(dsk), full(ng), full(init_state),
        ],
        out_specs=[
            pl.BlockSpec((None, blk, D_INNER), lambda i, c: (i, c, 0)),
            pl.BlockSpec((None, LANES, D_INNER), lambda i, c: (i, 0, 0)),
        ],
        out_shape=[jax.ShapeDtypeStruct((b, seq, D_INNER), BF16),
                   jax.ShapeDtypeStruct((b, LANES, D_INNER), F32)],
        scratch_shapes=[pltpu.VMEM((LANES, D_INNER), F32),
                        pltpu.VMEM((blk, D_INNER), F32)],
        compiler_params=_cparams(("parallel", "arbitrary")),
        name="ssd",
    )(xbc, xbc, meta_halo, small, sz, cw, cb, dtb, aneg, dsk, ng, init_state)


def _route(logits):
    lane = lax.broadcasted_iota(jnp.int32, logits.shape, 1).astype(F32)
    gl = jnp.where(lane < N_EXPERT_GROUPS, logits, NEG)
    gmax = jnp.max(gl, axis=1, keepdims=True)
    gidx = jnp.min(jnp.where(gl == gmax, lane, float(LANES)), axis=1, keepdims=True)
    p_g = 1.0 / jnp.sum(jnp.exp(gl - gmax), axis=1, keepdims=True)
    lo = EXPERT_LANE0 + EXPERTS_PER_GROUP * gidx
    el = jnp.where((lane >= lo) & (lane < lo + EXPERTS_PER_GROUP), logits, NEG)
    t1 = jnp.max(el, axis=1, keepdims=True)
    i1 = jnp.min(jnp.where(el == t1, lane, float(LANES)), axis=1, keepdims=True)
    el2 = jnp.where(lane == i1, NEG, el)
    t2 = jnp.max(el2, axis=1, keepdims=True)
    i2 = jnp.min(jnp.where(el2 == t2, lane, float(LANES)), axis=1, keepdims=True)
    e21 = jnp.exp(t2 - t1)
    w1 = p_g / (1.0 + e21)
    w2 = w1 * e21
    return lane, i1, i2, w1, w2


RT_E1, RT_E2, RT_R1, RT_R2, RT_W1, RT_W2 = range(6)


def _merge_body(x_ref, oa_ref, os_ref, ga_ref, gs_ref, wa_ref, ws_ref, wo_ref, nf_ref, wr_ref, br_ref,
                h2_ref, v_ref, rt_ref, cnt_out_ref, cnt_ref):
    @pl.when(pl.program_id(0) == 0)
    def _():
        cnt_ref[...] = jnp.zeros_like(cnt_ref)

    merged = (ga_ref[...].astype(F32) * _dot(oa_ref[...], wa_ref[...])
              + gs_ref[...].astype(F32) * _dot(os_ref[...], ws_ref[...]))
    h2 = x_ref[...] + _dot(merged.astype(BF16), wo_ref[...])
    h2_ref[...] = h2
    v = _rms(h2, nf_ref[...])
    _store_token_slabs(v_ref, v)
    lane, i1, i2, w1, w2 = _route(_dot(v.astype(BF16), wr_ref[...]) + br_ref[...])

    tm = lane.shape[0]
    chosen = jnp.where((lane == i1) | (lane == i2), 1.0, 0.0)
    ri = lax.broadcasted_iota(jnp.int32, (tm, tm), 0)
    ci = lax.broadcasted_iota(jnp.int32, (tm, tm), 1)
    earlier = _dot((ri > ci).astype(BF16), chosen.astype(BF16)) + cnt_ref[...]
    r1 = jnp.sum(jnp.where(lane == i1, earlier, 0.0), axis=1, keepdims=True)
    r2 = jnp.sum(jnp.where(lane == i2, earlier, 0.0), axis=1, keepdims=True)
    cnt_ref[...] += jnp.sum(chosen, axis=0, keepdims=True)
    cnt_out_ref[...] = jnp.broadcast_to(cnt_ref[...], cnt_out_ref.shape)

    rec = jnp.zeros_like(lane)
    for col, val in ((RT_E1, i1 - EXPERT_LANE0), (RT_E2, i2 - EXPERT_LANE0), (RT_R1, r1), (RT_R2, r2),
                     (RT_W1, w1), (RT_W2, w2)):
        rec = jnp.where(lane == col, val, rec)
    rt_ref[...] = rec


def _merge(x2d, oa, osm, ga, gs, wa, ws, wo, nf, wr, br, tm):
    r = x2d.shape[0]
    row = lambda n: pl.BlockSpec((tm, n), lambda i: (i, 0))
    full = lambda a: pl.BlockSpec(a.shape, lambda i: (0, 0))
    return pl.pallas_call(
        _merge_body,
        grid=(r // tm,),
        in_specs=[row(D_MODEL)] * 5 + [full(wa), full(ws), full(wo), full(nf), full(wr), full(br)],
        out_specs=[row(D_MODEL), pl.BlockSpec((tm * SLAB, LANES), lambda i: (i, 0)), row(LANES),
                   pl.BlockSpec((8, LANES), lambda i: (0, 0))],
        out_shape=[jax.ShapeDtypeStruct((r, D_MODEL), F32),
                   jax.ShapeDtypeStruct((r * SLAB, LANES), F32),
                   jax.ShapeDtypeStruct((r, LANES), F32),
                   jax.ShapeDtypeStruct((8, LANES), F32)],
        scratch_shapes=[pltpu.VMEM((1, LANES), F32)],
        compiler_params=_cparams(("arbitrary",)),
        name="merge",
    )(x2d, oa, osm, ga, gs, wa, ws, wo, nf, wr, br)


SLAB = D_MODEL // LANES


def _store_token_slabs(ref, x):
    tm = x.shape[0]
    for s in range(SLAB):
        ref[pl.ds(s, tm, stride=SLAB), :] = x[:, s * LANES:(s + 1) * LANES]


def _load_token_slabs(ref):
    tm = ref.shape[0] // SLAB
    return jnp.concatenate([ref[pl.ds(s, tm, stride=SLAB), :] for s in range(SLAB)], axis=1)


def _row_copy(src_ref, src_row, dst_ref, dst_row, sem):
    src = src_ref.at[pl.ds(pl.multiple_of(src_row * SLAB, SLAB), SLAB)]
    dst = dst_ref.at[pl.ds(pl.multiple_of(dst_row * SLAB, SLAB), SLAB)]
    return pltpu.make_async_copy(src, dst, sem)


def _dispatch_body(pos_ref, v_ref, xs_ref, sem):
    tm = v_ref.shape[0] // SLAB

    def issue(t, _):
        _row_copy(v_ref, t, xs_ref, pos_ref[0, t], sem).start(priority=0)
        _row_copy(v_ref, t, xs_ref, pos_ref[1, t], sem).start(priority=1)
        return 0

    lax.fori_loop(0, tm, issue, 0, unroll=8)
    for _ in range(2):
        pltpu.make_async_copy(v_ref, xs_ref.at[pl.ds(0, tm * SLAB)], sem).wait()


def _slot_spec(tm):
    return pl.BlockSpec((8, tm), lambda i: (0, i), memory_space=pltpu.SMEM)


def _dispatch(pos, v, tm):
    r = v.shape[0] // SLAB
    return pl.pallas_call(
        _dispatch_body,
        grid=(r // tm,),
        in_specs=[_slot_spec(tm), pl.BlockSpec((tm * SLAB, LANES), lambda i: (i, 0))],
        out_specs=pl.BlockSpec(memory_space=pl.ANY),
        out_shape=jax.ShapeDtypeStruct((2 * r * SLAB, LANES), F32),
        scratch_shapes=[pltpu.SemaphoreType.DMA],
        compiler_params=_cparams(("arbitrary",)),
        name="moe_dispatch",
    )(pos, v)


def _experts_body(tile_ref, exp_ref, off_ref, nwork_ref, xs_ref, wg_ref, wu_ref, wd_ref, ys_ref,
                  wg_s, wu_s, wd_s):
    w = pl.program_id(0)
    tme = xs_ref.shape[0] // SLAB

    @pl.when(w < nwork_ref[0])
    def _():
        e = exp_ref[w]
        tile = tile_ref[w]
        prev = jnp.maximum(w - 1, 0)
        new_expert = (w == 0) | (exp_ref[prev] != e)
        new_tile = (w == 0) | (tile_ref[prev] != tile)

        @pl.when(new_expert)
        def _():
            wg_s[...] = wg_ref[...].astype(BF16)
            wu_s[...] = wu_ref[...].astype(BF16)
            wd_s[...] = wd_ref[...].astype(BF16)

        x = _load_token_slabs(xs_ref).astype(BF16)
        g = _dot(x, wg_s[...])
        u = _dot(x, wu_s[...])
        y = _dot((g * _sigmoid(g) * u).astype(BF16), wd_s[...])
        rows = tile * tme + lax.broadcasted_iota(jnp.int32, (tme, 1), 0)
        mine = (rows >= off_ref[e]) & (rows < off_ref[e + 1])

        @pl.when(new_tile)
        def _():
            _store_token_slabs(ys_ref, jnp.where(mine, y, 0.0))

        @pl.when(jnp.logical_not(new_tile))
        def _():
            _store_token_slabs(ys_ref, jnp.where(mine, y, _load_token_slabs(ys_ref)))


def _experts(tile_of, exp_of, offs, nwork, xs, wg, wu, wd, tme):
    nw = tile_of.shape[0]
    grid_spec = pltpu.PrefetchScalarGridSpec(
        num_scalar_prefetch=4,
        grid=(nw,),
        in_specs=[
            pl.BlockSpec((tme * SLAB, LANES), lambda w, t, e, o, n: (t[w], 0)),
            pl.BlockSpec((None, D_MODEL, D_EXPERT), lambda w, t, e, o, n: (e[w], 0, 0)),
            pl.BlockSpec((None, D_MODEL, D_EXPERT), lambda w, t, e, o, n: (e[w], 0, 0)),
            pl.BlockSpec((None, D_EXPERT, D_MODEL), lambda w, t, e, o, n: (e[w], 0, 0)),
        ],
        out_specs=pl.BlockSpec((tme * SLAB, LANES), lambda w, t, e, o, n: (t[w], 0)),
        scratch_shapes=[pltpu.VMEM((D_MODEL, D_EXPERT), BF16), pltpu.VMEM((D_MODEL, D_EXPERT), BF16),
                        pltpu.VMEM((D_EXPERT, D_MODEL), BF16)],
    )
    return pl.pallas_call(
        _experts_body,
        grid_spec=grid_spec,
        out_shape=jax.ShapeDtypeStruct(xs.shape, F32),
        compiler_params=_cparams(("arbitrary",)),
        name="moe_experts",
    )(tile_of, exp_of, offs, nwork, xs, wg, wu, wd)


def _combine_body(pos_ref, pos_next_ref, rt_ref, h2_ref, nf_ref, ys_ref, out_ref, y1_ref, y2_ref, sem):
    i = pl.program_id(0)
    n = pl.num_programs(0)
    tm = h2_ref.shape[0]
    slot = i % 2

    def issue_tile(p_ref, s):
        def issue(t, _):
            _row_copy(ys_ref, p_ref[0, t], y1_ref.at[s], t, sem.at[s]).start(priority=0)
            _row_copy(ys_ref, p_ref[1, t], y2_ref.at[s], t, sem.at[s]).start(priority=1)
            return 0

        lax.fori_loop(0, tm, issue, 0, unroll=8)

    @pl.when(i == 0)
    def _():
        issue_tile(pos_ref, 0)

    @pl.when(i + 1 < n)
    def _():
        issue_tile(pos_next_ref, 1 - slot)

    for buf in (y1_ref, y2_ref):
        pltpu.make_async_copy(ys_ref.at[pl.ds(0, tm * SLAB)], buf.at[slot], sem.at[slot]).wait()
    lane = lax.broadcasted_iota(jnp.int32, rt_ref.shape, 1)
    rt = rt_ref[...]
    w1 = jnp.sum(jnp.where(lane == RT_W1, rt, 0.0), axis=1, keepdims=True)
    w2 = jnp.sum(jnp.where(lane == RT_W2, rt, 0.0), axis=1, keepdims=True)
    moe = w1 * _load_token_slabs(y1_ref.at[slot]) + w2 * _load_token_slabs(y2_ref.at[slot])
    out_ref[...] = _rms(h2_ref[...] + moe, nf_ref[...])


def _combine(pos, rt, h2, nf, ys, tm):
    r = h2.shape[0]
    last = r // tm - 1
    next_slots = pl.BlockSpec((8, tm), lambda i: (0, jnp.minimum(i + 1, last)), memory_space=pltpu.SMEM)
    return pl.pallas_call(
        _combine_body,
        grid=(r // tm,),
        in_specs=[_slot_spec(tm), next_slots, pl.BlockSpec((tm, LANES), lambda i: (i, 0)),
                  pl.BlockSpec((tm, D_MODEL), lambda i: (i, 0)), pl.BlockSpec((1, D_MODEL), lambda i: (0, 0)),
                  pl.BlockSpec(memory_space=pl.ANY)],
        out_specs=pl.BlockSpec((tm, D_MODEL), lambda i: (i, 0)),
        out_shape=jax.ShapeDtypeStruct((r, D_MODEL), F32),
        scratch_shapes=[pltpu.VMEM((2, tm * SLAB, LANES), F32), pltpu.VMEM((2, tm * SLAB, LANES), F32),
                        pltpu.SemaphoreType.DMA((2,))],
        compiler_params=_cparams(("arbitrary",)),
        name="moe_combine",
    )(pos, pos, rt, h2, nf, ys)


def _slots_body(rt_ref, cnt_ref, pos_ref):
    rt = rt_ref[...]
    lane = lax.broadcasted_iota(jnp.int32, rt.shape, 1).astype(F32)
    col = lambda c: jnp.sum(jnp.where(lane == c, rt, 0.0), axis=1, keepdims=True)
    ri = lax.broadcasted_iota(jnp.int32, (LANES, LANES), 0)
    ci = lax.broadcasted_iota(jnp.int32, (LANES, LANES), 1)
    first = _dot_exact_rhs(cnt_ref[...], (ri < ci).astype(BF16))[0:1, :]
    slot = lambda e, r: col(r) + jnp.sum(jnp.where(lane == col(e) + EXPERT_LANE0, first, 0.0),
                                         axis=1, keepdims=True)
    rec = jnp.where(lane == 0, slot(RT_E1, RT_R1), jnp.where(lane == 1, slot(RT_E2, RT_R2), 0.0))
    sr = lax.broadcasted_iota(jnp.int32, (8, LANES), 0)
    sc = lax.broadcasted_iota(jnp.int32, (8, LANES), 1)
    sel = (sr == sc).astype(BF16)
    pos_ref[...] = sum(_dot_nt(sel, part) for part in _split3(rec)).astype(jnp.int32)


def _slots(rt, counts, tm):
    r = rt.shape[0]
    return pl.pallas_call(
        _slots_body,
        grid=(r // tm,),
        in_specs=[pl.BlockSpec((tm, LANES), lambda i: (i, 0)), pl.BlockSpec((8, LANES), lambda i: (0, 0))],
        out_specs=pl.BlockSpec((8, tm), lambda i: (0, i)),
        out_shape=jax.ShapeDtypeStruct((8, r), jnp.int32),
        compiler_params=_cparams(("parallel",)),
        name="moe_slots",
    )(rt, counts)


def _moe_plan(counts, rows, tme):
    cnt = counts.astype(jnp.int32)
    ends = jnp.cumsum(cnt)
    offs = jnp.concatenate([jnp.zeros((1,), jnp.int32), ends])
    first_tile = offs[:-1] // tme
    last_tile = (ends - 1) // tme
    n_items = jnp.where(cnt > 0, last_tile - first_tile + 1, 0)
    item_end = jnp.cumsum(n_items)
    nwork = item_end[-1]
    nw = 2 * rows // tme + N_EXPERTS - 1
    w = jnp.arange(nw, dtype=jnp.int32)
    wc = jnp.minimum(w, nwork - 1)
    exp_of = jnp.sum((wc[:, None] >= item_end[None, :]).astype(jnp.int32), axis=1)
    onehot = (exp_of[:, None] == jnp.arange(N_EXPERTS, dtype=jnp.int32)[None, :]).astype(jnp.int32)
    item_start = jnp.sum(onehot * (item_end - n_items)[None, :], axis=1)
    tile_of = (jnp.sum(onehot * first_tile[None, :], axis=1) + (wc - item_start)).astype(jnp.int32)
    return tile_of, exp_of, offs, nwork.reshape(1).astype(jnp.int32)


def _prep_w_in(w):
    o = np.cumsum([0, Q_RANK, KV_RANK, ROPE_DIM, D_INNER, CONV_DIM, SSM_HEADS, D_MODEL, D_MODEL])
    cq, ckv, kr, z, xbc, dt, ga, gs = (w[:, o[i]:o[i + 1]] for i in range(8))
    hr = ROPE_DIM // 2
    zeros = lambda n: jnp.zeros((w.shape[0], n), w.dtype)
    kra = jnp.concatenate([zeros(NOPE_DIM), kr, zeros(HEAD_SLOT - NOPE_DIM - ROPE_DIM)], axis=1)
    krb = jnp.concatenate([zeros(NOPE_DIM), -kr[:, hr:], kr[:, :hr], dt,
                           zeros(HEAD_SLOT - DT_LANE0 - SSM_HEADS)], axis=1)
    small = jnp.concatenate([kra, krb], axis=1)
    return tuple(a.astype(BF16) for a in (cq, ckv, small, z, xbc, ga, gs))


def _prep_w_uq(w):
    w = w.reshape(Q_RANK, MLA_HEADS, NOPE_DIM + ROPE_DIM)
    hr = ROPE_DIM // 2
    nope, r1, r2 = w[..., :NOPE_DIM], w[..., NOPE_DIM:NOPE_DIM + hr], w[..., NOPE_DIM + hr:]
    z = jnp.zeros((Q_RANK, MLA_HEADS, HEAD_SLOT - NOPE_DIM - ROPE_DIM), w.dtype)
    wa = jnp.concatenate([nope, r1, r2, z], axis=-1).reshape(Q_RANK, -1)
    return wa.T.astype(BF16)


def _prep_w_ukv(w):
    w = w.reshape(KV_RANK, MLA_HEADS, NOPE_DIM + V_DIM)
    kn, v = w[..., :NOPE_DIM], w[..., NOPE_DIM:]
    wk = jnp.concatenate([kn, jnp.zeros((KV_RANK, MLA_HEADS, HEAD_SLOT - NOPE_DIM), w.dtype)], axis=-1)
    vslot = jnp.concatenate([v, jnp.zeros((KV_RANK, MLA_HEADS, V_SLOT - V_DIM), w.dtype)], axis=-1)
    return wk.reshape(KV_RANK, -1).astype(BF16), vslot.reshape(KV_RANK, -1).T.astype(BF16)


def _rope_tables(pos):
    inv = ROPE_THETA ** (-jnp.arange(0, ROPE_DIM, 2, dtype=F32) / ROPE_DIM)
    ang = pos.astype(F32)[:, None] * inv[None, :]
    cos, sin = jnp.cos(ang), jnp.sin(ang)
    n = pos.shape[0]
    pad = jnp.zeros((n, HEAD_SLOT - NOPE_DIM - ROPE_DIM), F32)
    scale = (NOPE_DIM + ROPE_DIM) ** -0.5 * float(np.log2(np.e))
    cq = jnp.concatenate([jnp.ones((n, NOPE_DIM), F32), cos, cos, pad], axis=1) * scale
    sq = jnp.concatenate([jnp.zeros((n, NOPE_DIM), F32), sin, sin, pad], axis=1) * scale
    ck = jnp.concatenate([jnp.zeros((n, NOPE_DIM), F32), cos, cos, pad], axis=1)
    sk = jnp.concatenate([jnp.zeros((n, NOPE_DIM), F32), sin, sin, pad], axis=1)
    return cq.T, sq.T, ck, sk


def _head_lanes(v):
    return jnp.zeros((1, LANES), F32).at[0, DT_LANE0:DT_LANE0 + SSM_HEADS].set(v.astype(F32))


def _pick(n, prefs):
    for t in prefs:
        if n % t == 0:
            return t
    raise ValueError(f"no tile for {n}")


def kernel(x, meta_tokens, norm_mix, w_in, mla_q_norm, mla_w_uq, mla_kv_norm, mla_w_ukv, ssm_conv_w, ssm_conv_b, ssm_dt_bias, ssm_a_log, ssm_d_skip, ssm_norm, w_branch_attn, w_branch_ssm, w_out, norm_ffn, moe_w_group, moe_b_group, moe_w_expert, moe_b_expert, moe_w_gate, moe_w_up, moe_w_down, norm_final):
    b, seq, _ = x.shape
    assert w_in.shape[0] == 1, "one layer"
    rows = b * seq
    x2d = x.reshape(rows, D_MODEL)
    tm = _pick(seq, (512, 256, 128))
    tq = _pick(seq, (512, 256, 128))
    tk = _pick(tq, (256, 128))
    chunk = 128
    ssd_blk = _pick(seq, (512, 256, 128))

    w_in_r = _prep_w_in(w_in[0])
    wq = _prep_w_uq(mla_w_uq[0])
    wk, wvt = _prep_w_ukv(mla_w_ukv[0])
    g_mix = norm_mix[0].reshape(1, D_MODEL)
    qn = mla_q_norm[0].reshape(1, Q_RANK)
    kvn = mla_kv_norm[0].reshape(1, KV_RANK)
    cw = ssm_conv_w[0]
    cb = ssm_conv_b[0].reshape(1, CONV_DIM)
    dtb = _head_lanes(ssm_dt_bias[0])
    aneg = _head_lanes(-jnp.exp(ssm_a_log[0].astype(F32)))
    dsk = jnp.repeat(ssm_d_skip[0].astype(F32), SSM_HEAD_DIM).reshape(1, D_INNER)
    ng = ssm_norm[0].reshape(1, D_INNER)
    wa = w_branch_attn[0].astype(BF16)
    ws = w_branch_ssm[0].astype(BF16)
    wo = w_out[0].astype(BF16)
    nffn = norm_ffn[0].reshape(1, D_MODEL)
    wr = jnp.zeros((D_MODEL, LANES), F32)
    wr = wr.at[:, :N_EXPERT_GROUPS].set(moe_w_group[0]).at[:, EXPERT_LANE0:EXPERT_LANE0 + N_EXPERTS].set(moe_w_expert[0])
    wr = wr.astype(BF16)
    br = jnp.zeros((1, LANES), F32)
    br = br.at[0, :N_EXPERT_GROUPS].set(moe_b_group[0]).at[0, EXPERT_LANE0:EXPERT_LANE0 + N_EXPERTS].set(moe_b_expert[0])
    wg, wu, wd = moe_w_gate[0], moe_w_up[0], moe_w_down[0]
    nfin = norm_final.reshape(1, D_MODEL)

    meta_blk = jnp.concatenate([jnp.zeros((META_PAD, D_MODEL), F32), meta_tokens.astype(F32)], axis=0)
    m_cq, m_ckv, m_small, m_sz, m_xbc, _, _ = _inproj(meta_blk, g_mix, w_in_r, META_BLOCK)
    meta_pos = jnp.maximum(jnp.arange(META_BLOCK) - META_PAD, 0)
    _, kmeta, vmeta_t = _mla_prep(m_cq, m_ckv, m_small, qn, kvn, wq, wk, wvt, _rope_tables(meta_pos),
                                  META_BLOCK, 1, META_BLOCK, META_BLOCK)
    zero_state = jnp.zeros((LANES, D_INNER), F32)
    zero_halo = jnp.zeros((HALO, CONV_DIM), BF16)
    _, meta_state = _ssd(m_xbc[None], m_small[None], m_sz[None], zero_halo, zero_state,
                         cw, cb, dtb, aneg, dsk, ng, META_BLOCK, META_BLOCK, META_PAD)
    meta_halo = m_xbc[META_BLOCK - HALO:]

    cq, ckv, small, sz, xbc, ga, gs = _inproj(x2d, g_mix, w_in_r, tm)
    qt, k, vt = _mla_prep(cq, ckv, small, qn, kvn, wq, wk, wvt, _rope_tables(N_META + jnp.arange(seq)),
                          tm, seq // tm, tq, tk)
    o_attn = _flash(qt.reshape(b, seq // tq, -1, tq), k.reshape(b, seq, -1), vt.reshape(b, seq // tk, -1, tk),
                    kmeta, jnp.pad(vmeta_t[0], ((0, 0), (0, tk - META_BLOCK))))
    o_ssm, _ = _ssd(xbc.reshape(b, seq, -1), small.reshape(b, seq, -1), sz.reshape(b, seq, -1), meta_halo,
                    meta_state[0], cw, cb, dtb, aneg, dsk, ng, ssd_blk, chunk, 0)
    h2, v, rt, counts = _merge(x2d, o_attn.reshape(rows, -1), o_ssm.reshape(rows, -1), ga, gs, wa, ws, wo,
                               nffn, wr, br, tm)
    tme = _pick(2 * rows, (512, 256, 128))
    tile_of, exp_of, offs, nwork = _moe_plan(counts[0, EXPERT_LANE0:EXPERT_LANE0 + N_EXPERTS], rows, tme)
    pos = _slots(rt, counts, tm)
    xs = _dispatch(pos, v, tm)
    ys = _experts(tile_of, exp_of, offs, nwork, xs, wg, wu, wd, tme)
    out = _combine(pos, rt, h2, nfin, ys, tm)
    return out.reshape(b, seq, D_MODEL)
```

```python
import functools

import numpy as np
import jax
import jax.numpy as jnp
from jax import lax
from jax.experimental import pallas as pl
from jax.experimental.pallas import tpu as pltpu

F32 = jnp.float32
BF16 = jnp.bfloat16

D_MODEL = 1024
N_META = 16
EPS = 1e-6
NEG = -1e30
MLA_HEADS = 16
Q_RANK = 256
KV_RANK = 128
NOPE_DIM = 64
ROPE_DIM = 32
V_DIM = 64
ROPE_THETA = 10000.0
SSM_HEADS = 16
SSM_HEAD_DIM = 64
D_INNER = SSM_HEADS * SSM_HEAD_DIM
SSM_GROUPS = 2
D_STATE = 64
CONV_WIDTH = 4
CONV_DIM = D_INNER + 2 * SSM_GROUPS * D_STATE
N_EXPERT_GROUPS = 4
EXPERTS_PER_GROUP = 8
N_EXPERTS = N_EXPERT_GROUPS * EXPERTS_PER_GROUP
D_EXPERT = 256

LANES = 128
META_BLOCK = 128
META_PAD = META_BLOCK - N_META
HEAD_SLOT = 128
V_SLOT = 80
FLASH_HEADS = 4
EXPERT_LANE0 = 32
DT_LANE0 = 96
VMEM_LIMIT = 48 * 1024 * 1024

A_CQ, A_CKV, A_SMALL, A_Z, A_XBC, A_GA, A_GS = 256, 128, 256, D_INNER, CONV_DIM, D_MODEL, D_MODEL


def _cparams(sem):
    return pltpu.CompilerParams(dimension_semantics=sem, vmem_limit_bytes=VMEM_LIMIT)


def _rms(x, g):
    return x * lax.rsqrt(jnp.mean(x * x, axis=-1, keepdims=True) + EPS) * g


def _sigmoid(x):
    return 1.0 / (1.0 + jnp.exp(-x))


def _dot(a, b):
    return jnp.dot(a, b, preferred_element_type=F32)


def _dot_nt(a, b):
    return lax.dot_general(a, b, (((1,), (1,)), ((), ())), preferred_element_type=F32)


def _dot_tn(a, b):
    return lax.dot_general(a, b, (((0,), (0,)), ((), ())), preferred_element_type=F32)


def _split3(x):
    hi = x.astype(BF16)
    r1 = x - hi.astype(F32)
    mid = r1.astype(BF16)
    lo = (r1 - mid.astype(F32)).astype(BF16)
    return hi, mid, lo


def _dot_exact_rhs(x, m):
    hi, mid, lo = _split3(x)
    return _dot(hi, m) + _dot(mid, m) + _dot(lo, m)


def _dot_exact_lhs(m, x):
    hi, mid, lo = _split3(x)
    return _dot(m, hi) + _dot(m, mid) + _dot(m, lo)


def _inproj_body(x_ref, g_ref, *refs):
    w_refs = refs[:7]
    cq_ref, ckv_ref, small_ref, sz_ref, xbc_ref, ga_ref, gs_ref = refs[7:]
    u = _rms(x_ref[...], g_ref[...]).astype(BF16)

    def mm(i):
        return _dot(u, w_refs[i][...])

    cq_ref[...] = mm(0).astype(BF16)
    ckv_ref[...] = mm(1).astype(BF16)
    small_ref[...] = mm(2)
    z = mm(3)
    sz_ref[...] = (z * _sigmoid(z)).astype(BF16)
    xbc_ref[...] = mm(4).astype(BF16)
    ga_ref[...] = _sigmoid(mm(5)).astype(BF16)
    gs_ref[...] = _sigmoid(mm(6)).astype(BF16)


def _inproj(x2d, g, ws, tm):
    r = x2d.shape[0]
    widths = (A_CQ, A_CKV, A_SMALL, A_Z, A_XBC, A_GA, A_GS)
    dtypes = (BF16, BF16, F32, BF16, BF16, BF16, BF16)
    assert tuple(w.shape[1] for w in ws) == widths
    return pl.pallas_call(
        _inproj_body,
        grid=(r // tm,),
        in_specs=[pl.BlockSpec((tm, D_MODEL), lambda i: (i, 0)), pl.BlockSpec((1, D_MODEL), lambda i: (0, 0))]
        + [pl.BlockSpec(w.shape, lambda i: (0, 0), pipeline_mode=pl.Buffered(1)) for w in ws],
        out_specs=[pl.BlockSpec((tm, n), lambda i: (i, 0)) for n in widths],
        out_shape=[jax.ShapeDtypeStruct((r, n), dt) for n, dt in zip(widths, dtypes)],
        compiler_params=_cparams(("parallel",)),
        name="inproj",
    )(x2d, g, *ws)


def _mla_prep_body(cq_ref, ckv_ref, small_ref, qn_ref, kvn_ref, wqt_ref, wk_ref, wvt_ref, vone_ref,
                   cqt_ref, sqt_ref, ckt_ref, skt_ref, qt_ref, k_ref, vt_ref):
    cn = _rms(cq_ref[...].astype(F32), qn_ref[...]).astype(BF16)
    kn = _rms(ckv_ref[...].astype(F32), kvn_ref[...]).astype(BF16)
    cqt = cqt_ref[...]
    sqt = sqt_ref[...]
    all_heads = MLA_HEADS * HEAD_SLOT
    part = all_heads // 2
    tqb = qt_ref.shape[-1]
    r1, r2, r3 = NOPE_DIM, NOPE_DIM + ROPE_DIM // 2, NOPE_DIM + ROPE_DIM
    for lo in range(0, all_heads, part):
        qa = _dot_nt(wqt_ref[lo:lo + part, :], cn)
        for r0 in range(0, part, HEAD_SLOT):
            x = qa[r0:r0 + HEAD_SLOT]
            rot = jnp.concatenate([x[:r1], -x[r2:r3], x[r1:r2], x[r3:]], axis=0)
            qh = (x * cqt + rot * sqt).astype(BF16)
            for j in range(qt_ref.shape[0]):
                qt_ref[j, lo + r0:lo + r0 + HEAD_SLOT, :] = qh[:, j * tqb:(j + 1) * tqb]
    krp = small_ref[:, 0:LANES] * ckt_ref[...] + small_ref[:, LANES:2 * LANES] * skt_ref[...]
    krp2 = jnp.concatenate([krp, krp], axis=1)
    for hp in range(MLA_HEADS // 2):
        lo, hi = hp * 2 * HEAD_SLOT, (hp + 1) * 2 * HEAD_SLOT
        k_ref[:, lo:hi] = (_dot(kn, wk_ref[:, lo:hi]) + krp2).astype(BF16)
    vt = _dot_nt(wvt_ref[...], kn) + vone_ref[...]
    tkv = vt_ref.shape[-1]
    for j in range(vt_ref.shape[0]):
        vt_ref[j] = vt[:, j * tkv:(j + 1) * tkv].astype(BF16)


def _mla_prep(cq, ckv, small, qn, kvn, wqt, wk, wvt, tabs, tm, seq_blocks, tqb, tkv):
    r = cq.shape[0]
    nq = MLA_HEADS * HEAD_SLOT
    nv = MLA_HEADS * V_SLOT
    cqt, sqt, ckt, skt = tabs
    ones_row = (jnp.arange(nv) % V_SLOT == V_DIM).astype(F32)
    vone = jnp.broadcast_to(ones_row[:, None], (nv, tm))
    row = lambda n: pl.BlockSpec((tm, n), lambda i: (i, 0))
    tab = lambda n: pl.BlockSpec((tm, n), lambda i: (i % seq_blocks, 0))
    tab_t = pl.BlockSpec((HEAD_SLOT, tm), lambda i: (0, i % seq_blocks))
    full = lambda a: pl.BlockSpec(a.shape, lambda i: (0, 0))
    return pl.pallas_call(
        _mla_prep_body,
        grid=(r // tm,),
        in_specs=[row(Q_RANK), row(KV_RANK), row(A_SMALL), full(qn), full(kvn), full(wqt), full(wk), full(wvt),
                  full(vone), tab_t, tab_t, tab(HEAD_SLOT), tab(HEAD_SLOT)],
        out_specs=[pl.BlockSpec((tm // tqb, nq, tqb), lambda i: (i, 0, 0)), row(nq),
                   pl.BlockSpec((tm // tkv, nv, tkv), lambda i: (i, 0, 0))],
        out_shape=[jax.ShapeDtypeStruct((r // tqb, nq, tqb), BF16),
                   jax.ShapeDtypeStruct((r, nq), BF16),
                   jax.ShapeDtypeStruct((r // tkv, nv, tkv), BF16)],
        compiler_params=_cparams(("parallel",)),
        name="mla_prep",
    )(cq, ckv, small, qn, kvn, wqt, wk, wvt, vone, cqt, sqt, ckt, skt)


def _flash_body(qt_ref, k_ref, vt_ref, km_ref, vmt_ref, o_ref, *scratch, tq, tk, seq):
    nh = FLASH_HEADS
    s_scr = (scratch[0:nh], scratch[nh:2 * nh])
    p_scr = (scratch[2 * nh:3 * nh], scratch[3 * nh:4 * nh])
    acc_scr = scratch[4 * nh:5 * nh]
    key_i = lax.broadcasted_iota(jnp.int32, (tk, tq), 0)
    qry_i = lax.broadcasted_iota(jnp.int32, (tk, tq), 1)
    meta_row = lax.broadcasted_iota(jnp.int32, (META_BLOCK, 1), 0)
    heads = [slice(hh * HEAD_SLOT, (hh + 1) * HEAD_SLOT) for hh in range(nh)]
    vals = [slice(hh * V_SLOT, (hh + 1) * V_SLOT) for hh in range(nh)]
    ndiag = tq // tk
    assert ndiag % 2 == 0, "slot parity is static only when a query block spans an even number of key blocks"

    def probs(st, m):
        return jnp.exp2((st - m).astype(BF16))

    def qblock(qi, _):
        q0 = pl.multiple_of(qi * tq, tq)
        qs = [qt_ref[qi, hs, :] for hs in heads]
        nfull = qi * ndiag
        last = nfull + ndiag - 1

        def scores_into(slot, kj):
            k0 = pl.multiple_of(kj * tk, tk)
            for hh in range(nh):
                s_scr[slot][hh][...] = _dot(k_ref[pl.ds(k0, tk), heads[hh]], qs[hh])

        ms = []
        for hh in range(nh):
            st = jnp.where(meta_row >= META_PAD, _dot(km_ref[:, heads[hh]], qs[hh]), NEG)
            m = jnp.max(st, axis=0, keepdims=True)
            ms.append(m)
            p_scr[1][hh][0:META_BLOCK, :] = probs(st, m)
            p_scr[1][hh][META_BLOCK:, :] = jnp.zeros((tk - META_BLOCK, tq), BF16)
            acc_scr[hh][...] = jnp.zeros((V_SLOT, tq), F32)
        scores_into(0, 0)

        def kvstep(kj, slot, carry, masked, look_ahead=True, maybe_first=False):
            ms, alphas = carry
            kprev = jnp.maximum(kj - 1, 0)
            pvs = []
            for hh in range(nh):
                vt_prev = vt_ref[kprev, vals[hh], :]
                if maybe_first:
                    vt_prev = jnp.where(kj == 0, vmt_ref[vals[hh], :], vt_prev)
                pvs.append(_dot(vt_prev, p_scr[1 - slot][hh][...]))
            if look_ahead:
                scores_into(1 - slot, kj + 1)
            out = ([], [])
            for hh in range(nh):
                st = s_scr[slot][hh][...]
                if masked:
                    st = jnp.where(kj * tk + key_i <= q0 + qry_i, st, NEG)
                m_new = jnp.maximum(ms[hh], jnp.max(st, axis=0, keepdims=True))
                p_scr[slot][hh][...] = probs(st, m_new)
                acc_scr[hh][...] = alphas[hh] * acc_scr[hh][...] + pvs[hh]
                out[0].append(m_new)
                out[1].append(jnp.exp2(ms[hh] - m_new))
            return tuple(tuple(x) for x in out)

        def run_steps(k_start, carry, n):
            for j in range(n):
                carry = kvstep(k_start + j, j % 2, carry, False, maybe_first=(j == 0))
            return carry

        ones = jnp.ones((1, tq), F32)
        carry = lax.fori_loop(0, nfull // 4, lambda t, c: run_steps(4 * t, c, 4), (tuple(ms), (ones,) * nh))
        carry = lax.cond(nfull % 4 == 2, lambda c: run_steps(nfull - 2, c, 2), lambda c: c, carry)
        for d in range(ndiag):
            carry = kvstep(nfull + d, d % 2, carry, True, look_ahead=d + 1 < ndiag, maybe_first=(d == 0))
        _, alphas = carry
        outs = []
        for hh in range(nh):
            acc = alphas[hh] * acc_scr[hh][...] + _dot(vt_ref[last, vals[hh], :], p_scr[(ndiag - 1) % 2][hh][...])
            outs.append(acc[:V_DIM] / acc[V_DIM:V_DIM + 1])
        o_ref[pl.ds(q0, tq), :] = jnp.concatenate(outs, axis=0).T.astype(BF16)
        return 0

    lax.fori_loop(0, seq // tq, qblock, 0)


def _flash(qt, k, vt, kmeta, vmeta_t):
    b, seq, _ = k.shape
    tq = qt.shape[-1]
    tk = vt.shape[-1]
    nh = FLASH_HEADS
    return pl.pallas_call(
        functools.partial(_flash_body, tq=tq, tk=tk, seq=seq),
        grid=(b, MLA_HEADS // nh),
        in_specs=[
            pl.BlockSpec((None, seq // tq, nh * HEAD_SLOT, tq), lambda i, p: (i, 0, p, 0)),
            pl.BlockSpec((None, seq, nh * HEAD_SLOT), lambda i, p: (i, 0, p)),
            pl.BlockSpec((None, seq // tk, nh * V_SLOT, tk), lambda i, p: (i, 0, p, 0)),
            pl.BlockSpec((META_BLOCK, nh * HEAD_SLOT), lambda i, p: (0, p)),
            pl.BlockSpec((nh * V_SLOT, tk), lambda i, p: (p, 0)),
        ],
        out_specs=pl.BlockSpec((None, seq, nh * V_DIM), lambda i, p: (i, 0, p)),
        out_shape=jax.ShapeDtypeStruct((b, seq, MLA_HEADS * V_DIM), BF16),
        scratch_shapes=([pltpu.VMEM((tk, tq), F32)] * (2 * nh) + [pltpu.VMEM((tk, tq), BF16)] * (2 * nh)
                        + [pltpu.VMEM((V_SLOT, tq), F32)] * nh),
        compiler_params=_cparams(("parallel", "parallel")),
        name="flash",
    )(qt, k, vt, kmeta, vmeta_t)


HALO = 16


def _expand2(xs, expand):
    q = xs[0].shape[0]
    parts = []
    for x in xs:
        hi = x.astype(BF16)
        parts += [hi, (x - hi.astype(F32)).astype(BF16)]
    out = _dot(jnp.concatenate(parts, axis=0), expand)
    return [out[2 * i * q:(2 * i + 1) * q] + out[(2 * i + 1) * q:(2 * i + 2) * q] for i in range(len(xs))]


def _ssd_body(xbc_ref, halo_ref, mh_ref, small_ref, sz_ref, cw_ref, cb_ref, dtb_ref, aneg_ref, dsk_ref,
              ng_ref, init_ref, o_ref, fin_ref, st_ref, y_ref, *, q, n_pad):
    c = pl.program_id(1)
    nc = pl.num_programs(1)

    @pl.when(c == 0)
    def _():
        st_ref[...] = init_ref[...]

    state = st_ref[...]
    halo = jnp.where(c == 0, mh_ref[...], halo_ref[...])
    for j in range(xbc_ref.shape[0] // q):
        rows = slice(j * q, (j + 1) * q)
        x_b = xbc_ref[rows, :]
        state = _ssd_chunk(x_b, halo, small_ref[rows, LANES:2 * LANES], sz_ref[rows, :], state,
                           cw_ref, cb_ref, dtb_ref, aneg_ref, dsk_ref, ng_ref, o_ref.at[rows, :], y_ref.at[rows, :],
                           q=q, n_pad=n_pad - j * q)
        halo = x_b[q - HALO:, :]
    st_ref[...] = state

    @pl.when(c == nc - 1)
    def _():
        fin_ref[...] = state


def _ssd_chunk(x_b, halo, dt_raw, sz, prev, cw_ref, cb_ref, dtb_ref, aneg_ref, dsk_ref, ng_ref, o_ref, y_ref,
               *, q, n_pad):
    x_ext = jnp.concatenate([halo, x_b], axis=0)
    sh_r = lax.broadcasted_iota(jnp.int32, (q, HALO + q), 0)
    sh_c = lax.broadcasted_iota(jnp.int32, (q, HALO + q), 1)
    xc = cb_ref[...] + cw_ref[CONV_WIDTH - 1:CONV_WIDTH, :] * x_b.astype(F32)
    shifts = jnp.concatenate([(sh_c == sh_r + (HALO - d)).astype(BF16) for d in range(1, CONV_WIDTH)], axis=0)
    shifted = _dot(shifts, x_ext)
    for d in range(1, CONV_WIDTH):
        xc = xc + cw_ref[CONV_WIDTH - 1 - d:CONV_WIDTH - d, :] * shifted[(d - 1) * q:d * q]
    xc = xc * _sigmoid(xc)

    lane = lax.broadcasted_iota(jnp.int32, (q, LANES), 1)
    dt = dt_raw + dtb_ref[...]
    dt = jnp.maximum(dt, 0.0) + jnp.log(1.0 + jnp.exp(-jnp.abs(dt)))
    dt = jnp.where((lane >= DT_LANE0) & (lane < DT_LANE0 + SSM_HEADS), dt, 0.0)
    if n_pad > 0:
        rowv = lax.broadcasted_iota(jnp.int32, (q, 1), 0) >= n_pad
        xc = jnp.where(rowv, xc, 0.0)
        dt = jnp.where(rowv, dt, 0.0)

    xs = xc[:, :D_INNER]
    bm = xc[:, D_INNER:D_INNER + LANES]
    cm = xc[:, D_INNER + LANES:]

    a = dt * aneg_ref[...]
    ri = lax.broadcasted_iota(jnp.int32, (q, q), 0)
    ci = lax.broadcasted_iota(jnp.int32, (q, q), 1)
    tril = ri >= ci
    a_cs = _dot_exact_lhs(tril.astype(BF16), a)
    a_cs_t = a_cs.T

    er = lax.broadcasted_iota(jnp.int32, (LANES, D_INNER), 0)
    ec = lax.broadcasted_iota(jnp.int32, (LANES, D_INNER), 1)
    expand = (er - DT_LANE0 == ec // SSM_HEAD_DIM).astype(BF16)
    dt_x, grow_x, rest_x = _expand2([dt, jnp.exp(a_cs), jnp.exp(a_cs[q - 1:q, :] - a_cs)], expand)

    xdt = xs * dt_x
    xdt_b = xdt.astype(BF16)
    xd_b = (xdt * rest_x).astype(BF16)

    bm_b = bm.astype(BF16)
    cm_b = cm.astype(BF16)
    lane_q = lax.broadcasted_iota(jnp.int32, (q, LANES), 1)

    for g in range(SSM_GROUPS):
        in_g = (lane_q >= g * D_STATE) & (lane_q < (g + 1) * D_STATE)
        cb = _dot_nt(jnp.where(in_g, cm, 0.0).astype(BF16), bm_b)
        for pr in range(SSM_HEADS // SSM_GROUPS // 2):
            pair = g * (SSM_HEADS // SSM_GROUPS // 2) + pr
            rhs = xdt_b[:, pair * LANES:(pair + 1) * LANES]
            ys = []
            for hh in range(2):
                e = DT_LANE0 + 2 * pair + hh
                seg = a_cs[:, e:e + 1] - a_cs_t[e:e + 1, :]
                lmat = jnp.exp(jnp.where(tril, seg, NEG))
                ys.append(_dot((cb * lmat).astype(BF16), rhs))
            y_ref[:, pair * LANES:(pair + 1) * LANES] = jnp.where(lane_q < SSM_HEAD_DIM, ys[0], ys[1])

    sr = lax.broadcasted_iota(jnp.int32, (LANES, D_INNER), 0)
    sc = lax.broadcasted_iota(jnp.int32, (LANES, D_INNER), 1)
    same_group = (sr // D_STATE) == (sc // (D_INNER // SSM_GROUPS))
    y_off = _dot(cm_b, prev.astype(BF16)) * grow_x
    st_new = prev * grow_x[q - 1:q, :] + jnp.where(same_group, _dot_tn(bm_b, xd_b), 0.0)

    y = y_ref[...] + y_off + xs * dsk_ref[...]
    yz = y * sz.astype(F32)
    half = D_INNER // SSM_GROUPS
    outs = []
    for g in range(SSM_GROUPS):
        part = yz[:, g * half:(g + 1) * half]
        outs.append(part * lax.rsqrt(jnp.mean(part * part, axis=-1, keepdims=True) + EPS))
    o_ref[...] = (jnp.concatenate(outs, axis=1) * ng_ref[...]).astype(BF16)
    return st_new


def _ssd(xbc, small, sz, meta_halo, init_state, cw, cb, dtb, aneg, dsk, ng, blk, q, n_pad):
    b, seq, _ = xbc.shape
    nc = seq // blk
    hb = blk // HALO
    full = lambda a: pl.BlockSpec(a.shape, lambda i, c: (0,) * a.ndim)
    return pl.pallas_call(
        functools.partial(_ssd_body, q=q, n_pad=n_pad),
        grid=(b, nc),
        in_specs=[
            pl.BlockSpec((None, blk, CONV_DIM), lambda i, c: (i, c, 0)),
            pl.BlockSpec((None, HALO, CONV_DIM), lambda i, c: (i, jnp.maximum(c * hb - 1, 0), 0)),
            full(meta_halo),
            pl.BlockSpec((None, blk, A_SMALL), lambda i, c: (i, c, 0)),
            pl.BlockSpec((None, blk, D_INNER), lambda i, c: (i, c, 0)),
            full(cw), full(cb), full(dtb), full(aneg), full(dsk), full(ng), full(init_state),
        ],
        out_specs=[
            pl.BlockSpec((None, blk, D_INNER), lambda i, c: (i, c, 0)),
            pl.BlockSpec((None, LANES, D_INNER), lambda i, c: (i, 0, 0)),
        ],
        out_shape=[jax.ShapeDtypeStruct((b, seq, D_INNER), BF16),
                   jax.ShapeDtypeStruct((b, LANES, D_INNER), F32)],
        scratch_shapes=[pltpu.VMEM((LANES, D_INNER), F32),
                        pltpu.VMEM((blk, D_INNER), F32)],
        compiler_params=_cparams(("parallel", "arbitrary")),
        name="ssd",
    )(xbc, xbc, meta_halo, small, sz, cw, cb, dtb, aneg, dsk, ng, init_state)


def _route(logits):
    lane = lax.broadcasted_iota(jnp.int32, logits.shape, 1).astype(F32)
    gl = jnp.where(lane < N_EXPERT_GROUPS, logits, NEG)
    gmax = jnp.max(gl, axis=1, keepdims=True)
    gidx = jnp.min(jnp.where(gl == gmax, lane, float(LANES)), axis=1, keepdims=True)
    p_g = 1.0 / jnp.sum(jnp.exp(gl - gmax), axis=1, keepdims=True)
    lo = EXPERT_LANE0 + EXPERTS_PER_GROUP * gidx
    el = jnp.where((lane >= lo) & (lane < lo + EXPERTS_PER_GROUP), logits, NEG)
    t1 = jnp.max(el, axis=1, keepdims=True)
    i1 = jnp.min(jnp.where(el == t1, lane, float(LANES)), axis=1, keepdims=True)
    el2 = jnp.where(lane == i1, NEG, el)
    t2 = jnp.max(el2, axis=1, keepdims=True)
    i2 = jnp.min(jnp.where(el2 == t2, lane, float(LANES)), axis=1, keepdims=True)
    e21 = jnp.exp(t2 - t1)
    w1 = p_g / (1.0 + e21)
    w2 = w1 * e21
    return lane, i1, i2, w1, w2


RT_E1, RT_E2, RT_R1, RT_R2, RT_W1, RT_W2 = range(6)


def _merge_body(x_ref, oa_ref, os_ref, ga_ref, gs_ref, wa_ref, ws_ref, wo_ref, nf_ref, wr_ref, br_ref,
                h2_ref, v_ref, rt_ref, cnt_out_ref, cnt_ref):
    @pl.when(pl.program_id(0) == 0)
    def _():
        cnt_ref[...] = jnp.zeros_like(cnt_ref)

    merged = (ga_ref[...].astype(F32) * _dot(oa_ref[...], wa_ref[...])
              + gs_ref[...].astype(F32) * _dot(os_ref[...], ws_ref[...]))
    h2 = x_ref[...] + _dot(merged.astype(BF16), wo_ref[...])
    h2_ref[...] = h2
    v = _rms(h2, nf_ref[...])
    _store_token_slabs(v_ref, v)
    lane, i1, i2, w1, w2 = _route(_dot(v.astype(BF16), wr_ref[...]) + br_ref[...])

    tm = lane.shape[0]
    chosen = jnp.where((lane == i1) | (lane == i2), 1.0, 0.0)
    ri = lax.broadcasted_iota(jnp.int32, (tm, tm), 0)
    ci = lax.broadcasted_iota(jnp.int32, (tm, tm), 1)
    earlier = _dot((ri > ci).astype(BF16), chosen.astype(BF16)) + cnt_ref[...]
    r1 = jnp.sum(jnp.where(lane == i1, earlier, 0.0), axis=1, keepdims=True)
    r2 = jnp.sum(jnp.where(lane == i2, earlier, 0.0), axis=1, keepdims=True)
    cnt_ref[...] += jnp.sum(chosen, axis=0, keepdims=True)
    cnt_out_ref[...] = jnp.broadcast_to(cnt_ref[...], cnt_out_ref.shape)

    rec = jnp.zeros_like(lane)
    for col, val in ((RT_E1, i1 - EXPERT_LANE0), (RT_E2, i2 - EXPERT_LANE0), (RT_R1, r1), (RT_R2, r2),
                     (RT_W1, w1), (RT_W2, w2)):
        rec = jnp.where(lane == col, val, rec)
    rt_ref[...] = rec


def _merge(x2d, oa, osm, ga, gs, wa, ws, wo, nf, wr, br, tm):
    r = x2d.shape[0]
    row = lambda n: pl.BlockSpec((tm, n), lambda i: (i, 0))
    full = lambda a: pl.BlockSpec(a.shape, lambda i: (0, 0))
    return pl.pallas_call(
        _merge_body,
        grid=(r // tm,),
        in_specs=[row(D_MODEL)] * 5 + [full(wa), full(ws), full(wo), full(nf), full(wr), full(br)],
        out_specs=[row(D_MODEL), pl.BlockSpec((tm * SLAB, LANES), lambda i: (i, 0)), row(LANES),
                   pl.BlockSpec((8, LANES), lambda i: (0, 0))],
        out_shape=[jax.ShapeDtypeStruct((r, D_MODEL), F32),
                   jax.ShapeDtypeStruct((r * SLAB, LANES), F32),
                   jax.ShapeDtypeStruct((r, LANES), F32),
                   jax.ShapeDtypeStruct((8, LANES), F32)],
        scratch_shapes=[pltpu.VMEM((1, LANES), F32)],
        compiler_params=_cparams(("arbitrary",)),
        name="merge",
    )(x2d, oa, osm, ga, gs, wa, ws, wo, nf, wr, br)


SLAB = D_MODEL // LANES


def _store_token_slabs(ref, x):
    tm = x.shape[0]
    for s in range(SLAB):
        ref[pl.ds(s, tm, stride=SLAB), :] = x[:, s * LANES:(s + 1) * LANES]


def _load_token_slabs(ref):
    tm = ref.shape[0] // SLAB
    return jnp.concatenate([ref[pl.ds(s, tm, stride=SLAB), :] for s in range(SLAB)], axis=1)


def _row_copy(src_ref, src_row, dst_ref, dst_row, sem):
    src = src_ref.at[pl.ds(pl.multiple_of(src_row * SLAB, SLAB), SLAB)]
    dst = dst_ref.at[pl.ds(pl.multiple_of(dst_row * SLAB, SLAB), SLAB)]
    return pltpu.make_async_copy(src, dst, sem)


def _dispatch_body(pos_ref, v_ref, xs_ref, sem):
    tm = v_ref.shape[0] // SLAB

    def issue(t, _):
        _row_copy(v_ref, t, xs_ref, pos_ref[0, t], sem).start(priority=0)
        _row_copy(v_ref, t, xs_ref, pos_ref[1, t], sem).start(priority=1)
        return 0

    lax.fori_loop(0, tm, issue, 0, unroll=8)
    for _ in range(2):
        pltpu.make_async_copy(v_ref, xs_ref.at[pl.ds(0, tm * SLAB)], sem).wait()


def _slot_spec(tm):
    return pl.BlockSpec((8, tm), lambda i: (0, i), memory_space=pltpu.SMEM)


def _dispatch(pos, v, tm):
    r = v.shape[0] // SLAB
    return pl.pallas_call(
        _dispatch_body,
        grid=(r // tm,),
        in_specs=[_slot_spec(tm), pl.BlockSpec((tm * SLAB, LANES), lambda i: (i, 0))],
        out_specs=pl.BlockSpec(memory_space=pl.ANY),
        out_shape=jax.ShapeDtypeStruct((2 * r * SLAB, LANES), F32),
        scratch_shapes=[pltpu.SemaphoreType.DMA],
        compiler_params=_cparams(("arbitrary",)),
        name="moe_dispatch",
    )(pos, v)


def _experts_body(tile_ref, exp_ref, off_ref, nwork_ref, xs_ref, wg_ref, wu_ref, wd_ref, ys_ref,
                  wg_s, wu_s, wd_s):
    w = pl.program_id(0)
    tme = xs_ref.shape[0] // SLAB

    @pl.when(w < nwork_ref[0])
    def _():
        e = exp_ref[w]
        tile = tile_ref[w]
        prev = jnp.maximum(w - 1, 0)
        new_expert = (w == 0) | (exp_ref[prev] != e)
        new_tile = (w == 0) | (tile_ref[prev] != tile)

        @pl.when(new_expert)
        def _():
            wg_s[...] = wg_ref[...].astype(BF16)
            wu_s[...] = wu_ref[...].astype(BF16)
            wd_s[...] = wd_ref[...].astype(BF16)

        x = _load_token_slabs(xs_ref).astype(BF16)
        g = _dot(x, wg_s[...])
        u = _dot(x, wu_s[...])
        y = _dot((g * _sigmoid(g) * u).astype(BF16), wd_s[...])
        rows = tile * tme + lax.broadcasted_iota(jnp.int32, (tme, 1), 0)
        mine = (rows >= off_ref[e]) & (rows < off_ref[e + 1])

        @pl.when(new_tile)
        def _():
            _store_token_slabs(ys_ref, jnp.where(mine, y, 0.0))

        @pl.when(jnp.logical_not(new_tile))
        def _():
            _store_token_slabs(ys_ref, jnp.where(mine, y, _load_token_slabs(ys_ref)))


def _experts(tile_of, exp_of, offs, nwork, xs, wg, wu, wd, tme):
    nw = tile_of.shape[0]
    grid_spec = pltpu.PrefetchScalarGridSpec(
        num_scalar_prefetch=4,
        grid=(nw,),
        in_specs=[
            pl.BlockSpec((tme * SLAB, LANES), lambda w, t, e, o, n: (t[w], 0)),
            pl.BlockSpec((None, D_MODEL, D_EXPERT), lambda w, t, e, o, n: (e[w], 0, 0)),
            pl.BlockSpec((None, D_MODEL, D_EXPERT), lambda w, t, e, o, n: (e[w], 0, 0)),
            pl.BlockSpec((None, D_EXPERT, D_MODEL), lambda w, t, e, o, n: (e[w], 0, 0)),
        ],
        out_specs=pl.BlockSpec((tme * SLAB, LANES), lambda w, t, e, o, n: (t[w], 0)),
        scratch_shapes=[pltpu.VMEM((D_MODEL, D_EXPERT), BF16), pltpu.VMEM((D_MODEL, D_EXPERT), BF16),
                        pltpu.VMEM((D_EXPERT, D_MODEL), BF16)],
    )
    return pl.pallas_call(
        _experts_body,
        grid_spec=grid_spec,
        out_shape=jax.ShapeDtypeStruct(xs.shape, F32),
        compiler_params=_cparams(("arbitrary",)),
        name="moe_experts",
    )(tile_of, exp_of, offs, nwork, xs, wg, wu, wd)


def _combine_body(pos_ref, pos_next_ref, rt_ref, h2_ref, nf_ref, ys_ref, out_ref, y1_ref, y2_ref, sem):
    i = pl.program_id(0)
    n = pl.num_programs(0)
    tm = h2_ref.shape[0]
    slot = i % 2

    def issue_tile(p_ref, s):
        def issue(t, _):
            _row_copy(ys_ref, p_ref[0, t], y1_ref.at[s], t, sem.at[s]).start(priority=0)
            _row_copy(ys_ref, p_ref[1, t], y2_ref.at[s], t, sem.at[s]).start(priority=1)
            return 0

        lax.fori_loop(0, tm, issue, 0, unroll=8)

    @pl.when(i == 0)
    def _():
        issue_tile(pos_ref, 0)

    @pl.when(i + 1 < n)
    def _():
        issue_tile(pos_next_ref, 1 - slot)

    for buf in (y1_ref, y2_ref):
        pltpu.make_async_copy(ys_ref.at[pl.ds(0, tm * SLAB)], buf.at[slot], sem.at[slot]).wait()
    lane = lax.broadcasted_iota(jnp.int32, rt_ref.shape, 1)
    rt = rt_ref[...]
    w1 = jnp.sum(jnp.where(lane == RT_W1, rt, 0.0), axis=1, keepdims=True)
    w2 = jnp.sum(jnp.where(lane == RT_W2, rt, 0.0), axis=1, keepdims=True)
    moe = w1 * _load_token_slabs(y1_ref.at[slot]) + w2 * _load_token_slabs(y2_ref.at[slot])
    out_ref[...] = _rms(h2_ref[...] + moe, nf_ref[...])


def _combine(pos, rt, h2, nf, ys, tm):
    r = h2.shape[0]
    last = r // tm - 1
    next_slots = pl.BlockSpec((8, tm), lambda i: (0, jnp.minimum(i + 1, last)), memory_space=pltpu.SMEM)
    return pl.pallas_call(
        _combine_body,
        grid=(r // tm,),
        in_specs=[_slot_spec(tm), next_slots, pl.BlockSpec((tm, LANES), lambda i: (i, 0)),
                  pl.BlockSpec((tm, D_MODEL), lambda i: (i, 0)), pl.BlockSpec((1, D_MODEL), lambda i: (0, 0)),
                  pl.BlockSpec(memory_space=pl.ANY)],
        out_specs=pl.BlockSpec((tm, D_MODEL), lambda i: (i, 0)),
        out_shape=jax.ShapeDtypeStruct((r, D_MODEL), F32),
        scratch_shapes=[pltpu.VMEM((2, tm * SLAB, LANES), F32), pltpu.VMEM((2, tm * SLAB, LANES), F32),
                        pltpu.SemaphoreType.DMA((2,))],
        compiler_params=_cparams(("arbitrary",)),
        name="moe_combine",
    )(pos, pos, rt, h2, nf, ys)


def _slots_body(rt_ref, cnt_ref, pos_ref):
    rt = rt_ref[...]
    lane = lax.broadcasted_iota(jnp.int32, rt.shape, 1).astype(F32)
    col = lambda c: jnp.sum(jnp.where(lane == c, rt, 0.0), axis=1, keepdims=True)
    ri = lax.broadcasted_iota(jnp.int32, (LANES, LANES), 0)
    ci = lax.broadcasted_iota(jnp.int32, (LANES, LANES), 1)
    first = _dot_exact_rhs(cnt_ref[...], (ri < ci).astype(BF16))[0:1, :]
    slot = lambda e, r: col(r) + jnp.sum(jnp.where(lane == col(e) + EXPERT_LANE0, first, 0.0),
                                         axis=1, keepdims=True)
    rec = jnp.where(lane == 0, slot(RT_E1, RT_R1), jnp.where(lane == 1, slot(RT_E2, RT_R2), 0.0))
    sr = lax.broadcasted_iota(jnp.int32, (8, LANES), 0)
    sc = lax.broadcasted_iota(jnp.int32, (8, LANES), 1)
    sel = (sr == sc).astype(BF16)
    pos_ref[...] = sum(_dot_nt(sel, part) for part in _split3(rec)).astype(jnp.int32)


def _slots(rt, counts, tm):
    r = rt.shape[0]
    return pl.pallas_call(
        _slots_body,
        grid=(r // tm,),
        in_specs=[pl.BlockSpec((tm, LANES), lambda i: (i, 0)), pl.BlockSpec((8, LANES), lambda i: (0, 0))],
        out_specs=pl.BlockSpec((8, tm), lambda i: (0, i)),
        out_shape=jax.ShapeDtypeStruct((8, r), jnp.int32),
        compiler_params=_cparams(("parallel",)),
        name="moe_slots",
    )(rt, counts)


def _moe_plan(counts, rows, tme):
    cnt = counts.astype(jnp.int32)
    ends = jnp.cumsum(cnt)
    offs = jnp.concatenate([jnp.zeros((1,), jnp.int32), ends])
    first_tile = offs[:-1] // tme
    last_tile = (ends - 1) // tme
    n_items = jnp.where(cnt > 0, last_tile - first_tile + 1, 0)
    item_end = jnp.cumsum(n_items)
    nwork = item_end[-1]
    nw = 2 * rows // tme + N_EXPERTS - 1
    w = jnp.arange(nw, dtype=jnp.int32)
    wc = jnp.minimum(w, nwork - 1)
    exp_of = jnp.sum((wc[:, None] >= item_end[None, :]).astype(jnp.int32), axis=1)
    onehot = (exp_of[:, None] == jnp.arange(N_EXPERTS, dtype=jnp.int32)[None, :]).astype(jnp.int32)
    item_start = jnp.sum(onehot * (item_end - n_items)[None, :], axis=1)
    tile_of = (jnp.sum(onehot * first_tile[None, :], axis=1) + (wc - item_start)).astype(jnp.int32)
    return tile_of, exp_of, offs, nwork.reshape(1).astype(jnp.int32)


def _prep_w_in(w):
    o = np.cumsum([0, Q_RANK, KV_RANK, ROPE_DIM, D_INNER, CONV_DIM, SSM_HEADS, D_MODEL, D_MODEL])
    cq, ckv, kr, z, xbc, dt, ga, gs = (w[:, o[i]:o[i + 1]] for i in range(8))
    hr = ROPE_DIM // 2
    zeros = lambda n: jnp.zeros((w.shape[0], n), w.dtype)
    kra = jnp.concatenate([zeros(NOPE_DIM), kr, zeros(HEAD_SLOT - NOPE_DIM - ROPE_DIM)], axis=1)
    krb = jnp.concatenate([zeros(NOPE_DIM), -kr[:, hr:], kr[:, :hr], dt,
                           zeros(HEAD_SLOT - DT_LANE0 - SSM_HEADS)], axis=1)
    small = jnp.concatenate([kra, krb], axis=1)
    return tuple(a.astype(BF16) for a in (cq, ckv, small, z, xbc, ga, gs))


def _prep_w_uq(w):
    w = w.reshape(Q_RANK, MLA_HEADS, NOPE_DIM + ROPE_DIM)
    hr = ROPE_DIM // 2
    nope, r1, r2 = w[..., :NOPE_DIM], w[..., NOPE_DIM:NOPE_DIM + hr], w[..., NOPE_DIM + hr:]
    z = jnp.zeros((Q_RANK, MLA_HEADS, HEAD_SLOT - NOPE_DIM - ROPE_DIM), w.dtype)
    wa = jnp.concatenate([nope, r1, r2, z], axis=-1).reshape(Q_RANK, -1)
    return wa.T.astype(BF16)


def _prep_w_ukv(w):
    w = w.reshape(KV_RANK, MLA_HEADS, NOPE_DIM + V_DIM)
    kn, v = w[..., :NOPE_DIM], w[..., NOPE_DIM:]
    wk = jnp.concatenate([kn, jnp.zeros((KV_RANK, MLA_HEADS, HEAD_SLOT - NOPE_DIM), w.dtype)], axis=-1)
    vslot = jnp.concatenate([v, jnp.zeros((KV_RANK, MLA_HEADS, V_SLOT - V_DIM), w.dtype)], axis=-1)
    return wk.reshape(KV_RANK, -1).astype(BF16), vslot.reshape(KV_RANK, -1).T.astype(BF16)


def _rope_tables(pos):
    inv = ROPE_THETA ** (-jnp.arange(0, ROPE_DIM, 2, dtype=F32) / ROPE_DIM)
    ang = pos.astype(F32)[:, None] * inv[None, :]
    cos, sin = jnp.cos(ang), jnp.sin(ang)
    n = pos.shape[0]
    pad = jnp.zeros((n, HEAD_SLOT - NOPE_DIM - ROPE_DIM), F32)
    scale = (NOPE_DIM + ROPE_DIM) ** -0.5 * float(np.log2(np.e))
    cq = jnp.concatenate([jnp.ones((n, NOPE_DIM), F32), cos, cos, pad], axis=1) * scale
    sq = jnp.concatenate([jnp.zeros((n, NOPE_DIM), F32), sin, sin, pad], axis=1) * scale
    ck = jnp.concatenate([jnp.zeros((n, NOPE_DIM), F32), cos, cos, pad], axis=1)
    sk = jnp.concatenate([jnp.zeros((n, NOPE_DIM), F32), sin, sin, pad], axis=1)
    return cq.T, sq.T, ck, sk


def _head_lanes(v):
    return jnp.zeros((1, LANES), F32).at[0, DT_LANE0:DT_LANE0 + SSM_HEADS].set(v.astype(F32))


def _pick(n, prefs):
    for t in prefs:
        if n % t == 0:
            return t
    raise ValueError(f"no tile for {n}")


def kernel(x, meta_tokens, norm_mix, w_in, mla_q_norm, mla_w_uq, mla_kv_norm, mla_w_ukv, ssm_conv_w, ssm_conv_b, ssm_dt_bias, ssm_a_log, ssm_d_skip, ssm_norm, w_branch_attn, w_branch_ssm, w_out, norm_ffn, moe_w_group, moe_b_group, moe_w_expert, moe_b_expert, moe_w_gate, moe_w_up, moe_w_down, norm_final):
    b, seq, _ = x.shape
    assert w_in.shape[0] == 1, "one layer"
    rows = b * seq
    x2d = x.reshape(rows, D_MODEL)
    tm = _pick(seq, (512, 256, 128))
    tq = _pick(seq, (512, 256, 128))
    tk = _pick(tq, (256, 128))
    chunk = 128
    ssd_blk = _pick(seq, (512, 256, 128))

    w_in_r = _prep_w_in(w_in[0])
    wq = _prep_w_uq(mla_w_uq[0])
    wk, wvt = _prep_w_ukv(mla_w_ukv[0])
    g_mix = norm_mix[0].reshape(1, D_MODEL)
    qn = mla_q_norm[0].reshape(1, Q_RANK)
    kvn = mla_kv_norm[0].reshape(1, KV_RANK)
    cw = ssm_conv_w[0]
    cb = ssm_conv_b[0].reshape(1, CONV_DIM)
    dtb = _head_lanes(ssm_dt_bias[0])
    aneg = _head_lanes(-jnp.exp(ssm_a_log[0].astype(F32)))
    dsk = jnp.repeat(ssm_d_skip[0].astype(F32), SSM_HEAD_DIM).reshape(1, D_INNER)
    ng = ssm_norm[0].reshape(1, D_INNER)
    wa = w_branch_attn[0].astype(BF16)
    ws = w_branch_ssm[0].astype(BF16)
    wo = w_out[0].astype(BF16)
    nffn = norm_ffn[0].reshape(1, D_MODEL)
    wr = jnp.zeros((D_MODEL, LANES), F32)
    wr = wr.at[:, :N_EXPERT_GROUPS].set(moe_w_group[0]).at[:, EXPERT_LANE0:EXPERT_LANE0 + N_EXPERTS].set(moe_w_expert[0])
    wr = wr.astype(BF16)
    br = jnp.zeros((1, LANES), F32)
    br = br.at[0, :N_EXPERT_GROUPS].set(moe_b_group[0]).at[0, EXPERT_LANE0:EXPERT_LANE0 + N_EXPERTS].set(moe_b_expert[0])
    wg, wu, wd = moe_w_gate[0], moe_w_up[0], moe_w_down[0]
    nfin = norm_final.reshape(1, D_MODEL)

    meta_blk = jnp.concatenate([jnp.zeros((META_PAD, D_MODEL), F32), meta_tokens.astype(F32)], axis=0)
    m_cq, m_ckv, m_small, m_sz, m_xbc, _, _ = _inproj(meta_blk, g_mix, w_in_r, META_BLOCK)
    meta_pos = jnp.maximum(jnp.arange(META_BLOCK) - META_PAD, 0)
    _, kmeta, vmeta_t = _mla_prep(m_cq, m_ckv, m_small, qn, kvn, wq, wk, wvt, _rope_tables(meta_pos),
                                  META_BLOCK, 1, META_BLOCK, META_BLOCK)
    zero_state = jnp.zeros((LANES, D_INNER), F32)
    zero_halo = jnp.zeros((HALO, CONV_DIM), BF16)
    _, meta_state = _ssd(m_xbc[None], m_small[None], m_sz[None], zero_halo, zero_state,
                         cw, cb, dtb, aneg, dsk, ng, META_BLOCK, META_BLOCK, META_PAD)
    meta_halo = m_xbc[META_BLOCK - HALO:]

    cq, ckv, small, sz, xbc, ga, gs = _inproj(x2d, g_mix, w_in_r, tm)
    qt, k, vt = _mla_prep(cq, ckv, small, qn, kvn, wq, wk, wvt, _rope_tables(N_META + jnp.arange(seq)),
                          tm, seq // tm, tq, tk)
    o_attn = _flash(qt.reshape(b, seq // tq, -1, tq), k.reshape(b, seq, -1), vt.reshape(b, seq // tk, -1, tk),
                    kmeta, jnp.pad(vmeta_t[0], ((0, 0), (0, tk - META_BLOCK))))
    o_ssm, _ = _ssd(xbc.reshape(b, seq, -1), small.reshape(b, seq, -1), sz.reshape(b, seq, -1), meta_halo,
                    meta_state[0], cw, cb, dtb, aneg, dsk, ng, ssd_blk, chunk, 0)
    h2, v, rt, counts = _merge(x2d, o_attn.reshape(rows, -1), o_ssm.reshape(rows, -1), ga, gs, wa, ws, wo,
                               nffn, wr, br, tm)
    tme = _pick(2 * rows, (512, 256, 128))
    tile_of, exp_of, offs, nwork = _moe_plan(counts[0, EXPERT_LANE0:EXPERT_LANE0 + N_EXPERTS], rows, tme)
    tmd = _pick(rows, (1024, 512, 256, 128))
    pos = _slots(rt, counts, tmd)
    xs = _dispatch(pos, v, tmd)
    ys = _experts(tile_of, exp_of, offs, nwork, xs, wg, wu, wd, tme)
    out = _combine(pos, rt, h2, nfin, ys, tm)
    return out.reshape(b, seq, D_MODEL)
```
